```python
import math
import jax, jax.numpy as jnp
from jax import lax
import numpy as np

D_MODEL = 1024
BATCH = 8
SEQ = 2048
DEPTH = 2
DEC_BATCH = 128
DEC_SEQ = 1
PAST_LEN = 16384
PAGE_SIZE = 128

EPS = 1e-6
N_EVEN = (DEPTH + 1) // 2
N_ODD = DEPTH // 2
MEM_LEN = 256
X_HEADS = 4
X_HEAD_DIM = D_MODEL // X_HEADS
A_CHUNK = 128
A_HEADS = 8
A_CH = D_MODEL
A_HEAD_DIM = A_CH // A_HEADS
B_INNER = D_MODEL
B_HEAD_DIM = 64
B_HEADS = B_INNER // B_HEAD_DIM
B_GROUPS = 2
B_HPG = B_HEADS // B_GROUPS
B_STATE = 128
B_CONV = 4
B_CONV_DIM = B_INNER + 2 * B_GROUPS * B_STATE
B_CHUNK = 128
IN0 = 2 * A_CH + B_INNER + B_CONV_DIM + B_HEADS
SPLIT0 = [A_CH, 2 * A_CH, 2 * A_CH + B_INNER, 2 * A_CH + B_INNER + B_CONV_DIM]
MIX0 = A_CH + B_INNER
C_KDIM = 128
C_HEADS = D_MODEL // C_KDIM
C_VDIM = D_MODEL // C_HEADS
C_CHUNK = 64
F_DENSE = 2816
N_EXPERTS = 8
TOP_K = 2
F_EXPERT = 3584

kernel_name = 'hybrid_gmlp_ssd_hgrn2_memxattn_step'


def rmsnorm(x, g):
    xf = x.astype(jnp.float32)
    y = xf * lax.rsqrt(jnp.mean(xf * xf, -1, keepdims=True) + EPS)
    return (y * g.astype(jnp.float32)).astype(x.dtype)


def group_rmsnorm(x, g, n_groups):
    shp = x.shape
    xf = x.astype(jnp.float32).reshape(shp[:-1] + (n_groups, shp[-1] // n_groups))
    y = (xf * lax.rsqrt(jnp.mean(xf * xf, -1, keepdims=True) + EPS)).reshape(shp)
    return (y * g.astype(jnp.float32)).astype(x.dtype)


def layernorm(x, g, b):
    xf = x.astype(jnp.float32)
    xc = xf - jnp.mean(xf, -1, keepdims=True)
    y = xc * lax.rsqrt(jnp.mean(xc * xc, -1, keepdims=True) + EPS)
    return (y * g.astype(jnp.float32) + b.astype(jnp.float32)).astype(x.dtype)


def chunk_spatial_gate(u, v, ws, bs):
    b, T = u.shape[:2]
    L = min(A_CHUNK, T)
    n = T // L
    mask = jnp.tril(jnp.ones((L, L), dtype=bool))
    w = jnp.where(mask, ws[:, :L, :L], 0.0).astype(v.dtype)
    vc = v.reshape(b, n, L, A_HEADS, A_HEAD_DIM)
    mix = jnp.einsum('hts,bnshd->bnthd', w, vc) + bs[:, :L].T.astype(v.dtype)[None, None, :, :, None]
    return u * mix.reshape(u.shape)


def causal_dwconv(x, prefix, w, bias):
    xp = jnp.concatenate([prefix.astype(x.dtype), x], axis=1)
    out = lax.conv_general_dilated(xp, w[:, None, :].astype(x.dtype), window_strides=(1,), padding='VALID',
                                   dimension_numbers=('NWC', 'WIO', 'NWC'), feature_group_count=x.shape[-1])
    return out + bias.astype(x.dtype), xp[:, -(B_CONV - 1):]


def ssd_scan(x, dt, a, bmat, cmat, s0):
    f32 = jnp.float32
    b, T = x.shape[:2]
    L = B_CHUNK if T % B_CHUNK == 0 else T
    n = T // L
    xg = x.astype(f32).reshape(b, n, L, B_GROUPS, B_HPG, B_HEAD_DIM)
    dtg = dt.reshape(b, n, L, B_GROUPS, B_HPG)
    xdt = xg * dtg[..., None]
    da = jnp.moveaxis(dtg * a.reshape(B_GROUPS, B_HPG), (3, 4), (1, 2))
    cs = jnp.cumsum(da, axis=-1)
    mask = jnp.tril(jnp.ones((L, L), dtype=bool))
    lmat = jnp.exp(jnp.where(mask, cs[..., :, None] - cs[..., None, :], -jnp.inf))
    Bc = bmat.astype(f32).reshape(b, n, L, B_GROUPS, B_STATE)
    Cc = cmat.astype(f32).reshape(b, n, L, B_GROUPS, B_STATE)
    y_diag = jnp.einsum('bclgn,bcsgn,bghcls,bcsghp->bclghp', Cc, Bc, lmat, xdt)
    decay = jnp.exp(cs[..., -1:] - cs)
    states = jnp.einsum('bclgn,bghcl,bclghp->bcghpn', Bc, decay, xdt)
    chunk_decay = jnp.exp(cs[..., -1])

    def step(s, inp):
        dec, st = inp
        return dec[..., None, None] * s + st, s

    s_final, prev = lax.scan(step, s0.astype(f32), (jnp.moveaxis(chunk_decay, -1, 0), jnp.moveaxis(states, 1, 0)))
    prev = jnp.moveaxis(prev, 0, 1)
    y_off = jnp.einsum('bclgn,bcghpn,bghcl->bclghp', Cc, prev, jnp.exp(cs))
    return (y_diag + y_off).reshape(b, T, B_HEADS, B_HEAD_DIM), s_final


def gla_chunked(q, k, v, logf, s0):
    b, T, H, K = q.shape
    V = v.shape[-1]
    L = C_CHUNK if T % C_CHUNK == 0 else T
    n = T // L
    q, k, v, logf = [t.reshape(b, n, L, H, t.shape[-1]) for t in (q, k, v, logf)]
    bc = jnp.cumsum(logf, axis=2)
    btot = bc[:, :, -1]
    q_in = q * jnp.exp(bc)
    k_in = k * jnp.exp(-bc)
    mask = jnp.tril(jnp.ones((L, L), dtype=bool))
    att = jnp.where(mask, jnp.einsum('bnthk,bnshk->bnhts', q_in, k_in), 0.0)
    o_intra = jnp.einsum('bnhts,bnshv->bnthv', att, v)
    upd = jnp.einsum('bnshk,bnshv->bnhkv', k * jnp.exp(btot[:, :, None] - bc), v)

    def step(s, inp):
        dec, u = inp
        return jnp.exp(dec)[..., None] * s + u, s

    s_final, prev = lax.scan(step, s0, (jnp.moveaxis(btot, 1, 0), jnp.moveaxis(upd, 1, 0)))
    prev = jnp.moveaxis(prev, 0, 1)
    o_inter = jnp.einsum('bnthk,bnhkv->bnthv', q_in, prev)
    return (o_intra + o_inter).reshape(b, T, H, V), s_final


def mixer_even(h, conv0, ssm0, w_in, a_ws, a_bs, a_ln_g, a_ln_b, conv_w, conv_b, dt_bias, a_log, d_skip, b_norm, w_out):
    f32 = jnp.float32
    b, T, _ = h.shape
    u, v, z, xbc, dt = jnp.split(h @ w_in, SPLIT0, axis=-1)
    u = jax.nn.gelu(u)
    v = layernorm(jax.nn.gelu(v), a_ln_g, a_ln_b)
    ya = chunk_spatial_gate(u.reshape(b, T, A_HEADS, A_HEAD_DIM), v.reshape(b, T, A_HEADS, A_HEAD_DIM), a_ws, a_bs)
    ya = ya.reshape(b, T, A_CH)
    xbc, conv_new = causal_dwconv(xbc, conv0, conv_w, conv_b)
    xbc = jax.nn.silu(xbc)
    xs, bm, cm = jnp.split(xbc, [B_INNER, B_INNER + B_GROUPS * B_STATE], axis=-1)
    dtf = jax.nn.softplus(dt.astype(f32) + dt_bias.astype(f32))
    a = -jnp.exp(a_log.astype(f32))
    xs4 = xs.reshape(b, T, B_HEADS, B_HEAD_DIM)
    y, ssm_new = ssd_scan(xs4, dtf, a, bm.reshape(b, T, B_GROUPS, B_STATE), cm.reshape(b, T, B_GROUPS, B_STATE), ssm0)
    y = (y + d_skip.astype(f32)[:, None] * xs4.astype(f32)).reshape(b, T, B_INNER).astype(h.dtype)
    yb = group_rmsnorm(y * jax.nn.silu(z), b_norm, B_GROUPS)
    out = jnp.concatenate([ya, yb], axis=-1) @ w_out
    return out, conv_new, ssm_new, v


def mixer_odd(h, s0, lb, w_in, c_norm, w_out):
    f32 = jnp.float32
    b, T, _ = h.shape
    q, f, i, g = jnp.split(h @ w_in, 4, axis=-1)
    fg = lb + (1.0 - lb) * jax.nn.sigmoid(f.astype(f32))
    qh = jax.nn.silu(q.astype(f32)).reshape(b, T, C_HEADS, C_KDIM)
    kh = (1.0 - fg).reshape(b, T, C_HEADS, C_KDIM)
    lf = jnp.log(fg).reshape(b, T, C_HEADS, C_KDIM)
    vh = i.astype(f32).reshape(b, T, C_HEADS, C_VDIM)
    o, s_new = gla_chunked(qh, kh, vh, lf, s0.astype(f32))
    o = group_rmsnorm(o.reshape(b, T, D_MODEL), c_norm, C_HEADS).astype(h.dtype) * jax.nn.silu(g)
    return o @ w_out, s_new


def mem_kv(mem, g, w):
    b = mem.shape[0]
    return (rmsnorm(mem, g) @ w).reshape(b, MEM_LEN, X_HEADS, X_HEAD_DIM)


def cross_attend(h, k, v, wq, wo):
    b, T, _ = h.shape
    q = (h @ wq).reshape(b, T, X_HEADS, X_HEAD_DIM)
    s = jnp.einsum('bthd,bmhd->bhtm', q, k.astype(q.dtype)).astype(jnp.float32) * (X_HEAD_DIM ** -0.5)
    p = jax.nn.softmax(s, axis=-1).astype(h.dtype)
    o = jnp.einsum('bhtm,bmhd->bthd', p, v.astype(h.dtype)).reshape(b, T, D_MODEL)
    return o @ wo


def swiglu(h, wg, wu, wd):
    return (jax.nn.silu(h @ wg) * (h @ wu)) @ wd


def moe_swiglu(h, router, wg, wu, wd):
    logits = (h @ router).astype(jnp.float32)
    top_v, top_i = lax.top_k(logits, TOP_K)
    w = jax.nn.softmax(top_v, axis=-1)
    gates = jnp.sum(jax.nn.one_hot(top_i, N_EXPERTS, dtype=jnp.float32) * w[..., None], axis=-2).astype(h.dtype)
    out = jnp.zeros_like(h)
    for e in range(N_EXPERTS):
        out = out + gates[..., e:e + 1] * swiglu(h, wg[e], wu[e], wd[e])
    return out


def _trunk(x, mem_k, mem_v, conv0, ssm0, hgrn0, lower_bounds, P):
    convs, ssms, hgrns, avs = [], [], [], []
    for l in range(DEPTH):
        j = l // 2
        h = rmsnorm(x, P['norm_mix_pre'][l])
        if l % 2 == 0:
            m, c_new, s_new, v_rows = mixer_even(
                h, conv0[j], ssm0[j], P['ev_w_in'][j], P['a_ws'][j], P['a_bs'][j], P['a_ln_g'][j], P['a_ln_b'][j],
                P['b_conv_w'][j], P['b_conv_b'][j], P['b_dt_bias'][j], P['b_a_log'][j], P['b_d'][j], P['b_norm'][j],
                P['ev_w_out'][j])
            convs.append(c_new)
            ssms.append(s_new)
            avs.append(v_rows)
        else:
            m, s_new = mixer_odd(h, hgrn0[j], lower_bounds[l], P['od_w_in'][j], P['c_norm'][j], P['od_w_out'][j])
            hgrns.append(s_new)
        x = x + rmsnorm(m, P['norm_mix_post'][l])
        h = rmsnorm(x, P['norm_x_pre'][l])
        x = x + rmsnorm(cross_attend(h, mem_k[l], mem_v[l], P['xa_wq'][l], P['xa_wo'][l]), P['norm_x_post'][l])
        h = rmsnorm(x, P['norm_ffn_pre'][l])
        if l % 2 == 0:
            f = swiglu(h, P['ffn_w_gate'][j], P['ffn_w_up'][j], P['ffn_w_down'][j])
        else:
            f = moe_swiglu(h, P['moe_router'][j], P['moe_w_gate'][j], P['moe_w_up'][j], P['moe_w_down'][j])
        x = x + rmsnorm(f, P['norm_ffn_post'][l])
    dt = x.dtype
    return x, jnp.stack(convs).astype(dt), jnp.stack(ssms).astype(dt), jnp.stack(hgrns).astype(dt), jnp.stack(avs).astype(dt)


def setup_inputs(seed: int = 0) -> dict:
    key = jax.random.key(seed)
    ks = iter(jax.random.split(key, 64))
    f32 = jnp.float32

    def nrm(shape, scale=1.0):
        return scale * jax.random.normal(next(ks), shape, f32)

    def gain(shape):
        return 1.0 + nrm(shape, 0.05)

    d = D_MODEL
    inp = {}
    inp['x_prompt'] = nrm((BATCH, SEQ, d))
    inp['x_sample'] = nrm((DEC_BATCH, DEC_SEQ, d))
    inp['mem_prompt'] = nrm((BATCH, MEM_LEN, d))
    inp['cache_mem_k'] = nrm((DEPTH, DEC_BATCH, MEM_LEN, X_HEADS, X_HEAD_DIM))
    inp['cache_mem_v'] = nrm((DEPTH, DEC_BATCH, MEM_LEN, X_HEADS, X_HEAD_DIM))
    inp['state_conv'] = nrm((N_EVEN, DEC_BATCH, B_CONV - 1, B_CONV_DIM))
    inp['state_ssm'] = nrm((N_EVEN, DEC_BATCH, B_GROUPS, B_HPG, B_HEAD_DIM, B_STATE), 0.3)
    inp['state_hgrn'] = nrm((N_ODD, DEC_BATCH, C_HEADS, C_KDIM, C_VDIM), 0.3)
    inp['norm_mix_pre'] = gain((DEPTH, d))
    inp['norm_mix_post'] = gain((DEPTH, d))
    inp['norm_x_pre'] = gain((DEPTH, d))
    inp['norm_x_post'] = gain((DEPTH, d))
    inp['norm_ffn_pre'] = gain((DEPTH, d))
    inp['norm_ffn_post'] = gain((DEPTH, d))
    inp['norm_mem'] = gain((DEPTH, d))
    inp['xa_wq'] = nrm((DEPTH, d, d), d ** -0.5)
    inp['xa_wk'] = nrm((DEPTH, d, d), d ** -0.5)
    inp['xa_wv'] = nrm((DEPTH, d, d), d ** -0.5)
    inp['xa_wo'] = nrm((DEPTH, d, d), d ** -0.5)
    inp['ev_w_in'] = nrm((N_EVEN, d, IN0), d ** -0.5)
    inp['a_ws'] = nrm((N_EVEN, A_HEADS, A_CHUNK, A_CHUNK), A_CHUNK ** -0.5)
    inp['a_bs'] = 1.0 + nrm((N_EVEN, A_HEADS, A_CHUNK), 0.1)
    inp['a_ln_g'] = gain((N_EVEN, A_CH))
    inp['a_ln_b'] = nrm((N_EVEN, A_CH), 0.02)
    inp['b_conv_w'] = nrm((N_EVEN, B_CONV, B_CONV_DIM), B_CONV ** -0.5)
    inp['b_conv_b'] = nrm((N_EVEN, B_CONV_DIM), 0.02)
    dt0 = jnp.exp(jax.random.uniform(next(ks), (N_EVEN, B_HEADS), f32, math.log(1e-3), math.log(1e-1)))
    inp['b_dt_bias'] = dt0 + jnp.log(-jnp.expm1(-dt0))
    inp['b_a_log'] = jnp.log(jax.random.uniform(next(ks), (N_EVEN, B_HEADS), f32, 1.0, 16.0))
    inp['b_d'] = gain((N_EVEN, B_HEADS))
    inp['b_norm'] = gain((N_EVEN, B_INNER))
    inp['ev_w_out'] = nrm((N_EVEN, MIX0, d), MIX0 ** -0.5)
    inp['ffn_w_gate'] = nrm((N_EVEN, d, F_DENSE), d ** -0.5)
    inp['ffn_w_up'] = nrm((N_EVEN, d, F_DENSE), d ** -0.5)
    inp['ffn_w_down'] = nrm((N_EVEN, F_DENSE, d), F_DENSE ** -0.5)
    inp['od_w_in'] = nrm((N_ODD, d, 4 * d), d ** -0.5)
    inp['hgrn_lb_logits'] = nrm((DEPTH, d), 0.02)
    inp['c_norm'] = gain((N_ODD, d))
    inp['od_w_out'] = nrm((N_ODD, d, d), d ** -0.5)
    inp['moe_router'] = nrm((N_ODD, d, N_EXPERTS), d ** -0.5)
    inp['moe_w_gate'] = nrm((N_ODD, N_EXPERTS, d, F_EXPERT), d ** -0.5)
    inp['moe_w_up'] = nrm((N_ODD, N_EXPERTS, d, F_EXPERT), d ** -0.5)
    inp['moe_w_down'] = nrm((N_ODD, N_EXPERTS, F_EXPERT, d), F_EXPERT ** -0.5)
    return inp


def reference(x_prompt, x_sample, mem_prompt, cache_mem_k, cache_mem_v, state_conv, state_ssm, state_hgrn,
              norm_mix_pre, norm_mix_post, norm_x_pre, norm_x_post, norm_ffn_pre, norm_ffn_post, norm_mem,
              xa_wq, xa_wk, xa_wv, xa_wo,
              ev_w_in, a_ws, a_bs, a_ln_g, a_ln_b, b_conv_w, b_conv_b, b_dt_bias, b_a_log, b_d, b_norm, ev_w_out,
              ffn_w_gate, ffn_w_up, ffn_w_down,
              od_w_in, hgrn_lb_logits, c_norm, od_w_out,
              moe_router, moe_w_gate, moe_w_up, moe_w_down):
    P = dict(norm_mix_pre=norm_mix_pre, norm_mix_post=norm_mix_post, norm_x_pre=norm_x_pre, norm_x_post=norm_x_post,
             norm_ffn_pre=norm_ffn_pre, norm_ffn_post=norm_ffn_post, xa_wq=xa_wq, xa_wo=xa_wo,
             ev_w_in=ev_w_in, a_ws=a_ws, a_bs=a_bs, a_ln_g=a_ln_g, a_ln_b=a_ln_b, b_conv_w=b_conv_w, b_conv_b=b_conv_b,
             b_dt_bias=b_dt_bias, b_a_log=b_a_log, b_d=b_d, b_norm=b_norm, ev_w_out=ev_w_out,
             ffn_w_gate=ffn_w_gate, ffn_w_up=ffn_w_up, ffn_w_down=ffn_w_down,
             od_w_in=od_w_in, c_norm=c_norm, od_w_out=od_w_out,
             moe_router=moe_router, moe_w_gate=moe_w_gate, moe_w_up=moe_w_up, moe_w_down=moe_w_down)
    lbp = jax.nn.softmax(hgrn_lb_logits.astype(jnp.float32), axis=0)
    lower_bounds = jnp.cumsum(lbp, axis=0) - lbp[0]

    bp = x_prompt.shape[0]
    dtp = x_prompt.dtype
    mem_k_prompt = jnp.stack([mem_kv(mem_prompt, norm_mem[l], xa_wk[l]) for l in range(DEPTH)])
    mem_v_prompt = jnp.stack([mem_kv(mem_prompt, norm_mem[l], xa_wv[l]) for l in range(DEPTH)])
    conv0 = jnp.zeros((N_EVEN, bp, B_CONV - 1, B_CONV_DIM), dtp)
    ssm0 = jnp.zeros((N_EVEN, bp, B_GROUPS, B_HPG, B_HEAD_DIM, B_STATE), dtp)
    hgrn0 = jnp.zeros((N_ODD, bp, C_HEADS, C_KDIM, C_VDIM), dtp)
    y_prompt, conv_prompt, ssm_prompt, hgrn_prompt, _ = _trunk(
        x_prompt, mem_k_prompt, mem_v_prompt, conv0, ssm0, hgrn0, lower_bounds, P)

    y_sample, conv_sample, ssm_sample, hgrn_sample, av_sample = _trunk(
        x_sample, cache_mem_k, cache_mem_v, state_conv, state_ssm, state_hgrn, lower_bounds, P)

    return (y_prompt, y_sample, mem_k_prompt, mem_v_prompt, conv_prompt, ssm_prompt, hgrn_prompt,
            conv_sample, ssm_sample, hgrn_sample, av_sample)
```

```python
import functools

import jax
import jax.numpy as jnp
from jax import lax
from jax.experimental import pallas as pl
from jax.experimental.pallas import tpu as pltpu

F32 = jnp.float32
BF16 = jnp.bfloat16
EPS = 1e-6

D = 1024
LANES = 128
SUBLANES = 8
A_HEADS = 8
A_CHUNK = 128
B_HEADS = 16
B_HEAD_DIM = 64
B_GROUPS = 2
B_GROUP_W = 512
B_STATE = 128
B_CONV = 4
B_CONV_DIM = 1536
IN0 = 4624
IN0_PAD = 4736
DT_COL_BLOCK = 36
C_HEADS = 8
C_KDIM = 128
C_CHUNK = 64
X_HEADS = 4
X_HEAD_DIM = 256
MEM_LEN = 256
N_EXPERTS = 8

VMEM_LIMIT = 56 * 1024 * 1024


def _cparams(sem):
    return pltpu.CompilerParams(dimension_semantics=sem, vmem_limit_bytes=VMEM_LIMIT)


def _dot(a, b):
    return jnp.dot(a, b, preferred_element_type=F32)


def _dot_nt(a, b):
    return lax.dot_general(a, b, (((1,), (1,)), ((), ())), preferred_element_type=F32)


def _dot_f32(a, b):
    return jnp.dot(a, b, precision=lax.Precision.HIGHEST, preferred_element_type=F32)


def _dot_nt_f32(a, b):
    return lax.dot_general(a, b, (((1,), (1,)), ((), ())), precision=lax.Precision.HIGHEST,
                           preferred_element_type=F32)


def _mm(a, w):
    if w.dtype == F32:
        return _dot_f32(a.astype(F32), w)
    return _dot(a.astype(BF16), w)


def _rms(x, g):
    return x * lax.rsqrt(jnp.mean(x * x, axis=-1, keepdims=True) + EPS) * g


def _split2(x):
    hi = x.astype(BF16)
    lo = (x - hi.astype(F32)).astype(BF16)
    return hi, lo


def _split3(x):
    hi = x.astype(BF16)
    r = x - hi.astype(F32)
    mid = r.astype(BF16)
    lo = (r - mid.astype(F32)).astype(BF16)
    return hi, mid, lo


def _sel_left(m, x, parts):
    out = None
    for p in parts(x):
        t = _dot(m, p)
        out = t if out is None else out + t
    return out


def _sel_right(x, m, parts):
    out = None
    for p in parts(x):
        t = _dot(p, m)
        out = t if out is None else out + t
    return out


def _rows_to_cols(x):
    n, w = x.shape
    if n < LANES:
        x = jnp.concatenate([x, jnp.zeros((LANES - n, w), x.dtype)], axis=0)
    return x.T


def _softplus(x):
    return jnp.maximum(x, 0.0) + jnp.log1p(jnp.exp(-jnp.abs(x)))


def _layernorm(x, g, b):
    xc = x - jnp.mean(x, axis=-1, keepdims=True)
    return xc * lax.rsqrt(jnp.mean(xc * xc, axis=-1, keepdims=True) + EPS) * g + b


def _group_rms(x, g, width):
    parts = []
    for s in range(0, x.shape[-1], width):
        t = x[:, s:s + width]
        parts.append(t * lax.rsqrt(jnp.mean(t * t, axis=-1, keepdims=True) + EPS))
    return jnp.concatenate(parts, axis=-1) * g


def _norm_matmul_kernel(x_ref, g_ref, w_ref, o_ref):
    o_ref[...] = _mm(_rms(x_ref[...], g_ref[...]), w_ref[...])


def norm_matmul(x, g, w, tm, tn=None):
    M, K = x.shape
    N = w.shape[1]
    tn = N if tn is None else tn
    return pl.pallas_call(
        _norm_matmul_kernel,
        grid=(M // tm, N // tn),
        in_specs=[pl.BlockSpec((tm, K), lambda i, j: (i, 0)),
                  pl.BlockSpec((1, K), lambda i, j: (0, 0)),
                  pl.BlockSpec((K, tn), lambda i, j: (0, j))],
        out_specs=pl.BlockSpec((tm, tn), lambda i, j: (i, j)),
        out_shape=jax.ShapeDtypeStruct((M, N), F32),
        compiler_params=_cparams(("parallel", "parallel")),
        name="norm_matmul",
    )(x, g.reshape(1, K), w)


def _matmul_norm_res_kernel(a_ref, w_ref, g_ref, r_ref, o_ref, acc_ref):
    k = pl.program_id(1)

    @pl.when(k == 0)
    def _():
        acc_ref[...] = jnp.zeros_like(acc_ref)

    acc_ref[...] += _mm(a_ref[...], w_ref[...])

    @pl.when(k == pl.num_programs(1) - 1)
    def _():
        o_ref[...] = r_ref[...] + _rms(acc_ref[...], g_ref[...])


def matmul_norm_res(a, w, g, res, tm, tk=None):
    M, K = a.shape
    N = w.shape[1]
    tk = K if tk is None else tk
    return pl.pallas_call(
        _matmul_norm_res_kernel,
        grid=(M // tm, K // tk),
        in_specs=[pl.BlockSpec((tm, tk), lambda i, k: (i, k)),
                  pl.BlockSpec((tk, N), lambda i, k: (k, 0)),
                  pl.BlockSpec((1, N), lambda i, k: (0, 0)),
                  pl.BlockSpec((tm, N), lambda i, k: (i, 0))],
        out_specs=pl.BlockSpec((tm, N), lambda i, k: (i, 0)),
        out_shape=jax.ShapeDtypeStruct((M, N), F32),
        scratch_shapes=[pltpu.VMEM((tm, N), F32)],
        compiler_params=_cparams(("parallel", "arbitrary")),
        name="matmul_norm_res",
    )(a, w, g.reshape(1, N), res)


def _top2_gates(h, router):
    lg = _dot_f32(h, router)
    lane = lax.broadcasted_iota(jnp.int32, lg.shape, 1)
    lg = jnp.where(lane < N_EXPERTS, lg, -jnp.inf)
    m1 = jnp.max(lg, axis=-1, keepdims=True)
    i1 = jnp.min(jnp.where(lg == m1, lane, LANES), axis=-1, keepdims=True)
    lg2 = jnp.where(lane == i1, -jnp.inf, lg)
    m2 = jnp.max(lg2, axis=-1, keepdims=True)
    i2 = jnp.min(jnp.where(lg2 == m2, lane, LANES), axis=-1, keepdims=True)
    e2 = jnp.exp(m2 - m1)
    den = 1.0 + e2
    return jnp.where(lane == i1, 1.0 / den, 0.0) + jnp.where(lane == i2, e2 / den, 0.0)


def _ffn_kernel(x_ref, gpre_ref, gpost_ref, router_ref, wg_ref, wu_ref, wd_ref, o_ref,
                h_ref, acc_ref, gates_ref, *, moe):
    e = pl.program_id(1)
    f = pl.program_id(2)

    @pl.when((e == 0) & (f == 0))
    def _():
        h = _rms(x_ref[...], gpre_ref[...])
        h_ref[...] = h.astype(h_ref.dtype)
        acc_ref[...] = jnp.zeros_like(acc_ref)
        if moe:
            gates_ref[...] = _top2_gates(h, router_ref[...])
        else:
            gates_ref[...] = jnp.zeros_like(gates_ref)

    hb = h_ref[...]
    act = jax.nn.silu(_mm(hb, wg_ref[0])) * _mm(hb, wu_ref[0])
    if moe:
        gates = gates_ref[...]
        lane = lax.broadcasted_iota(jnp.int32, gates.shape, 1)
        act = act * jnp.sum(jnp.where(lane == e, gates, 0.0), axis=-1, keepdims=True)
    acc_ref[...] += _mm(act, wd_ref[0])

    @pl.when((e == pl.num_programs(1) - 1) & (f == pl.num_programs(2) - 1))
    def _():
        o_ref[...] = x_ref[...] + _rms(acc_ref[...], gpost_ref[...])


def ffn(x, gpre, gpost, router, wg, wu, wd, tm, tf, moe):
    M, K = x.shape
    E, _, F = wg.shape
    return pl.pallas_call(
        functools.partial(_ffn_kernel, moe=moe),
        grid=(M // tm, E, F // tf),
        in_specs=[pl.BlockSpec((tm, K), lambda i, e, f: (i, 0)),
                  pl.BlockSpec((1, K), lambda i, e, f: (0, 0)),
                  pl.BlockSpec((1, K), lambda i, e, f: (0, 0)),
                  pl.BlockSpec((K, LANES), lambda i, e, f: (0, 0)),
                  pl.BlockSpec((1, K, tf), lambda i, e, f: (e, 0, f)),
                  pl.BlockSpec((1, K, tf), lambda i, e, f: (e, 0, f)),
                  pl.BlockSpec((1, tf, K), lambda i, e, f: (e, f, 0))],
        out_specs=pl.BlockSpec((tm, K), lambda i, e, f: (i, 0)),
        out_shape=jax.ShapeDtypeStruct((M, K), F32),
        scratch_shapes=[pltpu.VMEM((tm, K), wg.dtype), pltpu.VMEM((tm, K), F32), pltpu.VMEM((tm, LANES), F32)],
        compiler_params=_cparams(("parallel", "arbitrary", "arbitrary")),
        name="moe_ffn" if moe else "dense_ffn",
    )(x, gpre.reshape(1, K), gpost.reshape(1, K), router, wg, wu, wd)


def _xattn_kernel(x_ref, k_ref, v_ref, wq_ref, wo_ref, gpre_ref, gpost_ref, o_ref):
    x = x_ref[0]
    h = _rms(x, gpre_ref[...]).astype(BF16)
    q = _dot(h, wq_ref[...])
    k = k_ref[0].astype(BF16)
    v = v_ref[0].astype(BF16)
    outs = []
    for hd in range(X_HEADS):
        sl = slice(hd * X_HEAD_DIM, (hd + 1) * X_HEAD_DIM)
        s = _dot_nt(q[:, sl].astype(BF16), k[:, sl]) * (X_HEAD_DIM ** -0.5)
        e = jnp.exp(s - jnp.max(s, axis=-1, keepdims=True))
        p = e / jnp.sum(e, axis=-1, keepdims=True)
        outs.append(_dot(p.astype(BF16), v[:, sl]))
    o = jnp.concatenate(outs, axis=-1).astype(BF16)
    o_ref[0] = x + _rms(_dot(o, wo_ref[...]), gpost_ref[...])


def xattn_prompt(x, k, v, wq, wo, gpre, gpost, tm):
    B, T, K = x.shape
    return pl.pallas_call(
        _xattn_kernel,
        grid=(B, T // tm),
        in_specs=[pl.BlockSpec((1, tm, K), lambda b, t: (b, t, 0)),
                  pl.BlockSpec((1, MEM_LEN, K), lambda b, t: (b, 0, 0)),
                  pl.BlockSpec((1, MEM_LEN, K), lambda b, t: (b, 0, 0)),
                  pl.BlockSpec((K, K), lambda b, t: (0, 0)),
                  pl.BlockSpec((K, K), lambda b, t: (0, 0)),
                  pl.BlockSpec((1, K), lambda b, t: (0, 0)),
                  pl.BlockSpec((1, K), lambda b, t: (0, 0))],
        out_specs=pl.BlockSpec((1, tm, K), lambda b, t: (b, t, 0)),
        out_shape=jax.ShapeDtypeStruct((B, T, K), F32),
        compiler_params=_cparams(("parallel", "parallel")),
        name="xattn_prompt",
    )(x, k, v, wq, wo, gpre.reshape(1, K), gpost.reshape(1, K))


def _xattn_sample_kernel(q_ref, k_ref, v_ref, o_ref):
    q = q_ref[0]
    k = k_ref[0]
    v = v_ref[0]
    kq = k * q
    outs = []
    for hd in range(X_HEADS):
        sl = slice(hd * X_HEAD_DIM, (hd + 1) * X_HEAD_DIM)
        s = jnp.sum(kq[:, sl], axis=-1, keepdims=True) * (X_HEAD_DIM ** -0.5)
        e = jnp.exp(s - jnp.max(s, axis=0, keepdims=True))
        p = e / jnp.sum(e, axis=0, keepdims=True)
        outs.append(jnp.sum(p * v[:, sl], axis=0, keepdims=True))
    o_ref[0] = jnp.concatenate(outs, axis=-1)


def xattn_sample_core(q, k, v):
    B = q.shape[0]
    return pl.pallas_call(
        _xattn_sample_kernel,
        grid=(B,),
        in_specs=[pl.BlockSpec((1, 1, D), lambda b: (b, 0, 0)),
                  pl.BlockSpec((1, MEM_LEN, D), lambda b: (b, 0, 0)),
                  pl.BlockSpec((1, MEM_LEN, D), lambda b: (b, 0, 0))],
        out_specs=pl.BlockSpec((1, 1, D), lambda b: (b, 0, 0)),
        out_shape=jax.ShapeDtypeStruct((B, 1, D), F32),
        compiler_params=_cparams(("parallel",)),
        name="xattn_sample",
    )(q, k, v)


def _mixer_even_kernel(u_ref, v_ref, z_ref, xbc_ref, dt_ref, ws_ref, bst_ref, lng_ref, lnb_ref, cw_ref, cb_ref,
                       dtb_ref, aexp_ref, dexp_ref, bnorm_ref, tril_ref, expand_ref,
                       yab_ref, ssm_ref, xp_ref, st_ref):
    c = pl.program_id(1)
    L = A_CHUNK

    @pl.when(c == 0)
    def _():
        xp_ref[0:SUBLANES, :] = jnp.zeros((SUBLANES, B_CONV_DIM), F32)
        st_ref[...] = jnp.zeros_like(st_ref)

    row = lax.broadcasted_iota(jnp.int32, (L, L), 0)
    col = lax.broadcasted_iota(jnp.int32, (L, L), 1)
    causal = col <= row

    gu = jax.nn.gelu(u_ref[0])
    vb = _layernorm(jax.nn.gelu(v_ref[0]), lng_ref[...], lnb_ref[...]).astype(BF16)
    for hh in range(A_HEADS):
        sl = slice(hh * LANES, (hh + 1) * LANES)
        w = jnp.where(causal, ws_ref[hh], 0.0).astype(BF16)
        mix = _dot(w, vb[:, sl]) + bst_ref[:, hh:hh + 1]
        yab_ref[0, :, sl] = (gu[:, sl] * mix).astype(BF16)

    x = xbc_ref[0]
    xp_ref[SUBLANES:SUBLANES + L, :] = x
    conv = cb_ref[...] + cw_ref[B_CONV - 1:B_CONV, :] * x
    for k in range(B_CONV - 1):
        conv = conv + cw_ref[k:k + 1, :] * xp_ref[pl.ds(SUBLANES - (B_CONV - 1) + k, L), :]
    xp_ref[0:SUBLANES, :] = x[L - SUBLANES:L, :]
    xa = jax.nn.silu(conv)
    xs = xa[:, :D]
    bm = xa[:, D:D + B_GROUPS * B_STATE]
    cm = xa[:, D + B_GROUPS * B_STATE:]

    tril = tril_ref[...]
    dtf = _softplus(dt_ref[0] + dtb_ref[...])
    dt_x = _sel_right(dtf, expand_ref[...], _split2)
    cs_x = _sel_left(tril, dt_x * aexp_ref[...], _split3)
    ecs_x = jnp.exp(cs_x)
    last_x = cs_x[L - 1:L, :]
    xdt = xs * dt_x
    xdt_b = xdt.astype(BF16)
    xdec_b = (xdt * jnp.exp(last_x - cs_x)).astype(BF16)
    lane = lax.broadcasted_iota(jnp.int32, (L, LANES), 1)
    y_parts = []
    for g in range(B_GROUPS):
        gs = slice(g * B_GROUP_W, (g + 1) * B_GROUP_W)
        bg = bm[:, g * B_STATE:(g + 1) * B_STATE]
        cg = cm[:, g * B_STATE:(g + 1) * B_STATE].astype(BF16)
        gmat = _dot_nt(cg, bg.astype(BF16))
        st = st_ref[g]
        y_off = ecs_x[:, gs] * _dot(cg, st.astype(BF16))
        st_new = st * jnp.exp(last_x[:, gs]) + _dot(bg.T.astype(BF16), xdec_b[:, gs])
        st_ref[g] = st_new
        for pair in range(B_GROUP_W // LANES):
            base = g * B_GROUP_W + pair * LANES
            cs_t = cs_x[:, base:base + LANES].T
            ms = []
            for half in range(2):
                ch = half * B_HEAD_DIM
                diff = cs_x[:, base + ch:base + ch + 1] - cs_t[ch:ch + 1, :]
                ms.append((gmat * jnp.exp(jnp.where(causal, diff, -jnp.inf))).astype(BF16))
            xp2 = xdt_b[:, base:base + LANES]
            rhs = jnp.concatenate([jnp.where(lane < B_HEAD_DIM, xp2, jnp.zeros_like(xp2)),
                                   jnp.where(lane >= B_HEAD_DIM, xp2, jnp.zeros_like(xp2))], axis=0)
            y_parts.append(_dot(jnp.concatenate(ms, axis=1), rhs) + y_off[:, pair * LANES:(pair + 1) * LANES])
    y = jnp.concatenate(y_parts, axis=-1) + dexp_ref[...] * xs
    yb = _group_rms(y * jax.nn.silu(z_ref[0]), bnorm_ref[...], B_GROUP_W)
    yab_ref[0, :, D:] = yb.astype(BF16)

    @pl.when(c == pl.num_programs(1) - 1)
    def _():
        for g in range(B_GROUPS):
            ssm_ref[0, g] = st_ref[g].T


def _even_consts(P):
    head_of_ch = jnp.arange(D) // B_HEAD_DIM
    expand = (jnp.arange(LANES)[:, None] == head_of_ch[None, :]).astype(BF16)
    tril = (jnp.arange(A_CHUNK)[None, :] <= jnp.arange(A_CHUNK)[:, None]).astype(BF16)
    aexp = jnp.repeat(-jnp.exp(P["b_a_log"][0].astype(F32)), B_HEAD_DIM).reshape(1, D)
    dexp = jnp.repeat(P["b_d"][0].astype(F32), B_HEAD_DIM).reshape(1, D)
    dtb = jnp.pad(P["b_dt_bias"][0].astype(F32), (0, LANES - B_HEADS)).reshape(1, LANES)
    return expand, tril, aexp, dexp, dtb


def mixer_even_prompt(proj, P):
    B, T, _ = proj.shape
    L = A_CHUNK
    expand, tril, aexp, dexp, dtb = _even_consts(P)
    row = lambda a: a.reshape(1, -1)
    full = lambda shape: pl.BlockSpec(shape, lambda b, c: (0,) * len(shape))
    return pl.pallas_call(
        _mixer_even_kernel,
        grid=(B, T // L),
        in_specs=[pl.BlockSpec((1, L, D), lambda b, c: (b, c, 0)),
                  pl.BlockSpec((1, L, D), lambda b, c: (b, c, 1)),
                  pl.BlockSpec((1, L, D), lambda b, c: (b, c, 2)),
                  pl.BlockSpec((1, L, B_CONV_DIM), lambda b, c: (b, c, 2)),
                  pl.BlockSpec((1, L, LANES), lambda b, c: (b, c, DT_COL_BLOCK)),
                  full((A_HEADS, L, L)), full((L, A_HEADS)), full((1, D)), full((1, D)),
                  full((B_CONV, B_CONV_DIM)), full((1, B_CONV_DIM)), full((1, LANES)), full((1, D)), full((1, D)),
                  full((1, D)), full((L, L)), full((LANES, D))],
        out_specs=[pl.BlockSpec((1, L, 2 * D), lambda b, c: (b, c, 0)),
                   pl.BlockSpec((1, B_GROUPS, B_GROUP_W, B_STATE), lambda b, c: (b, 0, 0, 0))],
        out_shape=[jax.ShapeDtypeStruct((B, T, 2 * D), BF16),
                   jax.ShapeDtypeStruct((B, B_GROUPS, B_GROUP_W, B_STATE), F32)],
        scratch_shapes=[pltpu.VMEM((SUBLANES + L, B_CONV_DIM), F32),
                        pltpu.VMEM((B_GROUPS, B_STATE, B_GROUP_W), F32)],
        compiler_params=_cparams(("parallel", "arbitrary")),
        name="mixer_even_prompt",
    )(proj, proj, proj, proj, proj, P["a_ws"][0], P["a_bs"][0].T, row(P["a_ln_g"][0]), row(P["a_ln_b"][0]),
      P["b_conv_w"][0], row(P["b_conv_b"][0]), dtb, aexp, dexp, row(P["b_norm"][0]), tril, expand)


def _mixer_even_step_kernel(proj_ref, conv_ref, ssm_ref, ws0_ref, bs0_ref, lng_ref, lnb_ref, cw_ref, cb_ref,
                            dtb_ref, aexp_ref, dexp_ref, bnorm_ref, expand_ref,
                            yab_ref, av_ref, convo_ref, ssmo_ref):
    nb = proj_ref.shape[0]
    u = proj_ref[:, 0:D]
    v = proj_ref[:, D:2 * D]
    z = proj_ref[:, 2 * D:3 * D]
    x = proj_ref[:, 3 * D:3 * D + B_CONV_DIM]
    dt = proj_ref[:, DT_COL_BLOCK * LANES:(DT_COL_BLOCK + 1) * LANES]

    vln = _layernorm(jax.nn.gelu(v), lng_ref[...], lnb_ref[...])
    av_ref[...] = vln
    yab_ref[:, 0:D] = jax.nn.gelu(u) * (ws0_ref[...] * vln + bs0_ref[...])

    conv = cb_ref[...] + cw_ref[B_CONV - 1:B_CONV, :] * x
    for k in range(B_CONV - 1):
        conv = conv + cw_ref[k:k + 1, :] * conv_ref[k]
        if k > 0:
            convo_ref[k - 1] = conv_ref[k]
    convo_ref[B_CONV - 2] = x
    xa = jax.nn.silu(conv)
    xs = xa[:, :D]
    bm = xa[:, D:D + B_GROUPS * B_STATE]
    cm = xa[:, D + B_GROUPS * B_STATE:]
    dtf = _softplus(dt + dtb_ref[...])
    dt_x = _sel_right(dtf, expand_ref[...], _split3)
    dec_x = jnp.exp(dt_x * aexp_ref[...])
    xdt = xs * dt_x
    y_rows = []
    for g in range(B_GROUPS):
        gs = slice(g * B_GROUP_W, (g + 1) * B_GROUP_W)
        dec_t = _rows_to_cols(dec_x[:, gs])
        xdt_t = _rows_to_cols(xdt[:, gs])
        ys = []
        for j in range(nb):
            s_new = ssm_ref[j, g] * dec_t[:, j:j + 1] + xdt_t[:, j:j + 1] * bm[j:j + 1, g * B_STATE:(g + 1) * B_STATE]
            ssmo_ref[j, g] = s_new
            cj = jnp.broadcast_to(cm[j:j + 1, g * B_STATE:(g + 1) * B_STATE], (SUBLANES, B_STATE))
            ys.append(_dot_nt_f32(cj, s_new)[0:1, :])
        y_rows.append(jnp.concatenate(ys, axis=0))
    y = jnp.concatenate(y_rows, axis=-1) + dexp_ref[...] * xs
    yb = _group_rms(y * jax.nn.silu(z), bnorm_ref[...], B_GROUP_W)
    yab_ref[:, D:] = yb


def mixer_even_step(proj, conv0, ssm0, P, nb):
    B = proj.shape[0]
    expand, _, aexp, dexp, dtb = _even_consts(P)
    row = lambda a: a.reshape(1, -1)
    rep = lambda a: jnp.repeat(a.astype(F32), LANES).reshape(1, D)
    full = lambda shape: pl.BlockSpec(shape, lambda i: (0,) * len(shape))
    return pl.pallas_call(
        _mixer_even_step_kernel,
        grid=(B // nb,),
        in_specs=[pl.BlockSpec((nb, proj.shape[1]), lambda i: (i, 0)),
                  pl.BlockSpec((B_CONV - 1, nb, B_CONV_DIM), lambda i: (0, i, 0)),
                  pl.BlockSpec((nb, B_GROUPS, B_GROUP_W, B_STATE), lambda i: (i, 0, 0, 0)),
                  full((1, D)), full((1, D)), full((1, D)), full((1, D)),
                  full((B_CONV, B_CONV_DIM)), full((1, B_CONV_DIM)), full((1, LANES)), full((1, D)), full((1, D)),
                  full((1, D)), full((LANES, D))],
        out_specs=[pl.BlockSpec((nb, 2 * D), lambda i: (i, 0)),
                   pl.BlockSpec((nb, D), lambda i: (i, 0)),
                   pl.BlockSpec((B_CONV - 1, nb, B_CONV_DIM), lambda i: (0, i, 0)),
                   pl.BlockSpec((nb, B_GROUPS, B_GROUP_W, B_STATE), lambda i: (i, 0, 0, 0))],
        out_shape=[jax.ShapeDtypeStruct((B, 2 * D), F32),
                   jax.ShapeDtypeStruct((B, D), F32),
                   jax.ShapeDtypeStruct((B_CONV - 1, B, B_CONV_DIM), F32),
                   jax.ShapeDtypeStruct((B, B_GROUPS, B_GROUP_W, B_STATE), F32)],
        compiler_params=_cparams(("parallel",)),
        name="mixer_even_step",
    )(proj, conv0, ssm0, rep(P["a_ws"][0][:, 0, 0]), rep(P["a_bs"][0][:, 0]), row(P["a_ln_g"][0]), row(P["a_ln_b"][0]),
      P["b_conv_w"][0], row(P["b_conv_b"][0]), dtb, aexp, dexp, row(P["b_norm"][0]), expand)


def _gla_gates(q_raw, f_raw, lb):
    fg = lb + (1.0 - lb) * jax.nn.sigmoid(f_raw)
    return jax.nn.silu(q_raw), fg, 1.0 - fg


def _hgrn_kernel(q_ref, f_ref, i_ref, g_ref, lb_ref, cnorm_ref, tril_ref, o_ref, s_ref, st_ref):
    c = pl.program_id(1)
    L = C_CHUNK

    @pl.when(c == 0)
    def _():
        st_ref[...] = jnp.zeros_like(st_ref)

    row = lax.broadcasted_iota(jnp.int32, (L, L), 0)
    col = lax.broadcasted_iota(jnp.int32, (L, L), 1)
    causal = col <= row
    q, fg, k = _gla_gates(q_ref[0], f_ref[0], lb_ref[...])
    v = i_ref[0]
    bc = _sel_left(tril_ref[...], jnp.log(fg), _split3)
    btot = bc[L - 1:L, :]
    q_in = (q * jnp.exp(bc)).astype(BF16)
    k_in = (k * jnp.exp(-bc)).astype(BF16)
    k_dec = jnp.concatenate([k * jnp.exp(btot - bc), jnp.zeros((LANES - L, D), F32)], axis=0).astype(BF16)
    vb = v.astype(BF16)
    outs = []
    for hh in range(C_HEADS):
        sl = slice(hh * C_KDIM, (hh + 1) * C_KDIM)
        att = jnp.where(causal, _dot_nt(q_in[:, sl], k_in[:, sl]), 0.0)
        st = st_ref[hh]
        outs.append(_dot(att.astype(BF16), vb[:, sl]) + _dot_nt(q_in[:, sl], st.astype(BF16)))
        st_ref[hh] = st * jnp.exp(btot[:, sl]) + _dot(_rows_to_cols(v[:, sl]).astype(BF16), k_dec[:, sl])
    o = _group_rms(jnp.concatenate(outs, axis=-1), cnorm_ref[...], C_KDIM)
    o_ref[0] = (o * jax.nn.silu(g_ref[0])).astype(BF16)

    @pl.when(c == pl.num_programs(1) - 1)
    def _():
        for hh in range(C_HEADS):
            s_ref[0, hh] = st_ref[hh].T


def hgrn_prompt(proj, lb, cnorm):
    B, T, _ = proj.shape
    L = C_CHUNK
    tril = (jnp.arange(L)[None, :] <= jnp.arange(L)[:, None]).astype(BF16)
    full = lambda shape: pl.BlockSpec(shape, lambda b, c: (0,) * len(shape))
    return pl.pallas_call(
        _hgrn_kernel,
        grid=(B, T // L),
        in_specs=[pl.BlockSpec((1, L, D), lambda b, c: (b, c, 0)),
                  pl.BlockSpec((1, L, D), lambda b, c: (b, c, 1)),
                  pl.BlockSpec((1, L, D), lambda b, c: (b, c, 2)),
                  pl.BlockSpec((1, L, D), lambda b, c: (b, c, 3)),
                  full((1, D)), full((1, D)), full((L, L))],
        out_specs=[pl.BlockSpec((1, L, D), lambda b, c: (b, c, 0)),
                   pl.BlockSpec((1, C_HEADS, C_KDIM, C_KDIM), lambda b, c: (b, 0, 0, 0))],
        out_shape=[jax.ShapeDtypeStruct((B, T, D), BF16),
                   jax.ShapeDtypeStruct((B, C_HEADS, C_KDIM, C_KDIM), F32)],
        scratch_shapes=[pltpu.VMEM((C_HEADS, C_KDIM, C_KDIM), F32)],
        compiler_params=_cparams(("parallel", "arbitrary")),
        name="hgrn_prompt",
    )(proj, proj, proj, proj, lb.reshape(1, D), cnorm.reshape(1, D), tril)


def _hgrn_step_kernel(proj_ref, s_ref, lb_ref, cnorm_ref, o_ref, so_ref):
    nb = proj_ref.shape[0]
    q, fg, k = _gla_gates(proj_ref[:, 0:D], proj_ref[:, D:2 * D], lb_ref[...])
    v = proj_ref[:, 2 * D:3 * D]
    g = proj_ref[:, 3 * D:4 * D]
    outs = []
    for hh in range(C_HEADS):
        sl = slice(hh * C_KDIM, (hh + 1) * C_KDIM)
        fg_t = _rows_to_cols(fg[:, sl])
        k_t = _rows_to_cols(k[:, sl])
        rows = []
        for j in range(nb):
            s_new = s_ref[j, hh] * fg_t[:, j:j + 1] + k_t[:, j:j + 1] * v[j:j + 1, sl]
            so_ref[j, hh] = s_new
            qj = jnp.broadcast_to(q[j:j + 1, sl], (SUBLANES, C_KDIM))
            rows.append(_dot_f32(qj, s_new)[0:1, :])
        outs.append(jnp.concatenate(rows, axis=0))
    o = _group_rms(jnp.concatenate(outs, axis=-1), cnorm_ref[...], C_KDIM)
    o_ref[...] = o * jax.nn.silu(g)


def hgrn_step(proj, s0, lb, cnorm, nb):
    B = proj.shape[0]
    full = lambda shape: pl.BlockSpec(shape, lambda i: (0,) * len(shape))
    return pl.pallas_call(
        _hgrn_step_kernel,
        grid=(B // nb,),
        in_specs=[pl.BlockSpec((nb, 4 * D), lambda i: (i, 0)),
                  pl.BlockSpec((nb, C_HEADS, C_KDIM, C_KDIM), lambda i: (i, 0, 0, 0)),
                  full((1, D)), full((1, D))],
        out_specs=[pl.BlockSpec((nb, D), lambda i: (i, 0)),
                   pl.BlockSpec((nb, C_HEADS, C_KDIM, C_KDIM), lambda i: (i, 0, 0, 0))],
        out_shape=[jax.ShapeDtypeStruct((B, D), F32),
                   jax.ShapeDtypeStruct((B, C_HEADS, C_KDIM, C_KDIM), F32)],
        compiler_params=_cparams(("parallel",)),
        name="hgrn_step",
    )(proj, s0, lb.reshape(1, D), cnorm.reshape(1, D))


TM_PROJ = 256
TM_OUT = 512
TM_ATTN = 512
TM_FFN = 1024
TF_DENSE = 256
TF_MOE = 512
STEP_NB = 8
TN_STEP = 512
TK_STEP = 512
IN0_STEP_PAD = 5120


def _prep_weights(P):
    W = {}
    W["ev_w_in"] = jnp.pad(P["ev_w_in"][0], ((0, 0), (0, IN0_PAD - IN0))).astype(BF16)
    W["ev_w_out"] = P["ev_w_out"][0].astype(BF16)
    W["od_w_in"] = P["od_w_in"][0].astype(BF16)
    W["od_w_out"] = P["od_w_out"][0].astype(BF16)
    W["xa_wq"] = P["xa_wq"].astype(BF16)
    W["xa_wo"] = P["xa_wo"].astype(BF16)
    W["xa_wkv"] = jnp.concatenate([P["xa_wk"], P["xa_wv"]], axis=-1).astype(BF16)
    W["ffn"] = tuple(P[n].astype(BF16) for n in ("ffn_w_gate", "ffn_w_up", "ffn_w_down"))
    W["moe"] = tuple(P[n][0].astype(BF16) for n in ("moe_w_gate", "moe_w_up", "moe_w_down"))
    W["router"] = jnp.pad(P["moe_router"][0].astype(F32), ((0, 0), (0, LANES - N_EXPERTS)))
    lbp = jax.nn.softmax(P["hgrn_lb_logits"].astype(F32), axis=0)
    W["lower_bounds"] = jnp.cumsum(lbp, axis=0) - lbp[0]
    return W


def _channel_mix(x2, l, P, W, tm):
    if l == 0:
        wg, wu, wd = W["ffn"]
        return ffn(x2, P["norm_ffn_pre"][l], P["norm_ffn_post"][l], W["router"], wg, wu, wd, tm, TF_DENSE, False)
    wg, wu, wd = W["moe"]
    return ffn(x2, P["norm_ffn_pre"][l], P["norm_ffn_post"][l], W["router"], wg, wu, wd, tm, TF_MOE, True)


def _trunk_prompt(x, mem_k, mem_v, P, W):
    B, T, _ = x.shape
    M = B * T
    x2 = x.reshape(M, D)
    proj = norm_matmul(x2, P["norm_mix_pre"][0], W["ev_w_in"], TM_PROJ).reshape(B, T, IN0_PAD)
    yab, ssm = mixer_even_prompt(proj, P)
    conv = proj[:, T - (B_CONV - 1):, 3 * D:3 * D + B_CONV_DIM]
    x2 = matmul_norm_res(yab.reshape(M, 2 * D), W["ev_w_out"], P["norm_mix_post"][0], x2, TM_OUT)
    x2 = xattn_prompt(x2.reshape(B, T, D), mem_k[0], mem_v[0], W["xa_wq"][0], W["xa_wo"][0],
                      P["norm_x_pre"][0], P["norm_x_post"][0], TM_ATTN).reshape(M, D)
    x2 = _channel_mix(x2, 0, P, W, TM_FFN)
    proj = norm_matmul(x2, P["norm_mix_pre"][1], W["od_w_in"], TM_PROJ).reshape(B, T, 4 * D)
    o, hgrn = hgrn_prompt(proj, W["lower_bounds"][1], P["c_norm"][0])
    x2 = matmul_norm_res(o.reshape(M, D), W["od_w_out"], P["norm_mix_post"][1], x2, TM_OUT)
    x2 = xattn_prompt(x2.reshape(B, T, D), mem_k[1], mem_v[1], W["xa_wq"][1], W["xa_wo"][1],
                      P["norm_x_pre"][1], P["norm_x_post"][1], TM_ATTN).reshape(M, D)
    x2 = _channel_mix(x2, 1, P, W, TM_FFN)
    return x2.reshape(B, T, D), conv, ssm, hgrn


def _xattn_step(x2, l, mem_k, mem_v, P):
    B = x2.shape[0]
    q = norm_matmul(x2, P["norm_x_pre"][l], P["xa_wq"][l], B, TN_STEP)
    o = xattn_sample_core(q.reshape(B, 1, D), mem_k, mem_v).reshape(B, D)
    return matmul_norm_res(o, P["xa_wo"][l], P["norm_x_post"][l], x2, B, TK_STEP)


def _trunk_step(x, mem_k, mem_v, conv0, ssm0, hgrn0, P, W):
    B = x.shape[0]
    x2 = x.reshape(B, D)
    w_in = jnp.pad(P["ev_w_in"][0], ((0, 0), (0, IN0_STEP_PAD - IN0)))
    proj = norm_matmul(x2, P["norm_mix_pre"][0], w_in, B, TN_STEP)
    yab, av, conv, ssm = mixer_even_step(proj, jnp.swapaxes(conv0, 0, 1), ssm0, P, STEP_NB)
    x2 = matmul_norm_res(yab, P["ev_w_out"][0], P["norm_mix_post"][0], x2, B, TK_STEP)
    x2 = _xattn_step(x2, 0, mem_k[0], mem_v[0], P)
    x2 = ffn(x2, P["norm_ffn_pre"][0], P["norm_ffn_post"][0], W["router"],
             P["ffn_w_gate"], P["ffn_w_up"], P["ffn_w_down"], B, TF_DENSE, False)
    proj = norm_matmul(x2, P["norm_mix_pre"][1], P["od_w_in"][0], B, TN_STEP)
    o, hgrn = hgrn_step(proj, hgrn0, W["lower_bounds"][1], P["c_norm"][0], STEP_NB)
    x2 = matmul_norm_res(o, P["od_w_out"][0], P["norm_mix_post"][1], x2, B, TK_STEP)
    x2 = _xattn_step(x2, 1, mem_k[1], mem_v[1], P)
    x2 = _channel_mix(x2, 1, P, W, B)
    return x2.reshape(B, 1, D), jnp.swapaxes(conv, 0, 1), ssm, hgrn, av


def kernel(x_prompt, x_sample, mem_prompt, cache_mem_k, cache_mem_v, state_conv, state_ssm, state_hgrn,
           norm_mix_pre, norm_mix_post, norm_x_pre, norm_x_post, norm_ffn_pre, norm_ffn_post, norm_mem,
           xa_wq, xa_wk, xa_wv, xa_wo,
           ev_w_in, a_ws, a_bs, a_ln_g, a_ln_b, b_conv_w, b_conv_b, b_dt_bias, b_a_log, b_d, b_norm, ev_w_out,
           ffn_w_gate, ffn_w_up, ffn_w_down,
           od_w_in, hgrn_lb_logits, c_norm, od_w_out,
           moe_router, moe_w_gate, moe_w_up, moe_w_down):
    P = dict(norm_mix_pre=norm_mix_pre, norm_mix_post=norm_mix_post, norm_x_pre=norm_x_pre, norm_x_post=norm_x_post,
             norm_ffn_pre=norm_ffn_pre, norm_ffn_post=norm_ffn_post, xa_wq=xa_wq, xa_wk=xa_wk, xa_wv=xa_wv,
             xa_wo=xa_wo, ev_w_in=ev_w_in, a_ws=a_ws, a_bs=a_bs, a_ln_g=a_ln_g, a_ln_b=a_ln_b, b_conv_w=b_conv_w,
             b_conv_b=b_conv_b, b_dt_bias=b_dt_bias, b_a_log=b_a_log, b_d=b_d, b_norm=b_norm, ev_w_out=ev_w_out,
             ffn_w_gate=ffn_w_gate, ffn_w_up=ffn_w_up, ffn_w_down=ffn_w_down, od_w_in=od_w_in,
             hgrn_lb_logits=hgrn_lb_logits, c_norm=c_norm, od_w_out=od_w_out, moe_router=moe_router,
             moe_w_gate=moe_w_gate, moe_w_up=moe_w_up, moe_w_down=moe_w_down)
    W = _prep_weights(P)
    depth = norm_mem.shape[0]
    bp, T, _ = x_prompt.shape
    bs = x_sample.shape[0]

    mem2 = mem_prompt.reshape(bp * MEM_LEN, D)
    kv = [norm_matmul(mem2, norm_mem[l], W["xa_wkv"][l], TM_PROJ) for l in range(depth)]
    mem_k_p = jnp.stack([t[:, :D] for t in kv]).reshape(depth, bp, MEM_LEN, D)
    mem_v_p = jnp.stack([t[:, D:] for t in kv]).reshape(depth, bp, MEM_LEN, D)
    y_p, conv_p, ssm_p, hgrn_p = _trunk_prompt(x_prompt, mem_k_p, mem_v_p, P, W)

    y_s, conv_s, ssm_s, hgrn_s, av_s = _trunk_step(
        x_sample, cache_mem_k.reshape(depth, bs, MEM_LEN, D), cache_mem_v.reshape(depth, bs, MEM_LEN, D),
        state_conv[0], state_ssm[0].reshape(bs, B_GROUPS, B_GROUP_W, B_STATE), state_hgrn[0], P, W)

    kv_shape = (depth, bp, MEM_LEN, X_HEADS, X_HEAD_DIM)
    ssm_shape = (B_GROUPS, B_GROUP_W // B_HEAD_DIM, B_HEAD_DIM, B_STATE)
    return (y_p, y_s, mem_k_p.reshape(kv_shape), mem_v_p.reshape(kv_shape),
            conv_p[None], ssm_p.reshape((1, bp) + ssm_shape), hgrn_p[None],
            conv_s[None], ssm_s.reshape((1, bs) + ssm_shape), hgrn_s[None], av_s.reshape(1, bs, 1, D))
```

```python
import functools

import jax
import jax.numpy as jnp
from jax import lax
from jax.experimental import pallas as pl
from jax.experimental.pallas import tpu as pltpu

F32 = jnp.float32
BF16 = jnp.bfloat16
EPS = 1e-6

D = 1024
LANES = 128
SUBLANES = 8
A_HEADS = 8
A_CHUNK = 128
B_HEADS = 16
B_HEAD_DIM = 64
B_GROUPS = 2
B_GROUP_W = 512
B_STATE = 128
B_CONV = 4
B_CONV_DIM = 1536
IN0 = 4624
IN0_PAD = 4736
DT_COL_BLOCK = 36
C_HEADS = 8
C_KDIM = 128
C_CHUNK = 64
X_HEADS = 4
X_HEAD_DIM = 256
MEM_LEN = 256
N_EXPERTS = 8

VMEM_LIMIT = 56 * 1024 * 1024


def _cparams(sem):
    return pltpu.CompilerParams(dimension_semantics=sem, vmem_limit_bytes=VMEM_LIMIT)


def _dot(a, b):
    return jnp.dot(a, b, preferred_element_type=F32)


def _dot_nt(a, b):
    return lax.dot_general(a, b, (((1,), (1,)), ((), ())), preferred_element_type=F32)


def _dot_f32(a, b):
    return jnp.dot(a, b, precision=lax.Precision.HIGHEST, preferred_element_type=F32)


def _dot_nt_f32(a, b):
    return lax.dot_general(a, b, (((1,), (1,)), ((), ())), precision=lax.Precision.HIGHEST,
                           preferred_element_type=F32)


def _mm(a, w):
    if w.dtype == F32:
        return _dot_f32(a.astype(F32), w)
    return _dot(a.astype(BF16), w)


def _rms(x, g):
    return x * lax.rsqrt(jnp.mean(x * x, axis=-1, keepdims=True) + EPS) * g


def _split2(x):
    hi = x.astype(BF16)
    lo = (x - hi.astype(F32)).astype(BF16)
    return hi, lo


def _split3(x):
    hi = x.astype(BF16)
    r = x - hi.astype(F32)
    mid = r.astype(BF16)
    lo = (r - mid.astype(F32)).astype(BF16)
    return hi, mid, lo


def _sel_left(m, x, parts):
    out = None
    for p in parts(x):
        t = _dot(m, p)
        out = t if out is None else out + t
    return out


def _sel_right(x, m, parts):
    out = None
    for p in parts(x):
        t = _dot(p, m)
        out = t if out is None else out + t
    return out


def _rows_to_cols(x):
    n, w = x.shape
    if n < LANES:
        x = jnp.concatenate([x, jnp.zeros((LANES - n, w), x.dtype)], axis=0)
    return x.T


def _softplus(x):
    return jnp.maximum(x, 0.0) + jnp.log1p(jnp.exp(-jnp.abs(x)))


def _layernorm(x, g, b):
    xc = x - jnp.mean(x, axis=-1, keepdims=True)
    return xc * lax.rsqrt(jnp.mean(xc * xc, axis=-1, keepdims=True) + EPS) * g + b


def _group_rms(x, g, width):
    parts = []
    for s in range(0, x.shape[-1], width):
        t = x[:, s:s + width]
        parts.append(t * lax.rsqrt(jnp.mean(t * t, axis=-1, keepdims=True) + EPS))
    return jnp.concatenate(parts, axis=-1) * g


def _norm_matmul_kernel(x_ref, g_ref, w_ref, o_ref):
    o_ref[...] = _mm(_rms(x_ref[...], g_ref[...]), w_ref[...])


def norm_matmul(x, g, w, tm, tn=None):
    M, K = x.shape
    N = w.shape[1]
    tn = N if tn is None else tn
    return pl.pallas_call(
        _norm_matmul_kernel,
        grid=(M // tm, N // tn),
        in_specs=[pl.BlockSpec((tm, K), lambda i, j: (i, 0)),
                  pl.BlockSpec((1, K), lambda i, j: (0, 0)),
                  pl.BlockSpec((K, tn), lambda i, j: (0, j))],
        out_specs=pl.BlockSpec((tm, tn), lambda i, j: (i, j)),
        out_shape=jax.ShapeDtypeStruct((M, N), F32),
        compiler_params=_cparams(("parallel", "parallel")),
        name="norm_matmul",
    )(x, g.reshape(1, K), w)


def _matmul_norm_res_kernel(a_ref, w_ref, g_ref, r_ref, o_ref, acc_ref):
    k = pl.program_id(1)

    @pl.when(k == 0)
    def _():
        acc_ref[...] = jnp.zeros_like(acc_ref)

    acc_ref[...] += _mm(a_ref[...], w_ref[...])

    @pl.when(k == pl.num_programs(1) - 1)
    def _():
        o_ref[...] = r_ref[...] + _rms(acc_ref[...], g_ref[...])


def matmul_norm_res(a, w, g, res, tm, tk=None):
    M, K = a.shape
    N = w.shape[1]
    tk = K if tk is None else tk
    return pl.pallas_call(
        _matmul_norm_res_kernel,
        grid=(M // tm, K // tk),
        in_specs=[pl.BlockSpec((tm, tk), lambda i, k: (i, k)),
                  pl.BlockSpec((tk, N), lambda i, k: (k, 0)),
                  pl.BlockSpec((1, N), lambda i, k: (0, 0)),
                  pl.BlockSpec((tm, N), lambda i, k: (i, 0))],
        out_specs=pl.BlockSpec((tm, N), lambda i, k: (i, 0)),
        out_shape=jax.ShapeDtypeStruct((M, N), F32),
        scratch_shapes=[pltpu.VMEM((tm, N), F32)],
        compiler_params=_cparams(("parallel", "arbitrary")),
        name="matmul_norm_res",
    )(a, w, g.reshape(1, N), res)


def _top2(h, router):
    lg = _dot_f32(h, router)
    lane = lax.broadcasted_iota(jnp.int32, lg.shape, 1)
    lg = jnp.where(lane < N_EXPERTS, lg, -jnp.inf)
    m1 = jnp.max(lg, axis=-1, keepdims=True)
    i1 = jnp.min(jnp.where(lg == m1, lane, LANES), axis=-1, keepdims=True)
    lg2 = jnp.where(lane == i1, -jnp.inf, lg)
    m2 = jnp.max(lg2, axis=-1, keepdims=True)
    i2 = jnp.min(jnp.where(lg2 == m2, lane, LANES), axis=-1, keepdims=True)
    e2 = jnp.exp(m2 - m1)
    den = 1.0 + e2
    return lane, i1, i2, 1.0 / den, e2 / den


def _top2_gates(h, router):
    lane, i1, i2, w1, w2 = _top2(h, router)
    return jnp.where(lane == i1, w1, 0.0) + jnp.where(lane == i2, w2, 0.0)


def _ffn_kernel(x_ref, gpre_ref, gpost_ref, router_ref, wg_ref, wu_ref, wd_ref, o_ref,
                h_ref, acc_ref, gates_ref, *, moe):
    e = pl.program_id(1)
    f = pl.program_id(2)

    @pl.when((e == 0) & (f == 0))
    def _():
        h = _rms(x_ref[...], gpre_ref[...])
        h_ref[...] = h.astype(h_ref.dtype)
        acc_ref[...] = jnp.zeros_like(acc_ref)
        if moe:
            gates_ref[...] = _top2_gates(h, router_ref[...])
        else:
            gates_ref[...] = jnp.zeros_like(gates_ref)

    hb = h_ref[...]
    act = jax.nn.silu(_mm(hb, wg_ref[0])) * _mm(hb, wu_ref[0])
    if moe:
        gates = gates_ref[...]
        lane = lax.broadcasted_iota(jnp.int32, gates.shape, 1)
        act = act * jnp.sum(jnp.where(lane == e, gates, 0.0), axis=-1, keepdims=True)
    acc_ref[...] += _mm(act, wd_ref[0])

    @pl.when((e == pl.num_programs(1) - 1) & (f == pl.num_programs(2) - 1))
    def _():
        o_ref[...] = x_ref[...] + _rms(acc_ref[...], gpost_ref[...])


def ffn(x, gpre, gpost, router, wg, wu, wd, tm, tf, moe):
    M, K = x.shape
    E, _, F = wg.shape
    return pl.pallas_call(
        functools.partial(_ffn_kernel, moe=moe),
        grid=(M // tm, E, F // tf),
        in_specs=[pl.BlockSpec((tm, K), lambda i, e, f: (i, 0)),
                  pl.BlockSpec((1, K), lambda i, e, f: (0, 0)),
                  pl.BlockSpec((1, K), lambda i, e, f: (0, 0)),
                  pl.BlockSpec((K, LANES), lambda i, e, f: (0, 0)),
                  pl.BlockSpec((1, K, tf), lambda i, e, f: (e, 0, f)),
                  pl.BlockSpec((1, K, tf), lambda i, e, f: (e, 0, f)),
                  pl.BlockSpec((1, tf, K), lambda i, e, f: (e, f, 0))],
        out_specs=pl.BlockSpec((tm, K), lambda i, e, f: (i, 0)),
        out_shape=jax.ShapeDtypeStruct((M, K), F32),
        scratch_shapes=[pltpu.VMEM((tm, K), wg.dtype), pltpu.VMEM((tm, K), F32), pltpu.VMEM((tm, LANES), F32)],
        compiler_params=_cparams(("parallel", "arbitrary", "arbitrary")),
        name="moe_ffn" if moe else "dense_ffn",
    )(x, gpre.reshape(1, K), gpost.reshape(1, K), router, wg, wu, wd)


INFO_E1, INFO_E2, INFO_R1, INFO_R2, INFO_W1, INFO_W2 = range(6)


def _route_kernel(x_ref, gpre_ref, router_ref, tril_ref, info_ref, cnt_ref, carry_ref):
    i = pl.program_id(0)

    @pl.when(i == 0)
    def _():
        carry_ref[...] = jnp.zeros_like(carry_ref)

    lane, i1, i2, w1, w2 = _top2(_rms(x_ref[...], gpre_ref[...]), router_ref[...])
    sel = jnp.where((lane == i1) | (lane == i2), 1.0, 0.0)
    incl = _dot(tril_ref[...], sel.astype(BF16))
    rank = incl - sel + carry_ref[...]
    r1 = jnp.sum(jnp.where(lane == i1, rank, 0.0), axis=-1, keepdims=True)
    r2 = jnp.sum(jnp.where(lane == i2, rank, 0.0), axis=-1, keepdims=True)
    info = jnp.zeros(sel.shape, F32)
    for idx, val in ((INFO_E1, i1.astype(F32)), (INFO_E2, i2.astype(F32)), (INFO_R1, r1), (INFO_R2, r2),
                     (INFO_W1, w1), (INFO_W2, w2)):
        info = jnp.where(lane == idx, val, info)
    info_ref[...] = info
    carry_ref[...] += incl[incl.shape[0] - 1:, :]
    cnt_ref[...] = carry_ref[...]


def moe_route(x, gpre, router, tm):
    M, K = x.shape
    tril = (jnp.arange(tm)[None, :] <= jnp.arange(tm)[:, None]).astype(BF16)
    return pl.pallas_call(
        _route_kernel,
        grid=(M // tm,),
        in_specs=[pl.BlockSpec((tm, K), lambda i: (i, 0)),
                  pl.BlockSpec((1, K), lambda i: (0, 0)),
                  pl.BlockSpec((K, LANES), lambda i: (0, 0)),
                  pl.BlockSpec((tm, tm), lambda i: (0, 0))],
        out_specs=[pl.BlockSpec((tm, LANES), lambda i: (i, 0)),
                   pl.BlockSpec((1, LANES), lambda i: (0, 0))],
        out_shape=[jax.ShapeDtypeStruct((M, LANES), F32), jax.ShapeDtypeStruct((1, LANES), F32)],
        scratch_shapes=[pltpu.VMEM((1, LANES), F32)],
        compiler_params=_cparams(("arbitrary",)),
        name="moe_route",
    )(x, gpre.reshape(1, K), router, tril)


def _dispatch_kernel(dest_ref, x_ref, zeros_ref, xs_ref, sem):
    del zeros_ref
    tm = x_ref.shape[0]
    base = pl.program_id(0) * tm

    def issue(r, carry):
        for k in range(2):
            d = dest_ref[(base + r) * 2 + k]
            pltpu.make_async_copy(x_ref.at[pl.ds(r, 1), :], xs_ref.at[pl.ds(d, 1), :], sem).start()
        return carry

    lax.fori_loop(0, tm, issue, 0)
    for k in range(2):
        pltpu.make_async_copy(x_ref, xs_ref.at[pl.ds(0, tm), :], sem).wait()


def moe_dispatch(x, dest, n_slots, tm):
    M, K = x.shape
    return pl.pallas_call(
        _dispatch_kernel,
        grid_spec=pltpu.PrefetchScalarGridSpec(
            num_scalar_prefetch=1,
            grid=(M // tm,),
            in_specs=[pl.BlockSpec((tm, K), lambda i, dest: (i, 0)),
                      pl.BlockSpec(memory_space=pl.ANY)],
            out_specs=pl.BlockSpec(memory_space=pl.ANY),
            scratch_shapes=[pltpu.SemaphoreType.DMA(())]),
        out_shape=jax.ShapeDtypeStruct((n_slots, K), F32),
        input_output_aliases={2: 0},
        compiler_params=_cparams(("arbitrary",)),
        name="moe_dispatch",
    )(dest, x, jnp.zeros((n_slots, K), F32))


def _experts_kernel(te_ref, tv_ref, xs_ref, gpre_ref, wg_ref, wu_ref, wd_ref, o_ref, h_ref, acc_ref):
    del te_ref
    i = pl.program_id(0)
    f = pl.program_id(1)
    last = pl.num_programs(1) - 1

    @pl.when(tv_ref[i] == 1)
    def _():
        @pl.when(f == 0)
        def _():
            h_ref[...] = _rms(xs_ref[...], gpre_ref[...]).astype(BF16)
            acc_ref[...] = jnp.zeros_like(acc_ref)

        hb = h_ref[...]
        act = jax.nn.silu(_dot(hb, wg_ref[0])) * _dot(hb, wu_ref[0])
        acc_ref[...] += _dot(act.astype(BF16), wd_ref[0])

        @pl.when(f == last)
        def _():
            o_ref[...] = acc_ref[...]

    @pl.when((tv_ref[i] == 0) & (f == last))
    def _():
        o_ref[...] = jnp.zeros_like(o_ref)


def moe_experts(xs, gpre, tile_expert, tile_valid, wg, wu, wd, tg, tf):
    S, K = xs.shape
    F = wg.shape[2]
    nf = F // tf
    fidx = lambda i, f, te, tv: jnp.where(tv[i] == 1, f, nf - 1)
    return pl.pallas_call(
        _experts_kernel,
        grid_spec=pltpu.PrefetchScalarGridSpec(
            num_scalar_prefetch=2,
            grid=(S // tg, nf),
            in_specs=[pl.BlockSpec((tg, K), lambda i, f, te, tv: (i, 0)),
                      pl.BlockSpec((1, K), lambda i, f, te, tv: (0, 0)),
                      pl.BlockSpec((1, K, tf), lambda i, f, te, tv: (te[i], 0, fidx(i, f, te, tv))),
                      pl.BlockSpec((1, K, tf), lambda i, f, te, tv: (te[i], 0, fidx(i, f, te, tv))),
                      pl.BlockSpec((1, tf, K), lambda i, f, te, tv: (te[i], fidx(i, f, te, tv), 0))],
            out_specs=pl.BlockSpec((tg, K), lambda i, f, te, tv: (i, 0)),
            scratch_shapes=[pltpu.VMEM((tg, K), BF16), pltpu.VMEM((tg, K), F32)]),
        out_shape=jax.ShapeDtypeStruct((S, K), F32),
        compiler_params=_cparams(("parallel", "arbitrary")),
        name="moe_experts",
    )(tile_expert, tile_valid, xs, gpre.reshape(1, K), wg, wu, wd)


def _combine_kernel(dest_ref, x_ref, info_ref, gpost_ref, ys_ref, o_ref, buf_ref, sem):
    tm = x_ref.shape[0]
    base = pl.program_id(0) * tm

    def issue(r, carry):
        for k in range(2):
            d = dest_ref[(base + r) * 2 + k]
            pltpu.make_async_copy(ys_ref.at[pl.ds(d, 1), :], buf_ref.at[k, pl.ds(r, 1), :], sem).start()
        return carry

    lax.fori_loop(0, tm, issue, 0)
    for k in range(2):
        pltpu.make_async_copy(ys_ref.at[pl.ds(0, tm), :], buf_ref.at[k], sem).wait()
    info = info_ref[...]
    y = info[:, INFO_W1:INFO_W1 + 1] * buf_ref[0] + info[:, INFO_W2:INFO_W2 + 1] * buf_ref[1]
    o_ref[...] = x_ref[...] + _rms(y, gpost_ref[...])


def moe_combine(x, info, gpost, ys, dest, tm):
    M, K = x.shape
    return pl.pallas_call(
        _combine_kernel,
        grid_spec=pltpu.PrefetchScalarGridSpec(
            num_scalar_prefetch=1,
            grid=(M // tm,),
            in_specs=[pl.BlockSpec((tm, K), lambda i, dest: (i, 0)),
                      pl.BlockSpec((tm, LANES), lambda i, dest: (i, 0)),
                      pl.BlockSpec((1, K), lambda i, dest: (0, 0)),
                      pl.BlockSpec(memory_space=pl.ANY)],
            out_specs=pl.BlockSpec((tm, K), lambda i, dest: (i, 0)),
            scratch_shapes=[pltpu.VMEM((2, tm, K), F32), pltpu.SemaphoreType.DMA(())]),
        out_shape=jax.ShapeDtypeStruct((M, K), F32),
        compiler_params=_cparams(("arbitrary",)),
        name="moe_combine",
    )(dest, x, info, gpost.reshape(1, K), ys)


def moe_routed(x, gpre, gpost, router, wg, wu, wd):
    M, K = x.shape
    tg = TG_MOE
    n_slots = 2 * M + N_EXPERTS * tg
    info, cnt = moe_route(x, gpre, router, TM_ROUTE)
    cnt = cnt[0, :N_EXPERTS].astype(jnp.int32)
    padded = (cnt + tg - 1) // tg * tg
    ends = jnp.cumsum(padded)
    offs = ends - padded
    ids = info[:, INFO_E1:INFO_E2 + 1].astype(jnp.int32)
    ranks = info[:, INFO_R1:INFO_R2 + 1].astype(jnp.int32)
    dest = (offs[ids] + ranks).reshape(2 * M)
    tile_start = jnp.arange(n_slots // tg, dtype=jnp.int32) * tg
    tile_valid = (tile_start < ends[-1]).astype(jnp.int32)
    tile_expert = jnp.minimum(jnp.sum((tile_start[:, None] >= ends[None, :]).astype(jnp.int32), axis=1),
                              N_EXPERTS - 1)
    tile_expert = jnp.where(tile_valid == 1, tile_expert, tile_expert[jnp.maximum(ends[-1] // tg - 1, 0)])
    xs = moe_dispatch(x, dest, n_slots, TM_ROUTE)
    ys = moe_experts(xs, gpre, tile_expert, tile_valid, wg, wu, wd, tg, TF_MOE)
    return moe_combine(x, info, gpost, ys, dest, TM_COMBINE)


def _xattn_kernel(x_ref, k_ref, v_ref, wq_ref, wo_ref, gpre_ref, gpost_ref, o_ref):
    x = x_ref[0]
    h = _rms(x, gpre_ref[...]).astype(BF16)
    q = _dot(h, wq_ref[...])
    k = k_ref[0].astype(BF16)
    v = v_ref[0].astype(BF16)
    outs = []
    for hd in range(X_HEADS):
        sl = slice(hd * X_HEAD_DIM, (hd + 1) * X_HEAD_DIM)
        s = _dot_nt(q[:, sl].astype(BF16), k[:, sl]) * (X_HEAD_DIM ** -0.5)
        e = jnp.exp(s - jnp.max(s, axis=-1, keepdims=True))
        p = e / jnp.sum(e, axis=-1, keepdims=True)
        outs.append(_dot(p.astype(BF16), v[:, sl]))
    o = jnp.concatenate(outs, axis=-1).astype(BF16)
    o_ref[0] = x + _rms(_dot(o, wo_ref[...]), gpost_ref[...])


def xattn_prompt(x, k, v, wq, wo, gpre, gpost, tm):
    B, T, K = x.shape
    return pl.pallas_call(
        _xattn_kernel,
        grid=(B, T // tm),
        in_specs=[pl.BlockSpec((1, tm, K), lambda b, t: (b, t, 0)),
                  pl.BlockSpec((1, MEM_LEN, K), lambda b, t: (b, 0, 0)),
                  pl.BlockSpec((1, MEM_LEN, K), lambda b, t: (b, 0, 0)),
                  pl.BlockSpec((K, K), lambda b, t: (0, 0)),
                  pl.BlockSpec((K, K), lambda b, t: (0, 0)),
                  pl.BlockSpec((1, K), lambda b, t: (0, 0)),
                  pl.BlockSpec((1, K), lambda b, t: (0, 0))],
        out_specs=pl.BlockSpec((1, tm, K), lambda b, t: (b, t, 0)),
        out_shape=jax.ShapeDtypeStruct((B, T, K), F32),
        compiler_params=_cparams(("parallel", "parallel")),
        name="xattn_prompt",
    )(x, k, v, wq, wo, gpre.reshape(1, K), gpost.reshape(1, K))


def _xattn_sample_kernel(q_ref, k_ref, v_ref, o_ref):
    q = q_ref[0]
    k = k_ref[0, 0]
    v = v_ref[0, 0]
    s = jnp.sum(k * q[None], axis=-1, keepdims=True) * (X_HEAD_DIM ** -0.5)
    e = jnp.exp(s - jnp.max(s, axis=0, keepdims=True))
    p = e / jnp.sum(e, axis=0, keepdims=True)
    o_ref[0] = jnp.sum(p * v, axis=0)


def xattn_sample_core(q, k, v, l):
    B = q.shape[0]
    kv_spec = pl.BlockSpec((1, 1, MEM_LEN, X_HEADS, X_HEAD_DIM), lambda b: (l, b, 0, 0, 0))
    return pl.pallas_call(
        _xattn_sample_kernel,
        grid=(B,),
        in_specs=[pl.BlockSpec((1, X_HEADS, X_HEAD_DIM), lambda b: (b, 0, 0)), kv_spec, kv_spec],
        out_specs=pl.BlockSpec((1, X_HEADS, X_HEAD_DIM), lambda b: (b, 0, 0)),
        out_shape=jax.ShapeDtypeStruct((B, X_HEADS, X_HEAD_DIM), F32),
        compiler_params=_cparams(("parallel",)),
        name="xattn_sample",
    )(q, k, v)


def _mixer_even_kernel(u_ref, v_ref, z_ref, xbc_ref, dt_ref, ws_ref, bst_ref, lng_ref, lnb_ref, cw_ref, cb_ref,
                       dtb_ref, aexp_ref, dexp_ref, bnorm_ref, tril_ref, expand_ref,
                       yab_ref, ssm_ref, xp_ref, st_ref):
    c = pl.program_id(1)
    L = A_CHUNK

    @pl.when(c == 0)
    def _():
        xp_ref[0:SUBLANES, :] = jnp.zeros((SUBLANES, B_CONV_DIM), F32)
        st_ref[...] = jnp.zeros_like(st_ref)

    row = lax.broadcasted_iota(jnp.int32, (L, L), 0)
    col = lax.broadcasted_iota(jnp.int32, (L, L), 1)
    causal = col <= row

    gu = jax.nn.gelu(u_ref[0])
    vb = _layernorm(jax.nn.gelu(v_ref[0]), lng_ref[...], lnb_ref[...]).astype(BF16)
    for hh in range(A_HEADS):
        sl = slice(hh * LANES, (hh + 1) * LANES)
        w = jnp.where(causal, ws_ref[hh], 0.0).astype(BF16)
        mix = _dot(w, vb[:, sl]) + bst_ref[:, hh:hh + 1]
        yab_ref[0, :, sl] = (gu[:, sl] * mix).astype(BF16)

    x = xbc_ref[0]
    xp_ref[SUBLANES:SUBLANES + L, :] = x
    conv = cb_ref[...] + cw_ref[B_CONV - 1:B_CONV, :] * x
    for k in range(B_CONV - 1):
        conv = conv + cw_ref[k:k + 1, :] * xp_ref[pl.ds(SUBLANES - (B_CONV - 1) + k, L), :]
    xp_ref[0:SUBLANES, :] = x[L - SUBLANES:L, :]
    xa = jax.nn.silu(conv)
    xs = xa[:, :D]
    bm = xa[:, D:D + B_GROUPS * B_STATE]
    cm = xa[:, D + B_GROUPS * B_STATE:]

    tril = tril_ref[...]
    dtf = _softplus(dt_ref[0] + dtb_ref[...])
    dt_x = _sel_right(dtf, expand_ref[...], _split2)
    cs_x = _sel_left(tril, dt_x * aexp_ref[...], _split3)
    ecs_x = jnp.exp(cs_x)
    last_x = cs_x[L - 1:L, :]
    xdt = xs * dt_x
    xdt_b = xdt.astype(BF16)
    xdec_b = (xdt * jnp.exp(last_x - cs_x)).astype(BF16)
    lane = lax.broadcasted_iota(jnp.int32, (L, LANES), 1)
    y_parts = []
    for g in range(B_GROUPS):
        gs = slice(g * B_GROUP_W, (g + 1) * B_GROUP_W)
        bg = bm[:, g * B_STATE:(g + 1) * B_STATE]
        cg = cm[:, g * B_STATE:(g + 1) * B_STATE].astype(BF16)
        gmat = _dot_nt(cg, bg.astype(BF16))
        st = st_ref[g]
        y_off = ecs_x[:, gs] * _dot(cg, st.astype(BF16))
        st_new = st * jnp.exp(last_x[:, gs]) + _dot(bg.T.astype(BF16), xdec_b[:, gs])
        st_ref[g] = st_new
        for pair in range(B_GROUP_W // LANES):
            base = g * B_GROUP_W + pair * LANES
            cs_t = cs_x[:, base:base + LANES].T
            ms = []
            for half in range(2):
                ch = half * B_HEAD_DIM
                diff = cs_x[:, base + ch:base + ch + 1] - cs_t[ch:ch + 1, :]
                ms.append((gmat * jnp.exp(jnp.where(causal, diff, -jnp.inf))).astype(BF16))
            xp2 = xdt_b[:, base:base + LANES]
            rhs = jnp.concatenate([jnp.where(lane < B_HEAD_DIM, xp2, jnp.zeros_like(xp2)),
                                   jnp.where(lane >= B_HEAD_DIM, xp2, jnp.zeros_like(xp2))], axis=0)
            y_parts.append(_dot(jnp.concatenate(ms, axis=1), rhs) + y_off[:, pair * LANES:(pair + 1) * LANES])
    y = jnp.concatenate(y_parts, axis=-1) + dexp_ref[...] * xs
    yb = _group_rms(y * jax.nn.silu(z_ref[0]), bnorm_ref[...], B_GROUP_W)
    yab_ref[0, :, D:] = yb.astype(BF16)

    @pl.when(c == pl.num_programs(1) - 1)
    def _():
        for g in range(B_GROUPS):
            ssm_ref[0, g] = st_ref[g].T


def _even_consts(P):
    head_of_ch = jnp.arange(D) // B_HEAD_DIM
    expand = (jnp.arange(LANES)[:, None] == head_of_ch[None, :]).astype(BF16)
    tril = (jnp.arange(A_CHUNK)[None, :] <= jnp.arange(A_CHUNK)[:, None]).astype(BF16)
    aexp = jnp.repeat(-jnp.exp(P["b_a_log"][0].astype(F32)), B_HEAD_DIM).reshape(1, D)
    dexp = jnp.repeat(P["b_d"][0].astype(F32), B_HEAD_DIM).reshape(1, D)
    dtb = jnp.pad(P["b_dt_bias"][0].astype(F32), (0, LANES - B_HEADS)).reshape(1, LANES)
    return expand, tril, aexp, dexp, dtb


def mixer_even_prompt(proj, P):
    B, T, _ = proj.shape
    L = A_CHUNK
    expand, tril, aexp, dexp, dtb = _even_consts(P)
    row = lambda a: a.reshape(1, -1)
    full = lambda shape: pl.BlockSpec(shape, lambda b, c: (0,) * len(shape))
    return pl.pallas_call(
        _mixer_even_kernel,
        grid=(B, T // L),
        in_specs=[pl.BlockSpec((1, L, D), lambda b, c: (b, c, 0)),
                  pl.BlockSpec((1, L, D), lambda b, c: (b, c, 1)),
                  pl.BlockSpec((1, L, D), lambda b, c: (b, c, 2)),
                  pl.BlockSpec((1, L, B_CONV_DIM), lambda b, c: (b, c, 2)),
                  pl.BlockSpec((1, L, LANES), lambda b, c: (b, c, DT_COL_BLOCK)),
                  full((A_HEADS, L, L)), full((L, A_HEADS)), full((1, D)), full((1, D)),
                  full((B_CONV, B_CONV_DIM)), full((1, B_CONV_DIM)), full((1, LANES)), full((1, D)), full((1, D)),
                  full((1, D)), full((L, L)), full((LANES, D))],
        out_specs=[pl.BlockSpec((1, L, 2 * D), lambda b, c: (b, c, 0)),
                   pl.BlockSpec((1, B_GROUPS, B_GROUP_W, B_STATE), lambda b, c: (b, 0, 0, 0))],
        out_shape=[jax.ShapeDtypeStruct((B, T, 2 * D), BF16),
                   jax.ShapeDtypeStruct((B, B_GROUPS, B_GROUP_W, B_STATE), F32)],
        scratch_shapes=[pltpu.VMEM((SUBLANES + L, B_CONV_DIM), F32),
                        pltpu.VMEM((B_GROUPS, B_STATE, B_GROUP_W), F32)],
        compiler_params=_cparams(("parallel", "arbitrary")),
        name="mixer_even_prompt",
    )(proj, proj, proj, proj, proj, P["a_ws"][0], P["a_bs"][0].T, row(P["a_ln_g"][0]), row(P["a_ln_b"][0]),
      P["b_conv_w"][0], row(P["b_conv_b"][0]), dtb, aexp, dexp, row(P["b_norm"][0]), tril, expand)


def _mixer_even_step_kernel(proj_ref, conv_ref, ssm_ref, ws0_ref, bs0_ref, lng_ref, lnb_ref, cw_ref, cb_ref,
                            dtb_ref, aexp_ref, dexp_ref, bnorm_ref, expand_ref,
                            yab_ref, av_ref, convo_ref, ssmo_ref):
    nb = proj_ref.shape[0]
    u = proj_ref[:, 0:D]
    v = proj_ref[:, D:2 * D]
    z = proj_ref[:, 2 * D:3 * D]
    x = proj_ref[:, 3 * D:3 * D + B_CONV_DIM]
    dt = proj_ref[:, DT_COL_BLOCK * LANES:(DT_COL_BLOCK + 1) * LANES]

    vln = _layernorm(jax.nn.gelu(v), lng_ref[...], lnb_ref[...])
    av_ref[...] = vln
    yab_ref[:, 0:D] = jax.nn.gelu(u) * (ws0_ref[...] * vln + bs0_ref[...])

    conv = cb_ref[...] + cw_ref[B_CONV - 1:B_CONV, :] * x
    for k in range(B_CONV - 1):
        conv = conv + cw_ref[k:k + 1, :] * conv_ref[k]
        if k > 0:
            convo_ref[k - 1] = conv_ref[k]
    convo_ref[B_CONV - 2] = x
    xa = jax.nn.silu(conv)
    xs = xa[:, :D]
    bm = xa[:, D:D + B_GROUPS * B_STATE]
    cm = xa[:, D + B_GROUPS * B_STATE:]
    dtf = _softplus(dt + dtb_ref[...])
    dt_x = _sel_right(dtf, expand_ref[...], _split3)
    dec_x = jnp.exp(dt_x * aexp_ref[...])
    xdt = xs * dt_x
    y_rows = []
    for g in range(B_GROUPS):
        gs = slice(g * B_GROUP_W, (g + 1) * B_GROUP_W)
        dec_t = _rows_to_cols(dec_x[:, gs])
        xdt_t = _rows_to_cols(xdt[:, gs])
        ys = []
        for j in range(nb):
            s_new = ssm_ref[j, g] * dec_t[:, j:j + 1] + xdt_t[:, j:j + 1] * bm[j:j + 1, g * B_STATE:(g + 1) * B_STATE]
            ssmo_ref[j, g] = s_new
            cj = jnp.broadcast_to(cm[j:j + 1, g * B_STATE:(g + 1) * B_STATE], (SUBLANES, B_STATE))
            ys.append(_dot_nt_f32(cj, s_new)[0:1, :])
        y_rows.append(jnp.concatenate(ys, axis=0))
    y = jnp.concatenate(y_rows, axis=-1) + dexp_ref[...] * xs
    yb = _group_rms(y * jax.nn.silu(z), bnorm_ref[...], B_GROUP_W)
    yab_ref[:, D:] = yb


def mixer_even_step(proj, conv0, ssm0, P, nb):
    B = proj.shape[0]
    expand, _, aexp, dexp, dtb = _even_consts(P)
    row = lambda a: a.reshape(1, -1)
    rep = lambda a: jnp.repeat(a.astype(F32), LANES).reshape(1, D)
    full = lambda shape: pl.BlockSpec(shape, lambda i: (0,) * len(shape))
    return pl.pallas_call(
        _mixer_even_step_kernel,
        grid=(B // nb,),
        in_specs=[pl.BlockSpec((nb, proj.shape[1]), lambda i: (i, 0)),
                  pl.BlockSpec((B_CONV - 1, nb, B_CONV_DIM), lambda i: (0, i, 0)),
                  pl.BlockSpec((nb, B_GROUPS, B_GROUP_W, B_STATE), lambda i: (i, 0, 0, 0)),
                  full((1, D)), full((1, D)), full((1, D)), full((1, D)),
                  full((B_CONV, B_CONV_DIM)), full((1, B_CONV_DIM)), full((1, LANES)), full((1, D)), full((1, D)),
                  full((1, D)), full((LANES, D))],
        out_specs=[pl.BlockSpec((nb, 2 * D), lambda i: (i, 0)),
                   pl.BlockSpec((nb, D), lambda i: (i, 0)),
                   pl.BlockSpec((B_CONV - 1, nb, B_CONV_DIM), lambda i: (0, i, 0)),
                   pl.BlockSpec((nb, B_GROUPS, B_GROUP_W, B_STATE), lambda i: (i, 0, 0, 0))],
        out_shape=[jax.ShapeDtypeStruct((B, 2 * D), F32),
                   jax.ShapeDtypeStruct((B, D), F32),
                   jax.ShapeDtypeStruct((B_CONV - 1, B, B_CONV_DIM), F32),
                   jax.ShapeDtypeStruct((B, B_GROUPS, B_GROUP_W, B_STATE), F32)],
        compiler_params=_cparams(("parallel",)),
        name="mixer_even_step",
    )(proj, conv0, ssm0, rep(P["a_ws"][0][:, 0, 0]), rep(P["a_bs"][0][:, 0]), row(P["a_ln_g"][0]), row(P["a_ln_b"][0]),
      P["b_conv_w"][0], row(P["b_conv_b"][0]), dtb, aexp, dexp, row(P["b_norm"][0]), expand)


def _gla_gates(q_raw, f_raw, lb):
    fg = lb + (1.0 - lb) * jax.nn.sigmoid(f_raw)
    return jax.nn.silu(q_raw), fg, 1.0 - fg


def _hgrn_kernel(q_ref, f_ref, i_ref, g_ref, lb_ref, cnorm_ref, tril_ref, o_ref, s_ref, st_ref):
    c = pl.program_id(1)
    L = C_CHUNK

    @pl.when(c == 0)
    def _():
        st_ref[...] = jnp.zeros_like(st_ref)

    row = lax.broadcasted_iota(jnp.int32, (L, L), 0)
    col = lax.broadcasted_iota(jnp.int32, (L, L), 1)
    causal = col <= row
    q, fg, k = _gla_gates(q_ref[0], f_ref[0], lb_ref[...])
    v = i_ref[0]
    bc = _sel_left(tril_ref[...], jnp.log(fg), _split3)
    btot = bc[L - 1:L, :]
    q_in = (q * jnp.exp(bc)).astype(BF16)
    k_in = (k * jnp.exp(-bc)).astype(BF16)
    k_dec = jnp.concatenate([k * jnp.exp(btot - bc), jnp.zeros((LANES - L, D), F32)], axis=0).astype(BF16)
    vb = v.astype(BF16)
    outs = []
    for hh in range(C_HEADS):
        sl = slice(hh * C_KDIM, (hh + 1) * C_KDIM)
        att = jnp.where(causal, _dot_nt(q_in[:, sl], k_in[:, sl]), 0.0)
        st = st_ref[hh]
        outs.append(_dot(att.astype(BF16), vb[:, sl]) + _dot_nt(q_in[:, sl], st.astype(BF16)))
        st_ref[hh] = st * jnp.exp(btot[:, sl]) + _dot(_rows_to_cols(v[:, sl]).astype(BF16), k_dec[:, sl])
    o = _group_rms(jnp.concatenate(outs, axis=-1), cnorm_ref[...], C_KDIM)
    o_ref[0] = (o * jax.nn.silu(g_ref[0])).astype(BF16)

    @pl.when(c == pl.num_programs(1) - 1)
    def _():
        for hh in range(C_HEADS):
            s_ref[0, hh] = st_ref[hh].T


def hgrn_prompt(proj, lb, cnorm):
    B, T, _ = proj.shape
    L = C_CHUNK
    tril = (jnp.arange(L)[None, :] <= jnp.arange(L)[:, None]).astype(BF16)
    full = lambda shape: pl.BlockSpec(shape, lambda b, c: (0,) * len(shape))
    return pl.pallas_call(
        _hgrn_kernel,
        grid=(B, T // L),
        in_specs=[pl.BlockSpec((1, L, D), lambda b, c: (b, c, 0)),
                  pl.BlockSpec((1, L, D), lambda b, c: (b, c, 1)),
                  pl.BlockSpec((1, L, D), lambda b, c: (b, c, 2)),
                  pl.BlockSpec((1, L, D), lambda b, c: (b, c, 3)),
                  full((1, D)), full((1, D)), full((L, L))],
        out_specs=[pl.BlockSpec((1, L, D), lambda b, c: (b, c, 0)),
                   pl.BlockSpec((1, C_HEADS, C_KDIM, C_KDIM), lambda b, c: (b, 0, 0, 0))],
        out_shape=[jax.ShapeDtypeStruct((B, T, D), BF16),
                   jax.ShapeDtypeStruct((B, C_HEADS, C_KDIM, C_KDIM), F32)],
        scratch_shapes=[pltpu.VMEM((C_HEADS, C_KDIM, C_KDIM), F32)],
        compiler_params=_cparams(("parallel", "arbitrary")),
        name="hgrn_prompt",
    )(proj, proj, proj, proj, lb.reshape(1, D), cnorm.reshape(1, D), tril)


def _hgrn_step_kernel(proj_ref, s_ref, lb_ref, cnorm_ref, o_ref, so_ref):
    nb = proj_ref.shape[0]
    q, fg, k = _gla_gates(proj_ref[:, 0:D], proj_ref[:, D:2 * D], lb_ref[...])
    v = proj_ref[:, 2 * D:3 * D]
    g = proj_ref[:, 3 * D:4 * D]
    outs = []
    for hh in range(C_HEADS):
        sl = slice(hh * C_KDIM, (hh + 1) * C_KDIM)
        fg_t = _rows_to_cols(fg[:, sl])
        k_t = _rows_to_cols(k[:, sl])
        rows = []
        for j in range(nb):
            s_new = s_ref[j, hh] * fg_t[:, j:j + 1] + k_t[:, j:j + 1] * v[j:j + 1, sl]
            so_ref[j, hh] = s_new
            qj = jnp.broadcast_to(q[j:j + 1, sl], (SUBLANES, C_KDIM))
            rows.append(_dot_f32(qj, s_new)[0:1, :])
        outs.append(jnp.concatenate(rows, axis=0))
    o = _group_rms(jnp.concatenate(outs, axis=-1), cnorm_ref[...], C_KDIM)
    o_ref[...] = o * jax.nn.silu(g)


def hgrn_step(proj, s0, lb, cnorm, nb):
    B = proj.shape[0]
    full = lambda shape: pl.BlockSpec(shape, lambda i: (0,) * len(shape))
    return pl.pallas_call(
        _hgrn_step_kernel,
        grid=(B // nb,),
        in_specs=[pl.BlockSpec((nb, 4 * D), lambda i: (i, 0)),
                  pl.BlockSpec((nb, C_HEADS, C_KDIM, C_KDIM), lambda i: (i, 0, 0, 0)),
                  full((1, D)), full((1, D))],
        out_specs=[pl.BlockSpec((nb, D), lambda i: (i, 0)),
                   pl.BlockSpec((nb, C_HEADS, C_KDIM, C_KDIM), lambda i: (i, 0, 0, 0))],
        out_shape=[jax.ShapeDtypeStruct((B, D), F32),
                   jax.ShapeDtypeStruct((B, C_HEADS, C_KDIM, C_KDIM), F32)],
        compiler_params=_cparams(("parallel",)),
        name="hgrn_step",
    )(proj, s0, lb.reshape(1, D), cnorm.reshape(1, D))


TM_PROJ = 256
TM_OUT = 512
TM_ATTN = 512
TM_FFN = 1024
TF_DENSE = 256
TF_MOE = 512
TG_MOE = 512
TM_ROUTE = 512
TM_COMBINE = 256
ROUTED_MIN_TOKENS = 8 * TG_MOE
STEP_NB = 8
TN_STEP = 512
TK_STEP = 512
IN0_STEP_PAD = 5120


def _prep_weights(P):
    W = {}
    W["ev_w_in"] = jnp.pad(P["ev_w_in"][0], ((0, 0), (0, IN0_PAD - IN0))).astype(BF16)
    W["ev_w_out"] = P["ev_w_out"][0].astype(BF16)
    W["od_w_in"] = P["od_w_in"][0].astype(BF16)
    W["od_w_out"] = P["od_w_out"][0].astype(BF16)
    W["xa_wq"] = P["xa_wq"].astype(BF16)
    W["xa_wo"] = P["xa_wo"].astype(BF16)
    W["xa_wkv"] = jnp.concatenate([P["xa_wk"], P["xa_wv"]], axis=-1).astype(BF16)
    W["ffn"] = tuple(P[n].astype(BF16) for n in ("ffn_w_gate", "ffn_w_up", "ffn_w_down"))
    W["moe"] = tuple(P[n][0].astype(BF16) for n in ("moe_w_gate", "moe_w_up", "moe_w_down"))
    W["router"] = jnp.pad(P["moe_router"][0].astype(F32), ((0, 0), (0, LANES - N_EXPERTS)))
    lbp = jax.nn.softmax(P["hgrn_lb_logits"].astype(F32), axis=0)
    W["lower_bounds"] = jnp.cumsum(lbp, axis=0) - lbp[0]
    return W


def _channel_mix(x2, l, P, W, tm):
    if l == 0:
        wg, wu, wd = W["ffn"]
        return ffn(x2, P["norm_ffn_pre"][l], P["norm_ffn_post"][l], W["router"], wg, wu, wd, tm, TF_DENSE, False)
    wg, wu, wd = W["moe"]
    if x2.shape[0] >= ROUTED_MIN_TOKENS:
        return moe_routed(x2, P["norm_ffn_pre"][l], P["norm_ffn_post"][l], W["router"], wg, wu, wd)
    return ffn(x2, P["norm_ffn_pre"][l], P["norm_ffn_post"][l], W["router"], wg, wu, wd, tm, TF_MOE, True)


def _trunk_prompt(x, mem_k, mem_v, P, W):
    B, T, _ = x.shape
    M = B * T
    x2 = x.reshape(M, D)
    proj = norm_matmul(x2, P["norm_mix_pre"][0], W["ev_w_in"], TM_PROJ).reshape(B, T, IN0_PAD)
    yab, ssm = mixer_even_prompt(proj, P)
    conv = proj[:, T - (B_CONV - 1):, 3 * D:3 * D + B_CONV_DIM]
    x2 = matmul_norm_res(yab.reshape(M, 2 * D), W["ev_w_out"], P["norm_mix_post"][0], x2, TM_OUT)
    x2 = xattn_prompt(x2.reshape(B, T, D), mem_k[0], mem_v[0], W["xa_wq"][0], W["xa_wo"][0],
                      P["norm_x_pre"][0], P["norm_x_post"][0], TM_ATTN).reshape(M, D)
    x2 = _channel_mix(x2, 0, P, W, TM_FFN)
    proj = norm_matmul(x2, P["norm_mix_pre"][1], W["od_w_in"], TM_PROJ).reshape(B, T, 4 * D)
    o, hgrn = hgrn_prompt(proj, W["lower_bounds"][1], P["c_norm"][0])
    x2 = matmul_norm_res(o.reshape(M, D), W["od_w_out"], P["norm_mix_post"][1], x2, TM_OUT)
    x2 = xattn_prompt(x2.reshape(B, T, D), mem_k[1], mem_v[1], W["xa_wq"][1], W["xa_wo"][1],
                      P["norm_x_pre"][1], P["norm_x_post"][1], TM_ATTN).reshape(M, D)
    x2 = _channel_mix(x2, 1, P, W, TM_FFN)
    return x2.reshape(B, T, D), conv, ssm, hgrn


def _xattn_step(x2, l, mem_k, mem_v, P):
    B = x2.shape[0]
    q = norm_matmul(x2, P["norm_x_pre"][l], P["xa_wq"][l], B, TN_STEP)
    o = xattn_sample_core(q.reshape(B, X_HEADS, X_HEAD_DIM), mem_k, mem_v, l).reshape(B, D)
    return matmul_norm_res(o, P["xa_wo"][l], P["norm_x_post"][l], x2, B)


def _trunk_step(x, mem_k, mem_v, conv0, ssm0, hgrn0, P, W):
    B = x.shape[0]
    x2 = x.reshape(B, D)
    w_in = jnp.pad(P["ev_w_in"][0], ((0, 0), (0, IN0_STEP_PAD - IN0)))
    proj = norm_matmul(x2, P["norm_mix_pre"][0], w_in, B, TN_STEP)
    yab, av, conv, ssm = mixer_even_step(proj, jnp.swapaxes(conv0, 0, 1), ssm0, P, STEP_NB)
    x2 = matmul_norm_res(yab, P["ev_w_out"][0], P["norm_mix_post"][0], x2, B, TK_STEP)
    x2 = _xattn_step(x2, 0, mem_k, mem_v, P)
    x2 = ffn(x2, P["norm_ffn_pre"][0], P["norm_ffn_post"][0], W["router"],
             P["ffn_w_gate"], P["ffn_w_up"], P["ffn_w_down"], B, TF_DENSE, False)
    proj = norm_matmul(x2, P["norm_mix_pre"][1], P["od_w_in"][0], B, TN_STEP)
    o, hgrn = hgrn_step(proj, hgrn0, W["lower_bounds"][1], P["c_norm"][0], STEP_NB)
    x2 = matmul_norm_res(o, P["od_w_out"][0], P["norm_mix_post"][1], x2, B)
    x2 = _xattn_step(x2, 1, mem_k, mem_v, P)
    x2 = _channel_mix(x2, 1, P, W, B)
    return x2.reshape(B, 1, D), jnp.swapaxes(conv, 0, 1), ssm, hgrn, av


def kernel(x_prompt, x_sample, mem_prompt, cache_mem_k, cache_mem_v, state_conv, state_ssm, state_hgrn,
           norm_mix_pre, norm_mix_post, norm_x_pre, norm_x_post, norm_ffn_pre, norm_ffn_post, norm_mem,
           xa_wq, xa_wk, xa_wv, xa_wo,
           ev_w_in, a_ws, a_bs, a_ln_g, a_ln_b, b_conv_w, b_conv_b, b_dt_bias, b_a_log, b_d, b_norm, ev_w_out,
           ffn_w_gate, ffn_w_up, ffn_w_down,
           od_w_in, hgrn_lb_logits, c_norm, od_w_out,
           moe_router, moe_w_gate, moe_w_up, moe_w_down):
    P = dict(norm_mix_pre=norm_mix_pre, norm_mix_post=norm_mix_post, norm_x_pre=norm_x_pre, norm_x_post=norm_x_post,
             norm_ffn_pre=norm_ffn_pre, norm_ffn_post=norm_ffn_post, xa_wq=xa_wq, xa_wk=xa_wk, xa_wv=xa_wv,
             xa_wo=xa_wo, ev_w_in=ev_w_in, a_ws=a_ws, a_bs=a_bs, a_ln_g=a_ln_g, a_ln_b=a_ln_b, b_conv_w=b_conv_w,
             b_conv_b=b_conv_b, b_dt_bias=b_dt_bias, b_a_log=b_a_log, b_d=b_d, b_norm=b_norm, ev_w_out=ev_w_out,
             ffn_w_gate=ffn_w_gate, ffn_w_up=ffn_w_up, ffn_w_down=ffn_w_down, od_w_in=od_w_in,
             hgrn_lb_logits=hgrn_lb_logits, c_norm=c_norm, od_w_out=od_w_out, moe_router=moe_router,
             moe_w_gate=moe_w_gate, moe_w_up=moe_w_up, moe_w_down=moe_w_down)
    W = _prep_weights(P)
    depth = norm_mem.shape[0]
    bp, T, _ = x_prompt.shape
    bs = x_sample.shape[0]

    mem2 = mem_prompt.reshape(bp * MEM_LEN, D)
    kv = [norm_matmul(mem2, norm_mem[l], W["xa_wkv"][l], TM_PROJ) for l in range(depth)]
    mem_k_p = jnp.stack([t[:, :D] for t in kv]).reshape(depth, bp, MEM_LEN, D)
    mem_v_p = jnp.stack([t[:, D:] for t in kv]).reshape(depth, bp, MEM_LEN, D)
    y_p, conv_p, ssm_p, hgrn_p = _trunk_prompt(x_prompt, mem_k_p, mem_v_p, P, W)

    y_s, conv_s, ssm_s, hgrn_s, av_s = _trunk_step(
        x_sample, cache_mem_k, cache_mem_v,
        state_conv[0], state_ssm[0].reshape(bs, B_GROUPS, B_GROUP_W, B_STATE), state_hgrn[0], P, W)

    kv_shape = (depth, bp, MEM_LEN, X_HEADS, X_HEAD_DIM)
    ssm_shape = (B_GROUPS, B_GROUP_W // B_HEAD_DIM, B_HEAD_DIM, B_STATE)
    return (y_p, y_s, mem_k_p.reshape(kv_shape), mem_v_p.reshape(kv_shape),
            conv_p[None], ssm_p.reshape((1, bp) + ssm_shape), hgrn_p[None],
            conv_s[None], ssm_s.reshape((1, bs) + ssm_shape), hgrn_s[None], av_s.reshape(1, bs, 1, D))
```

```python
import functools

import jax
import jax.numpy as jnp
from jax import lax
from jax.experimental import pallas as pl
from jax.experimental.pallas import tpu as pltpu

F32 = jnp.float32
BF16 = jnp.bfloat16
EPS = 1e-6

D = 1024
LANES = 128
SUBLANES = 8
A_HEADS = 8
A_CHUNK = 128
B_HEADS = 16
B_HEAD_DIM = 64
B_GROUPS = 2
B_GROUP_W = 512
B_STATE = 128
B_CONV = 4
B_CONV_DIM = 1536
IN0 = 4624
IN0_PAD = 4736
DT_COL_BLOCK = 36
C_HEADS = 8
C_KDIM = 128
C_CHUNK = 64
C_ROWS = 128
X_HEADS = 4
X_HEAD_DIM = 256
MEM_LEN = 256
N_EXPERTS = 8

VMEM_LIMIT = 56 * 1024 * 1024


def _cparams(sem):
    return pltpu.CompilerParams(dimension_semantics=sem, vmem_limit_bytes=VMEM_LIMIT)


def _dot(a, b):
    return jnp.dot(a, b, preferred_element_type=F32)


def _dot_nt(a, b):
    return lax.dot_general(a, b, (((1,), (1,)), ((), ())), preferred_element_type=F32)


def _dot_f32(a, b):
    return jnp.dot(a, b, precision=lax.Precision.HIGHEST, preferred_element_type=F32)


def _dot_nt_f32(a, b):
    return lax.dot_general(a, b, (((1,), (1,)), ((), ())), precision=lax.Precision.HIGHEST,
                           preferred_element_type=F32)


def _mm(a, w):
    if w.dtype == F32:
        return _dot_f32(a.astype(F32), w)
    return _dot(a.astype(BF16), w)


def _rms(x, g):
    return x * lax.rsqrt(jnp.mean(x * x, axis=-1, keepdims=True) + EPS) * g


def _split2(x):
    hi = x.astype(BF16)
    lo = (x - hi.astype(F32)).astype(BF16)
    return hi, lo


def _split3(x):
    hi = x.astype(BF16)
    r = x - hi.astype(F32)
    mid = r.astype(BF16)
    lo = (r - mid.astype(F32)).astype(BF16)
    return hi, mid, lo


def _sel_left(m, x, parts):
    out = None
    for p in parts(x):
        t = _dot(m, p)
        out = t if out is None else out + t
    return out


def _sel_right(x, m, parts):
    out = None
    for p in parts(x):
        t = _dot(p, m)
        out = t if out is None else out + t
    return out


def _rows_to_cols(x):
    n, w = x.shape
    if n < LANES:
        x = jnp.concatenate([x, jnp.zeros((LANES - n, w), x.dtype)], axis=0)
    return x.T


def _softplus(x):
    return jnp.maximum(x, 0.0) + jnp.log1p(jnp.exp(-jnp.abs(x)))


def _layernorm(x, g, b):
    xc = x - jnp.mean(x, axis=-1, keepdims=True)
    return xc * lax.rsqrt(jnp.mean(xc * xc, axis=-1, keepdims=True) + EPS) * g + b


def _group_rms(x, g, width):
    parts = []
    for s in range(0, x.shape[-1], width):
        t = x[:, s:s + width]
        parts.append(t * lax.rsqrt(jnp.mean(t * t, axis=-1, keepdims=True) + EPS))
    return jnp.concatenate(parts, axis=-1) * g


def _norm_matmul_kernel(x_ref, g_ref, w_ref, o_ref):
    o_ref[...] = _mm(_rms(x_ref[...], g_ref[...]), w_ref[...])


def norm_matmul(x, g, w, tm, tn=None):
    M, K = x.shape
    N = w.shape[1]
    tn = N if tn is None else tn
    return pl.pallas_call(
        _norm_matmul_kernel,
        grid=(M // tm, N // tn),
        in_specs=[pl.BlockSpec((tm, K), lambda i, j: (i, 0)),
                  pl.BlockSpec((1, K), lambda i, j: (0, 0)),
                  pl.BlockSpec((K, tn), lambda i, j: (0, j))],
        out_specs=pl.BlockSpec((tm, tn), lambda i, j: (i, j)),
        out_shape=jax.ShapeDtypeStruct((M, N), F32),
        compiler_params=_cparams(("parallel", "parallel")),
        name="norm_matmul",
    )(x, g.reshape(1, K), w)


def _matmul_norm_res_kernel(a_ref, w_ref, g_ref, r_ref, o_ref, acc_ref):
    k = pl.program_id(1)

    @pl.when(k == 0)
    def _():
        acc_ref[...] = jnp.zeros_like(acc_ref)

    acc_ref[...] += _mm(a_ref[...], w_ref[...])

    @pl.when(k == pl.num_programs(1) - 1)
    def _():
        o_ref[...] = r_ref[...] + _rms(acc_ref[...], g_ref[...])


def matmul_norm_res(a, w, g, res, tm, tk=None):
    M, K = a.shape
    N = w.shape[1]
    tk = K if tk is None else tk
    return pl.pallas_call(
        _matmul_norm_res_kernel,
        grid=(M // tm, K // tk),
        in_specs=[pl.BlockSpec((tm, tk), lambda i, k: (i, k)),
                  pl.BlockSpec((tk, N), lambda i, k: (k, 0)),
                  pl.BlockSpec((1, N), lambda i, k: (0, 0)),
                  pl.BlockSpec((tm, N), lambda i, k: (i, 0))],
        out_specs=pl.BlockSpec((tm, N), lambda i, k: (i, 0)),
        out_shape=jax.ShapeDtypeStruct((M, N), F32),
        scratch_shapes=[pltpu.VMEM((tm, N), F32)],
        compiler_params=_cparams(("parallel", "arbitrary")),
        name="matmul_norm_res",
    )(a, w, g.reshape(1, N), res)


def _top2(h, router):
    lg = _dot_f32(h, router)
    lane = lax.broadcasted_iota(jnp.int32, lg.shape, 1)
    lg = jnp.where(lane < N_EXPERTS, lg, -jnp.inf)
    m1 = jnp.max(lg, axis=-1, keepdims=True)
    i1 = jnp.min(jnp.where(lg == m1, lane, LANES), axis=-1, keepdims=True)
    lg2 = jnp.where(lane == i1, -jnp.inf, lg)
    m2 = jnp.max(lg2, axis=-1, keepdims=True)
    i2 = jnp.min(jnp.where(lg2 == m2, lane, LANES), axis=-1, keepdims=True)
    e2 = jnp.exp(m2 - m1)
    den = 1.0 + e2
    return lane, i1, i2, 1.0 / den, e2 / den


def _top2_gates(h, router):
    lane, i1, i2, w1, w2 = _top2(h, router)
    return jnp.where(lane == i1, w1, 0.0) + jnp.where(lane == i2, w2, 0.0)


def _ffn_kernel(x_ref, gpre_ref, gpost_ref, router_ref, wg_ref, wu_ref, wd_ref, o_ref,
                h_ref, acc_ref, gates_ref, *, moe):
    e = pl.program_id(1)
    f = pl.program_id(2)

    @pl.when((e == 0) & (f == 0))
    def _():
        h = _rms(x_ref[...], gpre_ref[...])
        h_ref[...] = h.astype(h_ref.dtype)
        acc_ref[...] = jnp.zeros_like(acc_ref)
        if moe:
            gates_ref[...] = _top2_gates(h, router_ref[...])
        else:
            gates_ref[...] = jnp.zeros_like(gates_ref)

    hb = h_ref[...]
    act = jax.nn.silu(_mm(hb, wg_ref[0])) * _mm(hb, wu_ref[0])
    if moe:
        gates = gates_ref[...]
        lane = lax.broadcasted_iota(jnp.int32, gates.shape, 1)
        act = act * jnp.sum(jnp.where(lane == e, gates, 0.0), axis=-1, keepdims=True)
    acc_ref[...] += _mm(act, wd_ref[0])

    @pl.when((e == pl.num_programs(1) - 1) & (f == pl.num_programs(2) - 1))
    def _():
        o_ref[...] = x_ref[...] + _rms(acc_ref[...], gpost_ref[...])


def ffn(x, gpre, gpost, router, wg, wu, wd, tm, tf, moe):
    M, K = x.shape
    E, _, F = wg.shape
    return pl.pallas_call(
        functools.partial(_ffn_kernel, moe=moe),
        grid=(M // tm, E, F // tf),
        in_specs=[pl.BlockSpec((tm, K), lambda i, e, f: (i, 0)),
                  pl.BlockSpec((1, K), lambda i, e, f: (0, 0)),
                  pl.BlockSpec((1, K), lambda i, e, f: (0, 0)),
                  pl.BlockSpec((K, LANES), lambda i, e, f: (0, 0)),
                  pl.BlockSpec((1, K, tf), lambda i, e, f: (e, 0, f)),
                  pl.BlockSpec((1, K, tf), lambda i, e, f: (e, 0, f)),
                  pl.BlockSpec((1, tf, K), lambda i, e, f: (e, f, 0))],
        out_specs=pl.BlockSpec((tm, K), lambda i, e, f: (i, 0)),
        out_shape=jax.ShapeDtypeStruct((M, K), F32),
        scratch_shapes=[pltpu.VMEM((tm, K), wg.dtype), pltpu.VMEM((tm, K), F32), pltpu.VMEM((tm, LANES), F32)],
        compiler_params=_cparams(("parallel", "arbitrary", "arbitrary")),
        name="moe_ffn" if moe else "dense_ffn",
    )(x, gpre.reshape(1, K), gpost.reshape(1, K), router, wg, wu, wd)


INFO_E1, INFO_E2, INFO_R1, INFO_R2, INFO_W1, INFO_W2 = range(6)


def _route_kernel(x_ref, gpre_ref, router_ref, tril_ref, info_ref, cnt_ref, carry_ref):
    i = pl.program_id(0)

    @pl.when(i == 0)
    def _():
        carry_ref[...] = jnp.zeros_like(carry_ref)

    lane, i1, i2, w1, w2 = _top2(_rms(x_ref[...], gpre_ref[...]), router_ref[...])
    sel = jnp.where((lane == i1) | (lane == i2), 1.0, 0.0)
    incl = _dot(tril_ref[...], sel.astype(BF16))
    rank = incl - sel + carry_ref[...]
    r1 = jnp.sum(jnp.where(lane == i1, rank, 0.0), axis=-1, keepdims=True)
    r2 = jnp.sum(jnp.where(lane == i2, rank, 0.0), axis=-1, keepdims=True)
    info = jnp.zeros(sel.shape, F32)
    for idx, val in ((INFO_E1, i1.astype(F32)), (INFO_E2, i2.astype(F32)), (INFO_R1, r1), (INFO_R2, r2),
                     (INFO_W1, w1), (INFO_W2, w2)):
        info = jnp.where(lane == idx, val, info)
    info_ref[...] = info
    carry_ref[...] += incl[incl.shape[0] - 1:, :]
    cnt_ref[...] = carry_ref[...]


def moe_route(x, gpre, router, tm):
    M, K = x.shape
    tril = (jnp.arange(tm)[None, :] <= jnp.arange(tm)[:, None]).astype(BF16)
    return pl.pallas_call(
        _route_kernel,
        grid=(M // tm,),
        in_specs=[pl.BlockSpec((tm, K), lambda i: (i, 0)),
                  pl.BlockSpec((1, K), lambda i: (0, 0)),
                  pl.BlockSpec((K, LANES), lambda i: (0, 0)),
                  pl.BlockSpec((tm, tm), lambda i: (0, 0))],
        out_specs=[pl.BlockSpec((tm, LANES), lambda i: (i, 0)),
                   pl.BlockSpec((1, LANES), lambda i: (0, 0))],
        out_shape=[jax.ShapeDtypeStruct((M, LANES), F32), jax.ShapeDtypeStruct((1, LANES), F32)],
        scratch_shapes=[pltpu.VMEM((1, LANES), F32)],
        compiler_params=_cparams(("arbitrary",)),
        name="moe_route",
    )(x, gpre.reshape(1, K), router, tril)


def _dispatch_kernel(dest_ref, x_ref, zeros_ref, xs_ref, sem):
    del zeros_ref
    tm = x_ref.shape[0]
    base = pl.program_id(0) * tm

    def issue(r, carry):
        for k in range(2):
            d = dest_ref[(base + r) * 2 + k]
            pltpu.make_async_copy(x_ref.at[pl.ds(r, 1), :], xs_ref.at[pl.ds(d, 1), :], sem).start()
        return carry

    lax.fori_loop(0, tm, issue, 0, unroll=DMA_UNROLL)
    for k in range(2):
        pltpu.make_async_copy(x_ref, xs_ref.at[pl.ds(0, tm), :], sem).wait()


def moe_dispatch(x, dest, n_slots, tm):
    M, K = x.shape
    return pl.pallas_call(
        _dispatch_kernel,
        grid_spec=pltpu.PrefetchScalarGridSpec(
            num_scalar_prefetch=1,
            grid=(M // tm,),
            in_specs=[pl.BlockSpec((tm, K), lambda i, dest: (i, 0)),
                      pl.BlockSpec(memory_space=pl.ANY)],
            out_specs=pl.BlockSpec(memory_space=pl.ANY),
            scratch_shapes=[pltpu.SemaphoreType.DMA(())]),
        out_shape=jax.ShapeDtypeStruct((n_slots, K), F32),
        input_output_aliases={2: 0},
        compiler_params=_cparams(("arbitrary",)),
        name="moe_dispatch",
    )(dest, x, jnp.zeros((n_slots, K), F32))


def _experts_kernel(te_ref, tv_ref, xs_ref, gpre_ref, wg_ref, wu_ref, wd_ref, o_ref, h_ref, acc_ref):
    del te_ref
    i = pl.program_id(0)
    f = pl.program_id(1)
    last = pl.num_programs(1) - 1

    @pl.when(tv_ref[i] == 1)
    def _():
        @pl.when(f == 0)
        def _():
            h_ref[...] = _rms(xs_ref[...], gpre_ref[...]).astype(BF16)
            acc_ref[...] = jnp.zeros_like(acc_ref)

        hb = h_ref[...]
        act = jax.nn.silu(_dot(hb, wg_ref[0])) * _dot(hb, wu_ref[0])
        acc_ref[...] += _dot(act.astype(BF16), wd_ref[0])

        @pl.when(f == last)
        def _():
            o_ref[...] = acc_ref[...]

    @pl.when((tv_ref[i] == 0) & (f == last))
    def _():
        o_ref[...] = jnp.zeros_like(o_ref)


def moe_experts(xs, gpre, tile_expert, tile_valid, wg, wu, wd, tg, tf):
    S, K = xs.shape
    F = wg.shape[2]
    nf = F // tf
    fidx = lambda i, f, te, tv: jnp.where(tv[i] == 1, f, nf - 1)
    return pl.pallas_call(
        _experts_kernel,
        grid_spec=pltpu.PrefetchScalarGridSpec(
            num_scalar_prefetch=2,
            grid=(S // tg, nf),
            in_specs=[pl.BlockSpec((tg, K), lambda i, f, te, tv: (i, 0)),
                      pl.BlockSpec((1, K), lambda i, f, te, tv: (0, 0)),
                      pl.BlockSpec((1, K, tf), lambda i, f, te, tv: (te[i], 0, fidx(i, f, te, tv))),
                      pl.BlockSpec((1, K, tf), lambda i, f, te, tv: (te[i], 0, fidx(i, f, te, tv))),
                      pl.BlockSpec((1, tf, K), lambda i, f, te, tv: (te[i], fidx(i, f, te, tv), 0))],
            out_specs=pl.BlockSpec((tg, K), lambda i, f, te, tv: (i, 0)),
            scratch_shapes=[pltpu.VMEM((tg, K), BF16), pltpu.VMEM((tg, K), F32)]),
        out_shape=jax.ShapeDtypeStruct((S, K), F32),
        compiler_params=_cparams(("parallel", "arbitrary")),
        name="moe_experts",
    )(tile_expert, tile_valid, xs, gpre.reshape(1, K), wg, wu, wd)


def _combine_kernel(dest_ref, x_ref, info_ref, gpost_ref, ys_ref, o_ref, buf_ref, sem):
    tm = x_ref.shape[0]
    i = pl.program_id(0)

    def gather(tile, slot):
        def issue(r, carry):
            for k in range(2):
                d = dest_ref[(tile * tm + r) * 2 + k]
                pltpu.make_async_copy(ys_ref.at[pl.ds(d, 1), :], buf_ref.at[slot, k, pl.ds(r, 1), :],
                                      sem.at[slot]).start()
            return carry

        lax.fori_loop(0, tm, issue, 0, unroll=DMA_UNROLL)

    @pl.when(i == 0)
    def _():
        gather(0, 0)

    @pl.when(i + 1 < pl.num_programs(0))
    def _():
        gather(i + 1, (i + 1) % 2)

    slot = i % 2
    for k in range(2):
        pltpu.make_async_copy(ys_ref.at[pl.ds(0, tm), :], buf_ref.at[slot, k], sem.at[slot]).wait()
    info = info_ref[...]
    y = info[:, INFO_W1:INFO_W1 + 1] * buf_ref[slot, 0] + info[:, INFO_W2:INFO_W2 + 1] * buf_ref[slot, 1]
    o_ref[...] = x_ref[...] + _rms(y, gpost_ref[...])


def moe_combine(x, info, gpost, ys, dest, tm):
    M, K = x.shape
    return pl.pallas_call(
        _combine_kernel,
        grid_spec=pltpu.PrefetchScalarGridSpec(
            num_scalar_prefetch=1,
            grid=(M // tm,),
            in_specs=[pl.BlockSpec((tm, K), lambda i, dest: (i, 0)),
                      pl.BlockSpec((tm, LANES), lambda i, dest: (i, 0)),
                      pl.BlockSpec((1, K), lambda i, dest: (0, 0)),
                      pl.BlockSpec(memory_space=pl.ANY)],
            out_specs=pl.BlockSpec((tm, K), lambda i, dest: (i, 0)),
            scratch_shapes=[pltpu.VMEM((2, 2, tm, K), F32), pltpu.SemaphoreType.DMA((2,))]),
        out_shape=jax.ShapeDtypeStruct((M, K), F32),
        compiler_params=_cparams(("arbitrary",)),
        name="moe_combine",
    )(dest, x, info, gpost.reshape(1, K), ys)


def moe_routed(x, gpre, gpost, router, wg, wu, wd):
    M, K = x.shape
    tg = TG_MOE
    n_slots = 2 * M + N_EXPERTS * tg
    info, cnt = moe_route(x, gpre, router, TM_ROUTE)
    cnt = cnt[0, :N_EXPERTS].astype(jnp.int32)
    padded = (cnt + tg - 1) // tg * tg
    ends = jnp.cumsum(padded)
    offs = ends - padded
    ids = info[:, INFO_E1:INFO_E2 + 1].astype(jnp.int32)
    ranks = info[:, INFO_R1:INFO_R2 + 1].astype(jnp.int32)
    dest = (offs[ids] + ranks).reshape(2 * M)
    tile_start = jnp.arange(n_slots // tg, dtype=jnp.int32) * tg
    tile_valid = (tile_start < ends[-1]).astype(jnp.int32)
    tile_expert = jnp.minimum(jnp.sum((tile_start[:, None] >= ends[None, :]).astype(jnp.int32), axis=1),
                              N_EXPERTS - 1)
    tile_expert = jnp.where(tile_valid == 1, tile_expert, tile_expert[jnp.maximum(ends[-1] // tg - 1, 0)])
    xs = moe_dispatch(x, dest, n_slots, TM_DISPATCH)
    ys = moe_experts(xs, gpre, tile_expert, tile_valid, wg, wu, wd, tg, TF_ROUTED)
    return moe_combine(x, info, gpost, ys, dest, TM_COMBINE)


def _xattn_kernel(x_ref, k_ref, v_ref, wq_ref, wo_ref, gpre_ref, gpost_ref, o_ref):
    x = x_ref[0]
    h = _rms(x, gpre_ref[...]).astype(BF16)
    q = _dot(h, wq_ref[...])
    k = k_ref[0].astype(BF16)
    v = v_ref[0].astype(BF16)
    outs = []
    for hd in range(X_HEADS):
        sl = slice(hd * X_HEAD_DIM, (hd + 1) * X_HEAD_DIM)
        s = _dot_nt(q[:, sl].astype(BF16), k[:, sl]) * (X_HEAD_DIM ** -0.5)
        e = jnp.exp(s - jnp.max(s, axis=-1, keepdims=True))
        p = e / jnp.sum(e, axis=-1, keepdims=True)
        outs.append(_dot(p.astype(BF16), v[:, sl]))
    o = jnp.concatenate(outs, axis=-1).astype(BF16)
    o_ref[0] = x + _rms(_dot(o, wo_ref[...]), gpost_ref[...])


def xattn_prompt(x, k, v, wq, wo, gpre, gpost, tm):
    B, T, K = x.shape
    return pl.pallas_call(
        _xattn_kernel,
        grid=(B, T // tm),
        in_specs=[pl.BlockSpec((1, tm, K), lambda b, t: (b, t, 0)),
                  pl.BlockSpec((1, MEM_LEN, K), lambda b, t: (b, 0, 0)),
                  pl.BlockSpec((1, MEM_LEN, K), lambda b, t: (b, 0, 0)),
                  pl.BlockSpec((K, K), lambda b, t: (0, 0)),
                  pl.BlockSpec((K, K), lambda b, t: (0, 0)),
                  pl.BlockSpec((1, K), lambda b, t: (0, 0)),
                  pl.BlockSpec((1, K), lambda b, t: (0, 0))],
        out_specs=pl.BlockSpec((1, tm, K), lambda b, t: (b, t, 0)),
        out_shape=jax.ShapeDtypeStruct((B, T, K), F32),
        compiler_params=_cparams(("parallel", "parallel")),
        name="xattn_prompt",
    )(x, k, v, wq, wo, gpre.reshape(1, K), gpost.reshape(1, K))


def _xattn_sample_kernel(q_ref, k_ref, v_ref, o_ref):
    for j in range(q_ref.shape[0]):
        q = q_ref[j]
        k = k_ref[0, j]
        v = v_ref[0, j]
        s = jnp.sum(k * q[None], axis=-1, keepdims=True) * (X_HEAD_DIM ** -0.5)
        e = jnp.exp(s - jnp.max(s, axis=0, keepdims=True))
        p = e / jnp.sum(e, axis=0, keepdims=True)
        o_ref[j] = jnp.sum(p * v, axis=0)


def xattn_sample_core(q, k, v, l, nb):
    B = q.shape[0]
    kv_spec = pl.BlockSpec((1, nb, MEM_LEN, X_HEADS, X_HEAD_DIM), lambda b: (l, b, 0, 0, 0))
    return pl.pallas_call(
        _xattn_sample_kernel,
        grid=(B // nb,),
        in_specs=[pl.BlockSpec((nb, X_HEADS, X_HEAD_DIM), lambda b: (b, 0, 0)), kv_spec, kv_spec],
        out_specs=pl.BlockSpec((nb, X_HEADS, X_HEAD_DIM), lambda b: (b, 0, 0)),
        out_shape=jax.ShapeDtypeStruct((B, X_HEADS, X_HEAD_DIM), F32),
        compiler_params=_cparams(("parallel",)),
        name="xattn_sample",
    )(q, k, v)


def _mixer_even_kernel(u_ref, v_ref, z_ref, xbc_ref, dt_ref, ws_ref, bst_ref, lng_ref, lnb_ref, cw_ref, cb_ref,
                       dtb_ref, aexp_ref, dexp_ref, bnorm_ref, tril_ref, expand_ref,
                       yab_ref, ssm_ref, xp_ref, st_ref):
    c = pl.program_id(1)
    L = A_CHUNK

    @pl.when(c == 0)
    def _():
        xp_ref[0:SUBLANES, :] = jnp.zeros((SUBLANES, B_CONV_DIM), F32)
        st_ref[...] = jnp.zeros_like(st_ref)

    row = lax.broadcasted_iota(jnp.int32, (L, L), 0)
    col = lax.broadcasted_iota(jnp.int32, (L, L), 1)
    causal = col <= row

    gu = jax.nn.gelu(u_ref[0])
    vb = _layernorm(jax.nn.gelu(v_ref[0]), lng_ref[...], lnb_ref[...]).astype(BF16)
    for hh in range(A_HEADS):
        sl = slice(hh * LANES, (hh + 1) * LANES)
        w = jnp.where(causal, ws_ref[hh], 0.0).astype(BF16)
        mix = _dot(w, vb[:, sl]) + bst_ref[:, hh:hh + 1]
        yab_ref[0, :, sl] = (gu[:, sl] * mix).astype(BF16)

    x = xbc_ref[0]
    xp_ref[SUBLANES:SUBLANES + L, :] = x
    conv = cb_ref[...] + cw_ref[B_CONV - 1:B_CONV, :] * x
    for k in range(B_CONV - 1):
        conv = conv + cw_ref[k:k + 1, :] * xp_ref[pl.ds(SUBLANES - (B_CONV - 1) + k, L), :]
    xp_ref[0:SUBLANES, :] = x[L - SUBLANES:L, :]
    xa = jax.nn.silu(conv)
    xs = xa[:, :D]
    bm = xa[:, D:D + B_GROUPS * B_STATE]
    cm = xa[:, D + B_GROUPS * B_STATE:]

    tril = tril_ref[...]
    dtf = _softplus(dt_ref[0] + dtb_ref[...])
    dt_x = _sel_right(dtf, expand_ref[...], _split2)
    cs_x = _sel_left(tril, dt_x * aexp_ref[...], _split3)
    ecs_x = jnp.exp(cs_x)
    last_x = cs_x[L - 1:L, :]
    xdt = xs * dt_x
    xdt_b = xdt.astype(BF16)
    xdec_b = (xdt * jnp.exp(last_x - cs_x)).astype(BF16)
    lane = lax.broadcasted_iota(jnp.int32, (L, LANES), 1)
    y_parts = []
    for g in range(B_GROUPS):
        gs = slice(g * B_GROUP_W, (g + 1) * B_GROUP_W)
        bg = bm[:, g * B_STATE:(g + 1) * B_STATE]
        cg = cm[:, g * B_STATE:(g + 1) * B_STATE].astype(BF16)
        gmat = _dot_nt(cg, bg.astype(BF16))
        st = st_ref[g]
        y_off = ecs_x[:, gs] * _dot(cg, st.astype(BF16))
        st_new = st * jnp.exp(last_x[:, gs]) + _dot(bg.T.astype(BF16), xdec_b[:, gs])
        st_ref[g] = st_new
        for pair in range(B_GROUP_W // LANES):
            base = g * B_GROUP_W + pair * LANES
            cs_t = cs_x[:, base:base + LANES].T
            ms = []
            for half in range(2):
                ch = half * B_HEAD_DIM
                diff = cs_x[:, base + ch:base + ch + 1] - cs_t[ch:ch + 1, :]
                ms.append((gmat * jnp.exp(jnp.where(causal, diff, -jnp.inf))).astype(BF16))
            xp2 = xdt_b[:, base:base + LANES]
            rhs = jnp.concatenate([jnp.where(lane < B_HEAD_DIM, xp2, jnp.zeros_like(xp2)),
                                   jnp.where(lane >= B_HEAD_DIM, xp2, jnp.zeros_like(xp2))], axis=0)
            y_parts.append(_dot(jnp.concatenate(ms, axis=1), rhs) + y_off[:, pair * LANES:(pair + 1) * LANES])
    y = jnp.concatenate(y_parts, axis=-1) + dexp_ref[...] * xs
    yb = _group_rms(y * jax.nn.silu(z_ref[0]), bnorm_ref[...], B_GROUP_W)
    yab_ref[0, :, D:] = yb.astype(BF16)

    @pl.when(c == pl.num_programs(1) - 1)
    def _():
        for g in range(B_GROUPS):
            ssm_ref[0, g] = st_ref[g].T


def _even_consts(P):
    head_of_ch = jnp.arange(D) // B_HEAD_DIM
    expand = (jnp.arange(LANES)[:, None] == head_of_ch[None, :]).astype(BF16)
    tril = (jnp.arange(A_CHUNK)[None, :] <= jnp.arange(A_CHUNK)[:, None]).astype(BF16)
    aexp = jnp.repeat(-jnp.exp(P["b_a_log"][0].astype(F32)), B_HEAD_DIM).reshape(1, D)
    dexp = jnp.repeat(P["b_d"][0].astype(F32), B_HEAD_DIM).reshape(1, D)
    dtb = jnp.pad(P["b_dt_bias"][0].astype(F32), (0, LANES - B_HEADS)).reshape(1, LANES)
    return expand, tril, aexp, dexp, dtb


def mixer_even_prompt(proj, P):
    B, T, _ = proj.shape
    L = A_CHUNK
    expand, tril, aexp, dexp, dtb = _even_consts(P)
    row = lambda a: a.reshape(1, -1)
    full = lambda shape: pl.BlockSpec(shape, lambda b, c: (0,) * len(shape))
    return pl.pallas_call(
        _mixer_even_kernel,
        grid=(B, T // L),
        in_specs=[pl.BlockSpec((1, L, D), lambda b, c: (b, c, 0)),
                  pl.BlockSpec((1, L, D), lambda b, c: (b, c, 1)),
                  pl.BlockSpec((1, L, D), lambda b, c: (b, c, 2)),
                  pl.BlockSpec((1, L, B_CONV_DIM), lambda b, c: (b, c, 2)),
                  pl.BlockSpec((1, L, LANES), lambda b, c: (b, c, DT_COL_BLOCK)),
                  full((A_HEADS, L, L)), full((L, A_HEADS)), full((1, D)), full((1, D)),
                  full((B_CONV, B_CONV_DIM)), full((1, B_CONV_DIM)), full((1, LANES)), full((1, D)), full((1, D)),
                  full((1, D)), full((L, L)), full((LANES, D))],
        out_specs=[pl.BlockSpec((1, L, 2 * D), lambda b, c: (b, c, 0)),
                   pl.BlockSpec((1, B_GROUPS, B_GROUP_W, B_STATE), lambda b, c: (b, 0, 0, 0))],
        out_shape=[jax.ShapeDtypeStruct((B, T, 2 * D), BF16),
                   jax.ShapeDtypeStruct((B, B_GROUPS, B_GROUP_W, B_STATE), F32)],
        scratch_shapes=[pltpu.VMEM((SUBLANES + L, B_CONV_DIM), F32),
                        pltpu.VMEM((B_GROUPS, B_STATE, B_GROUP_W), F32)],
        compiler_params=_cparams(("parallel", "arbitrary")),
        name="mixer_even_prompt",
    )(proj, proj, proj, proj, proj, P["a_ws"][0], P["a_bs"][0].T, row(P["a_ln_g"][0]), row(P["a_ln_b"][0]),
      P["b_conv_w"][0], row(P["b_conv_b"][0]), dtb, aexp, dexp, row(P["b_norm"][0]), tril, expand)


def _mixer_even_step_kernel(proj_ref, conv_ref, ssm_ref, ws0_ref, bs0_ref, lng_ref, lnb_ref, cw_ref, cb_ref,
                            dtb_ref, aexp_ref, dexp_ref, bnorm_ref, expand_ref,
                            yab_ref, av_ref, convo_ref, ssmo_ref):
    nb = proj_ref.shape[0]
    u = proj_ref[:, 0:D]
    v = proj_ref[:, D:2 * D]
    z = proj_ref[:, 2 * D:3 * D]
    x = proj_ref[:, 3 * D:3 * D + B_CONV_DIM]
    dt = proj_ref[:, DT_COL_BLOCK * LANES:(DT_COL_BLOCK + 1) * LANES]

    vln = _layernorm(jax.nn.gelu(v), lng_ref[...], lnb_ref[...])
    av_ref[...] = vln
    yab_ref[:, 0:D] = jax.nn.gelu(u) * (ws0_ref[...] * vln + bs0_ref[...])

    conv = cb_ref[...] + cw_ref[B_CONV - 1:B_CONV, :] * x
    for k in range(B_CONV - 1):
        conv = conv + cw_ref[k:k + 1, :] * conv_ref[k]
        if k > 0:
            convo_ref[k - 1] = conv_ref[k]
    convo_ref[B_CONV - 2] = x
    xa = jax.nn.silu(conv)
    xs = xa[:, :D]
    bm = xa[:, D:D + B_GROUPS * B_STATE]
    cm = xa[:, D + B_GROUPS * B_STATE:]
    dtf = _softplus(dt + dtb_ref[...])
    dt_x = _sel_right(dtf, expand_ref[...], _split3)
    dec_x = jnp.exp(dt_x * aexp_ref[...])
    xdt = xs * dt_x
    y_rows = []
    for g in range(B_GROUPS):
        gs = slice(g * B_GROUP_W, (g + 1) * B_GROUP_W)
        dec_t = _rows_to_cols(dec_x[:, gs])
        xdt_t = _rows_to_cols(xdt[:, gs])
        ys = []
        for j in range(nb):
            s_new = ssm_ref[j, g] * dec_t[:, j:j + 1] + xdt_t[:, j:j + 1] * bm[j:j + 1, g * B_STATE:(g + 1) * B_STATE]
            ssmo_ref[j, g] = s_new
            cj = jnp.broadcast_to(cm[j:j + 1, g * B_STATE:(g + 1) * B_STATE], (SUBLANES, B_STATE))
            ys.append(_dot_nt_f32(cj, s_new)[0:1, :])
        y_rows.append(jnp.concatenate(ys, axis=0))
    y = jnp.concatenate(y_rows, axis=-1) + dexp_ref[...] * xs
    yb = _group_rms(y * jax.nn.silu(z), bnorm_ref[...], B_GROUP_W)
    yab_ref[:, D:] = yb


def mixer_even_step(proj, conv0, ssm0, P, nb):
    B = proj.shape[0]
    expand, _, aexp, dexp, dtb = _even_consts(P)
    row = lambda a: a.reshape(1, -1)
    rep = lambda a: jnp.repeat(a.astype(F32), LANES).reshape(1, D)
    full = lambda shape: pl.BlockSpec(shape, lambda i: (0,) * len(shape))
    return pl.pallas_call(
        _mixer_even_step_kernel,
        grid=(B // nb,),
        in_specs=[pl.BlockSpec((nb, proj.shape[1]), lambda i: (i, 0)),
                  pl.BlockSpec((B_CONV - 1, nb, B_CONV_DIM), lambda i: (0, i, 0)),
                  pl.BlockSpec((nb, B_GROUPS, B_GROUP_W, B_STATE), lambda i: (i, 0, 0, 0)),
                  full((1, D)), full((1, D)), full((1, D)), full((1, D)),
                  full((B_CONV, B_CONV_DIM)), full((1, B_CONV_DIM)), full((1, LANES)), full((1, D)), full((1, D)),
                  full((1, D)), full((LANES, D))],
        out_specs=[pl.BlockSpec((nb, 2 * D), lambda i: (i, 0)),
                   pl.BlockSpec((nb, D), lambda i: (i, 0)),
                   pl.BlockSpec((B_CONV - 1, nb, B_CONV_DIM), lambda i: (0, i, 0)),
                   pl.BlockSpec((nb, B_GROUPS, B_GROUP_W, B_STATE), lambda i: (i, 0, 0, 0))],
        out_shape=[jax.ShapeDtypeStruct((B, 2 * D), F32),
                   jax.ShapeDtypeStruct((B, D), F32),
                   jax.ShapeDtypeStruct((B_CONV - 1, B, B_CONV_DIM), F32),
                   jax.ShapeDtypeStruct((B, B_GROUPS, B_GROUP_W, B_STATE), F32)],
        compiler_params=_cparams(("parallel",)),
        name="mixer_even_step",
    )(proj, conv0, ssm0, rep(P["a_ws"][0][:, 0, 0]), rep(P["a_bs"][0][:, 0]), row(P["a_ln_g"][0]), row(P["a_ln_b"][0]),
      P["b_conv_w"][0], row(P["b_conv_b"][0]), dtb, aexp, dexp, row(P["b_norm"][0]), expand)


def _gla_gates(q_raw, f_raw, lb):
    fg = lb + (1.0 - lb) * jax.nn.sigmoid(f_raw)
    return jax.nn.silu(q_raw), fg, 1.0 - fg


def _hgrn_kernel(q_ref, f_ref, i_ref, g_ref, lb_ref, cnorm_ref, tril_ref, o_ref, s_ref, st_ref):
    c = pl.program_id(1)
    L = C_CHUNK
    R = q_ref.shape[1]

    @pl.when(c == 0)
    def _():
        st_ref[...] = jnp.zeros_like(st_ref)

    row = lax.broadcasted_iota(jnp.int32, (R, R), 0)
    col = lax.broadcasted_iota(jnp.int32, (R, R), 1)
    causal = (col <= row) & (row // L == col // L)
    chunk_of_row = lax.broadcasted_iota(jnp.int32, (R, 1), 0) // L
    q, fg, k = _gla_gates(q_ref[0], f_ref[0], lb_ref[...])
    v = i_ref[0]
    bc = _sel_left(tril_ref[...], jnp.log(fg), _split3)
    q_in = (q * jnp.exp(bc)).astype(BF16)
    k_in = (k * jnp.exp(-bc)).astype(BF16)
    vb = v.astype(BF16)
    btots, k_decs = [], []
    for s in range(R // L):
        btot = bc[(s + 1) * L - 1:(s + 1) * L, :]
        btots.append(btot)
        k_decs.append(jnp.where(chunk_of_row == s, k * jnp.exp(btot - bc), 0.0).astype(BF16))
    outs = []
    for hh in range(C_HEADS):
        sl = slice(hh * C_KDIM, (hh + 1) * C_KDIM)
        att = jnp.where(causal, _dot_nt(q_in[:, sl], k_in[:, sl]), 0.0)
        v_t = v[:, sl].T.astype(BF16)
        st = st_ref[hh]
        inter = []
        for s in range(R // L):
            inter.append(_dot_nt(q_in[s * L:(s + 1) * L, sl], st.astype(BF16)))
            st = st * jnp.exp(btots[s][:, sl]) + _dot(v_t, k_decs[s][:, sl])
        st_ref[hh] = st
        outs.append(_dot(att.astype(BF16), vb[:, sl]) + jnp.concatenate(inter, axis=0))
    o = _group_rms(jnp.concatenate(outs, axis=-1), cnorm_ref[...], C_KDIM)
    o_ref[0] = (o * jax.nn.silu(g_ref[0])).astype(BF16)

    @pl.when(c == pl.num_programs(1) - 1)
    def _():
        for hh in range(C_HEADS):
            s_ref[0, hh] = st_ref[hh].T


def hgrn_prompt(proj, lb, cnorm):
    B, T, _ = proj.shape
    L = C_ROWS
    r = jnp.arange(L)
    tril = ((r[None, :] <= r[:, None]) & (r[None, :] // C_CHUNK == r[:, None] // C_CHUNK)).astype(BF16)
    full = lambda shape: pl.BlockSpec(shape, lambda b, c: (0,) * len(shape))
    return pl.pallas_call(
        _hgrn_kernel,
        grid=(B, T // L),
        in_specs=[pl.BlockSpec((1, L, D), lambda b, c: (b, c, 0)),
                  pl.BlockSpec((1, L, D), lambda b, c: (b, c, 1)),
                  pl.BlockSpec((1, L, D), lambda b, c: (b, c, 2)),
                  pl.BlockSpec((1, L, D), lambda b, c: (b, c, 3)),
                  full((1, D)), full((1, D)), full((L, L))],
        out_specs=[pl.BlockSpec((1, L, D), lambda b, c: (b, c, 0)),
                   pl.BlockSpec((1, C_HEADS, C_KDIM, C_KDIM), lambda b, c: (b, 0, 0, 0))],
        out_shape=[jax.ShapeDtypeStruct((B, T, D), BF16),
                   jax.ShapeDtypeStruct((B, C_HEADS, C_KDIM, C_KDIM), F32)],
        scratch_shapes=[pltpu.VMEM((C_HEADS, C_KDIM, C_KDIM), F32)],
        compiler_params=_cparams(("parallel", "arbitrary")),
        name="hgrn_prompt",
    )(proj, proj, proj, proj, lb.reshape(1, D), cnorm.reshape(1, D), tril)


def _hgrn_step_kernel(proj_ref, s_ref, lb_ref, cnorm_ref, o_ref, so_ref):
    nb = proj_ref.shape[0]
    q, fg, k = _gla_gates(proj_ref[:, 0:D], proj_ref[:, D:2 * D], lb_ref[...])
    v = proj_ref[:, 2 * D:3 * D]
    g = proj_ref[:, 3 * D:4 * D]
    outs = []
    for hh in range(C_HEADS):
        sl = slice(hh * C_KDIM, (hh + 1) * C_KDIM)
        fg_t = _rows_to_cols(fg[:, sl])
        k_t = _rows_to_cols(k[:, sl])
        rows = []
        for j in range(nb):
            s_new = s_ref[j, hh] * fg_t[:, j:j + 1] + k_t[:, j:j + 1] * v[j:j + 1, sl]
            so_ref[j, hh] = s_new
            qj = jnp.broadcast_to(q[j:j + 1, sl], (SUBLANES, C_KDIM))
            rows.append(_dot_f32(qj, s_new)[0:1, :])
        outs.append(jnp.concatenate(rows, axis=0))
    o = _group_rms(jnp.concatenate(outs, axis=-1), cnorm_ref[...], C_KDIM)
    o_ref[...] = o * jax.nn.silu(g)


def hgrn_step(proj, s0, lb, cnorm, nb):
    B = proj.shape[0]
    full = lambda shape: pl.BlockSpec(shape, lambda i: (0,) * len(shape))
    return pl.pallas_call(
        _hgrn_step_kernel,
        grid=(B // nb,),
        in_specs=[pl.BlockSpec((nb, 4 * D), lambda i: (i, 0)),
                  pl.BlockSpec((nb, C_HEADS, C_KDIM, C_KDIM), lambda i: (i, 0, 0, 0)),
                  full((1, D)), full((1, D))],
        out_specs=[pl.BlockSpec((nb, D), lambda i: (i, 0)),
                   pl.BlockSpec((nb, C_HEADS, C_KDIM, C_KDIM), lambda i: (i, 0, 0, 0))],
        out_shape=[jax.ShapeDtypeStruct((B, D), F32),
                   jax.ShapeDtypeStruct((B, C_HEADS, C_KDIM, C_KDIM), F32)],
        compiler_params=_cparams(("parallel",)),
        name="hgrn_step",
    )(proj, s0, lb.reshape(1, D), cnorm.reshape(1, D))


TM_PROJ = 256
TM_OUT = 512
TM_ATTN = 512
TM_FFN = 1024
TF_DENSE = 256
TF_MOE = 512
TF_ROUTED = 896
TG_MOE = 512
TM_ROUTE = 512
TM_DISPATCH = 1024
TM_COMBINE = 256
DMA_UNROLL = 8
ROUTED_MIN_TOKENS = 8 * TG_MOE
STEP_NB = 8
XATTN_STEP_NB = 4
TN_STEP = 512
TK_STEP = 512
IN0_STEP_PAD = 5120


def _prep_weights(P):
    W = {}
    W["ev_w_in"] = jnp.pad(P["ev_w_in"][0], ((0, 0), (0, IN0_PAD - IN0))).astype(BF16)
    W["ev_w_out"] = P["ev_w_out"][0].astype(BF16)
    W["od_w_in"] = P["od_w_in"][0].astype(BF16)
    W["od_w_out"] = P["od_w_out"][0].astype(BF16)
    W["xa_wq"] = P["xa_wq"].astype(BF16)
    W["xa_wo"] = P["xa_wo"].astype(BF16)
    W["xa_wkv"] = jnp.concatenate([P["xa_wk"], P["xa_wv"]], axis=-1).astype(BF16)
    W["ffn"] = tuple(P[n].astype(BF16) for n in ("ffn_w_gate", "ffn_w_up", "ffn_w_down"))
    W["moe"] = tuple(P[n][0].astype(BF16) for n in ("moe_w_gate", "moe_w_up", "moe_w_down"))
    W["router"] = jnp.pad(P["moe_router"][0].astype(F32), ((0, 0), (0, LANES - N_EXPERTS)))
    lbp = jax.nn.softmax(P["hgrn_lb_logits"].astype(F32), axis=0)
    W["lower_bounds"] = jnp.cumsum(lbp, axis=0) - lbp[0]
    return W


def _channel_mix(x2, l, P, W, tm):
    if l == 0:
        wg, wu, wd = W["ffn"]
        return ffn(x2, P["norm_ffn_pre"][l], P["norm_ffn_post"][l], W["router"], wg, wu, wd, tm, TF_DENSE, False)
    wg, wu, wd = W["moe"]
    if x2.shape[0] >= ROUTED_MIN_TOKENS:
        return moe_routed(x2, P["norm_ffn_pre"][l], P["norm_ffn_post"][l], W["router"], wg, wu, wd)
    return ffn(x2, P["norm_ffn_pre"][l], P["norm_ffn_post"][l], W["router"], wg, wu, wd, tm, TF_MOE, True)


def _trunk_prompt(x, mem_k, mem_v, P, W):
    B, T, _ = x.shape
    M = B * T
    x2 = x.reshape(M, D)
    proj = norm_matmul(x2, P["norm_mix_pre"][0], W["ev_w_in"], TM_PROJ).reshape(B, T, IN0_PAD)
    yab, ssm = mixer_even_prompt(proj, P)
    conv = proj[:, T - (B_CONV - 1):, 3 * D:3 * D + B_CONV_DIM]
    x2 = matmul_norm_res(yab.reshape(M, 2 * D), W["ev_w_out"], P["norm_mix_post"][0], x2, TM_OUT)
    x2 = xattn_prompt(x2.reshape(B, T, D), mem_k[0], mem_v[0], W["xa_wq"][0], W["xa_wo"][0],
                      P["norm_x_pre"][0], P["norm_x_post"][0], TM_ATTN).reshape(M, D)
    x2 = _channel_mix(x2, 0, P, W, TM_FFN)
    proj = norm_matmul(x2, P["norm_mix_pre"][1], W["od_w_in"], TM_PROJ).reshape(B, T, 4 * D)
    o, hgrn = hgrn_prompt(proj, W["lower_bounds"][1], P["c_norm"][0])
    x2 = matmul_norm_res(o.reshape(M, D), W["od_w_out"], P["norm_mix_post"][1], x2, TM_OUT)
    x2 = xattn_prompt(x2.reshape(B, T, D), mem_k[1], mem_v[1], W["xa_wq"][1], W["xa_wo"][1],
                      P["norm_x_pre"][1], P["norm_x_post"][1], TM_ATTN).reshape(M, D)
    x2 = _channel_mix(x2, 1, P, W, TM_FFN)
    return x2.reshape(B, T, D), conv, ssm, hgrn


def _xattn_step(x2, l, mem_k, mem_v, P):
    B = x2.shape[0]
    q = norm_matmul(x2, P["norm_x_pre"][l], P["xa_wq"][l], B, TN_STEP)
    o = xattn_sample_core(q.reshape(B, X_HEADS, X_HEAD_DIM), mem_k, mem_v, l, XATTN_STEP_NB).reshape(B, D)
    return matmul_norm_res(o, P["xa_wo"][l], P["norm_x_post"][l], x2, B)


def _trunk_step(x, mem_k, mem_v, conv0, ssm0, hgrn0, P, W):
    B = x.shape[0]
    x2 = x.reshape(B, D)
    w_in = jnp.pad(P["ev_w_in"][0], ((0, 0), (0, IN0_STEP_PAD - IN0)))
    proj = norm_matmul(x2, P["norm_mix_pre"][0], w_in, B, TN_STEP)
    yab, av, conv, ssm = mixer_even_step(proj, jnp.swapaxes(conv0, 0, 1), ssm0, P, STEP_NB)
    x2 = matmul_norm_res(yab, P["ev_w_out"][0], P["norm_mix_post"][0], x2, B, TK_STEP)
    x2 = _xattn_step(x2, 0, mem_k, mem_v, P)
    x2 = ffn(x2, P["norm_ffn_pre"][0], P["norm_ffn_post"][0], W["router"],
             P["ffn_w_gate"], P["ffn_w_up"], P["ffn_w_down"], B, TF_DENSE, False)
    proj = norm_matmul(x2, P["norm_mix_pre"][1], P["od_w_in"][0], B, TN_STEP)
    o, hgrn = hgrn_step(proj, hgrn0, W["lower_bounds"][1], P["c_norm"][0], STEP_NB)
    x2 = matmul_norm_res(o, P["od_w_out"][0], P["norm_mix_post"][1], x2, B)
    x2 = _xattn_step(x2, 1, mem_k, mem_v, P)
    x2 = _channel_mix(x2, 1, P, W, B)
    return x2.reshape(B, 1, D), jnp.swapaxes(conv, 0, 1), ssm, hgrn, av


def kernel(x_prompt, x_sample, mem_prompt, cache_mem_k, cache_mem_v, state_conv, state_ssm, state_hgrn,
           norm_mix_pre, norm_mix_post, norm_x_pre, norm_x_post, norm_ffn_pre, norm_ffn_post, norm_mem,
           xa_wq, xa_wk, xa_wv, xa_wo,
           ev_w_in, a_ws, a_bs, a_ln_g, a_ln_b, b_conv_w, b_conv_b, b_dt_bias, b_a_log, b_d, b_norm, ev_w_out,
           ffn_w_gate, ffn_w_up, ffn_w_down,
           od_w_in, hgrn_lb_logits, c_norm, od_w_out,
           moe_router, moe_w_gate, moe_w_up, moe_w_down):
    P = dict(norm_mix_pre=norm_mix_pre, norm_mix_post=norm_mix_post, norm_x_pre=norm_x_pre, norm_x_post=norm_x_post,
             norm_ffn_pre=norm_ffn_pre, norm_ffn_post=norm_ffn_post, xa_wq=xa_wq, xa_wk=xa_wk, xa_wv=xa_wv,
             xa_wo=xa_wo, ev_w_in=ev_w_in, a_ws=a_ws, a_bs=a_bs, a_ln_g=a_ln_g, a_ln_b=a_ln_b, b_conv_w=b_conv_w,
             b_conv_b=b_conv_b, b_dt_bias=b_dt_bias, b_a_log=b_a_log, b_d=b_d, b_norm=b_norm, ev_w_out=ev_w_out,
             ffn_w_gate=ffn_w_gate, ffn_w_up=ffn_w_up, ffn_w_down=ffn_w_down, od_w_in=od_w_in,
             hgrn_lb_logits=hgrn_lb_logits, c_norm=c_norm, od_w_out=od_w_out, moe_router=moe_router,
             moe_w_gate=moe_w_gate, moe_w_up=moe_w_up, moe_w_down=moe_w_down)
    W = _prep_weights(P)
    depth = norm_mem.shape[0]
    bp, T, _ = x_prompt.shape
    bs = x_sample.shape[0]

    mem2 = mem_prompt.reshape(bp * MEM_LEN, D)
    kv = [norm_matmul(mem2, norm_mem[l], W["xa_wkv"][l], TM_PROJ) for l in range(depth)]
    mem_k_p = jnp.stack([t[:, :D] for t in kv]).reshape(depth, bp, MEM_LEN, D)
    mem_v_p = jnp.stack([t[:, D:] for t in kv]).reshape(depth, bp, MEM_LEN, D)
    y_p, conv_p, ssm_p, hgrn_p = _trunk_prompt(x_prompt, mem_k_p, mem_v_p, P, W)

    y_s, conv_s, ssm_s, hgrn_s, av_s = _trunk_step(
        x_sample, cache_mem_k, cache_mem_v,
        state_conv[0], state_ssm[0].reshape(bs, B_GROUPS, B_GROUP_W, B_STATE), state_hgrn[0], P, W)

    kv_shape = (depth, bp, MEM_LEN, X_HEADS, X_HEAD_DIM)
    ssm_shape = (B_GROUPS, B_GROUP_W // B_HEAD_DIM, B_HEAD_DIM, B_STATE)
    return (y_p, y_s, mem_k_p.reshape(kv_shape), mem_v_p.reshape(kv_shape),
            conv_p[None], ssm_p.reshape((1, bp) + ssm_shape), hgrn_p[None],
            conv_s[None], ssm_s.reshape((1, bs) + ssm_shape), hgrn_s[None], av_s.reshape(1, bs, 1, D))
```

```python
import functools

import jax
import jax.numpy as jnp
from jax import lax
from jax.experimental import pallas as pl
from jax.experimental.pallas import tpu as pltpu

F32 = jnp.float32
BF16 = jnp.bfloat16
EPS = 1e-6

D = 1024
LANES = 128
SUBLANES = 8
A_HEADS = 8
A_CHUNK = 128
B_HEADS = 16
B_HEAD_DIM = 64
B_GROUPS = 2
B_GROUP_W = 512
B_STATE = 128
B_CONV = 4
B_CONV_DIM = 1536
IN0 = 4624
IN0_PAD = 4736
DT_COL_BLOCK = 36
C_HEADS = 8
C_KDIM = 128
C_CHUNK = 64
C_ROWS = 128
X_HEADS = 4
X_HEAD_DIM = 256
MEM_LEN = 256
N_EXPERTS = 8

VMEM_LIMIT = 56 * 1024 * 1024


def _cparams(sem):
    return pltpu.CompilerParams(dimension_semantics=sem, vmem_limit_bytes=VMEM_LIMIT)


def _dot(a, b):
    return jnp.dot(a, b, preferred_element_type=F32)


def _dot_nt(a, b):
    return lax.dot_general(a, b, (((1,), (1,)), ((), ())), preferred_element_type=F32)


def _dot_f32(a, b):
    return jnp.dot(a, b, precision=lax.Precision.HIGHEST, preferred_element_type=F32)


def _dot_nt_f32(a, b):
    return lax.dot_general(a, b, (((1,), (1,)), ((), ())), precision=lax.Precision.HIGHEST,
                           preferred_element_type=F32)


def _mm(a, w):
    if w.dtype == F32:
        return _dot_f32(a.astype(F32), w)
    return _dot(a.astype(BF16), w)


def _rms(x, g):
    return x * lax.rsqrt(jnp.mean(x * x, axis=-1, keepdims=True) + EPS) * g


def _split2(x):
    hi = x.astype(BF16)
    lo = (x - hi.astype(F32)).astype(BF16)
    return hi, lo


def _split3(x):
    hi = x.astype(BF16)
    r = x - hi.astype(F32)
    mid = r.astype(BF16)
    lo = (r - mid.astype(F32)).astype(BF16)
    return hi, mid, lo


def _sel_left(m, x, parts):
    out = None
    for p in parts(x):
        t = _dot(m, p)
        out = t if out is None else out + t
    return out


def _sel_right(x, m, parts):
    out = None
    for p in parts(x):
        t = _dot(p, m)
        out = t if out is None else out + t
    return out


def _rows_to_cols(x):
    n, w = x.shape
    if n < LANES:
        x = jnp.concatenate([x, jnp.zeros((LANES - n, w), x.dtype)], axis=0)
    return x.T


def _softplus(x):
    return jnp.maximum(x, 0.0) + jnp.log1p(jnp.exp(-jnp.abs(x)))


def _layernorm(x, g, b):
    xc = x - jnp.mean(x, axis=-1, keepdims=True)
    return xc * lax.rsqrt(jnp.mean(xc * xc, axis=-1, keepdims=True) + EPS) * g + b


def _group_rms(x, g, width):
    parts = []
    for s in range(0, x.shape[-1], width):
        t = x[:, s:s + width]
        parts.append(t * lax.rsqrt(jnp.mean(t * t, axis=-1, keepdims=True) + EPS))
    return jnp.concatenate(parts, axis=-1) * g


def _norm_matmul_kernel(x_ref, g_ref, w_ref, o_ref):
    o_ref[...] = _mm(_rms(x_ref[...], g_ref[...]), w_ref[...])


def norm_matmul(x, g, w, tm, tn=None):
    M, K = x.shape
    N = w.shape[1]
    tn = N if tn is None else tn
    return pl.pallas_call(
        _norm_matmul_kernel,
        grid=(M // tm, N // tn),
        in_specs=[pl.BlockSpec((tm, K), lambda i, j: (i, 0)),
                  pl.BlockSpec((1, K), lambda i, j: (0, 0)),
                  pl.BlockSpec((K, tn), lambda i, j: (0, j))],
        out_specs=pl.BlockSpec((tm, tn), lambda i, j: (i, j)),
        out_shape=jax.ShapeDtypeStruct((M, N), F32),
        compiler_params=_cparams(("parallel", "parallel")),
        name="norm_matmul",
    )(x, g.reshape(1, K), w)


def _mem_kv_kernel(x_ref, g_ref, wk_ref, wv_ref, k_ref, v_ref):
    h = _rms(x_ref[...], g_ref[0]).astype(BF16)
    k_ref[0] = _dot(h, wk_ref[0])
    v_ref[0] = _dot(h, wv_ref[0])


def mem_kv(mem, g, wk, wv, tm):
    M, K = mem.shape
    depth = g.shape[0]
    wspec = pl.BlockSpec((1, K, K), lambda l, i: (l, 0, 0))
    ospec = pl.BlockSpec((1, tm, K), lambda l, i: (l, i, 0))
    return pl.pallas_call(
        _mem_kv_kernel,
        grid=(depth, M // tm),
        in_specs=[pl.BlockSpec((tm, K), lambda l, i: (i, 0)),
                  pl.BlockSpec((1, 1, K), lambda l, i: (l, 0, 0)), wspec, wspec],
        out_specs=[ospec, ospec],
        out_shape=[jax.ShapeDtypeStruct((depth, M, K), F32)] * 2,
        compiler_params=_cparams(("parallel", "parallel")),
        name="mem_kv",
    )(mem, g.reshape(depth, 1, K), wk, wv)


def _matmul_norm_res_kernel(a_ref, w_ref, g_ref, r_ref, o_ref, acc_ref):
    k = pl.program_id(1)

    @pl.when(k == 0)
    def _():
        acc_ref[...] = jnp.zeros_like(acc_ref)

    acc_ref[...] += _mm(a_ref[...], w_ref[...])

    @pl.when(k == pl.num_programs(1) - 1)
    def _():
        o_ref[...] = r_ref[...] + _rms(acc_ref[...], g_ref[...])


def matmul_norm_res(a, w, g, res, tm, tk=None):
    M, K = a.shape
    N = w.shape[1]
    tk = K if tk is None else tk
    return pl.pallas_call(
        _matmul_norm_res_kernel,
        grid=(M // tm, K // tk),
        in_specs=[pl.BlockSpec((tm, tk), lambda i, k: (i, k)),
                  pl.BlockSpec((tk, N), lambda i, k: (k, 0)),
                  pl.BlockSpec((1, N), lambda i, k: (0, 0)),
                  pl.BlockSpec((tm, N), lambda i, k: (i, 0))],
        out_specs=pl.BlockSpec((tm, N), lambda i, k: (i, 0)),
        out_shape=jax.ShapeDtypeStruct((M, N), F32),
        scratch_shapes=[pltpu.VMEM((tm, N), F32)],
        compiler_params=_cparams(("parallel", "arbitrary")),
        name="matmul_norm_res",
    )(a, w, g.reshape(1, N), res)


def _top2(h, router, three_pass=False):
    if three_pass:
        hh, hl = _split2(h)
        rh, rl = _split2(router)
        lg = _dot(hh, rh) + _dot(hh, rl) + _dot(hl, rh)
    else:
        lg = _dot_f32(h, router)
    lane = lax.broadcasted_iota(jnp.int32, lg.shape, 1)
    lg = jnp.where(lane < N_EXPERTS, lg, -jnp.inf)
    m1 = jnp.max(lg, axis=-1, keepdims=True)
    i1 = jnp.min(jnp.where(lg == m1, lane, LANES), axis=-1, keepdims=True)
    lg2 = jnp.where(lane == i1, -jnp.inf, lg)
    m2 = jnp.max(lg2, axis=-1, keepdims=True)
    i2 = jnp.min(jnp.where(lg2 == m2, lane, LANES), axis=-1, keepdims=True)
    e2 = jnp.exp(m2 - m1)
    den = 1.0 + e2
    return lane, i1, i2, 1.0 / den, e2 / den


def _top2_gates(h, router):
    lane, i1, i2, w1, w2 = _top2(h, router)
    return jnp.where(lane == i1, w1, 0.0) + jnp.where(lane == i2, w2, 0.0)


def _ffn_kernel(x_ref, gpre_ref, gpost_ref, router_ref, wg_ref, wu_ref, wd_ref, o_ref,
                h_ref, acc_ref, gates_ref, *, moe):
    e = pl.program_id(1)
    f = pl.program_id(2)

    @pl.when((e == 0) & (f == 0))
    def _():
        h = _rms(x_ref[...], gpre_ref[...])
        h_ref[...] = h.astype(h_ref.dtype)
        acc_ref[...] = jnp.zeros_like(acc_ref)
        if moe:
            gates_ref[...] = _top2_gates(h, router_ref[...])
        else:
            gates_ref[...] = jnp.zeros_like(gates_ref)

    hb = h_ref[...]
    act = jax.nn.silu(_mm(hb, wg_ref[0])) * _mm(hb, wu_ref[0])
    if moe:
        gates = gates_ref[...]
        lane = lax.broadcasted_iota(jnp.int32, gates.shape, 1)
        act = act * jnp.sum(jnp.where(lane == e, gates, 0.0), axis=-1, keepdims=True)
    acc_ref[...] += _mm(act, wd_ref[0])

    @pl.when((e == pl.num_programs(1) - 1) & (f == pl.num_programs(2) - 1))
    def _():
        o_ref[...] = x_ref[...] + _rms(acc_ref[...], gpost_ref[...])


def ffn(x, gpre, gpost, router, wg, wu, wd, tm, tf, moe):
    M, K = x.shape
    E, _, F = wg.shape
    return pl.pallas_call(
        functools.partial(_ffn_kernel, moe=moe),
        grid=(M // tm, E, F // tf),
        in_specs=[pl.BlockSpec((tm, K), lambda i, e, f: (i, 0)),
                  pl.BlockSpec((1, K), lambda i, e, f: (0, 0)),
                  pl.BlockSpec((1, K), lambda i, e, f: (0, 0)),
                  pl.BlockSpec((K, LANES), lambda i, e, f: (0, 0)),
                  pl.BlockSpec((1, K, tf), lambda i, e, f: (e, 0, f)),
                  pl.BlockSpec((1, K, tf), lambda i, e, f: (e, 0, f)),
                  pl.BlockSpec((1, tf, K), lambda i, e, f: (e, f, 0))],
        out_specs=pl.BlockSpec((tm, K), lambda i, e, f: (i, 0)),
        out_shape=jax.ShapeDtypeStruct((M, K), F32),
        scratch_shapes=[pltpu.VMEM((tm, K), wg.dtype), pltpu.VMEM((tm, K), F32), pltpu.VMEM((tm, LANES), F32)],
        compiler_params=_cparams(("parallel", "arbitrary", "arbitrary")),
        name="moe_ffn" if moe else "dense_ffn",
    )(x, gpre.reshape(1, K), gpost.reshape(1, K), router, wg, wu, wd)


INFO_E1, INFO_E2, INFO_R1, INFO_R2, INFO_W1, INFO_W2 = range(6)


def _route_kernel(x_ref, gpre_ref, router_ref, tril_ref, info_ref, cnt_ref, carry_ref):
    i = pl.program_id(0)

    @pl.when(i == 0)
    def _():
        carry_ref[...] = jnp.zeros_like(carry_ref)

    lane, i1, i2, w1, w2 = _top2(_rms(x_ref[...], gpre_ref[...]), router_ref[...], three_pass=True)
    sel =jnp.where((lane == i1) | (lane == i2), 1.0, 0.0)
    incl = _dot(tril_ref[...], sel.astype(BF16))
    rank = incl - sel + carry_ref[...]
    r1 = jnp.sum(jnp.where(lane == i1, rank, 0.0), axis=-1, keepdims=True)
    r2 = jnp.sum(jnp.where(lane == i2, rank, 0.0), axis=-1, keepdims=True)
    info = jnp.zeros(sel.shape, F32)
    for idx, val in ((INFO_E1, i1.astype(F32)), (INFO_E2, i2.astype(F32)), (INFO_R1, r1), (INFO_R2, r2),
                     (INFO_W1, w1), (INFO_W2, w2)):
        info = jnp.where(lane == idx, val, info)
    info_ref[...] = info
    carry_ref[...] += incl[incl.shape[0] - 1:, :]
    cnt_ref[...] = carry_ref[...]


def moe_route(x, gpre, router, tm):
    M, K = x.shape
    tril = (jnp.arange(tm)[None, :] <= jnp.arange(tm)[:, None]).astype(BF16)
    return pl.pallas_call(
        _route_kernel,
        grid=(M // tm,),
        in_specs=[pl.BlockSpec((tm, K), lambda i: (i, 0)),
                  pl.BlockSpec((1, K), lambda i: (0, 0)),
                  pl.BlockSpec((K, LANES), lambda i: (0, 0)),
                  pl.BlockSpec((tm, tm), lambda i: (0, 0))],
        out_specs=[pl.BlockSpec((tm, LANES), lambda i: (i, 0)),
                   pl.BlockSpec((1, LANES), lambda i: (0, 0))],
        out_shape=[jax.ShapeDtypeStruct((M, LANES), F32), jax.ShapeDtypeStruct((1, LANES), F32)],
        scratch_shapes=[pltpu.VMEM((1, LANES), F32)],
        compiler_params=_cparams(("arbitrary",)),
        name="moe_route",
    )(x, gpre.reshape(1, K), router, tril)


def _dispatch_kernel(dest_ref, x_ref, zeros_ref, xs_ref, sem):
    del zeros_ref
    tm = x_ref.shape[0]
    base = pl.program_id(0) * tm

    def issue(r, carry):
        for k in range(2):
            d = dest_ref[(base + r) * 2 + k]
            pltpu.make_async_copy(x_ref.at[pl.ds(r, 1), :], xs_ref.at[pl.ds(d, 1), :], sem).start()
        return carry

    lax.fori_loop(0, tm, issue, 0, unroll=DMA_UNROLL)
    for k in range(2):
        pltpu.make_async_copy(x_ref, xs_ref.at[pl.ds(0, tm), :], sem).wait()


def moe_dispatch(x, dest, n_slots, tm):
    M, K = x.shape
    return pl.pallas_call(
        _dispatch_kernel,
        grid_spec=pltpu.PrefetchScalarGridSpec(
            num_scalar_prefetch=1,
            grid=(M // tm,),
            in_specs=[pl.BlockSpec((tm, K), lambda i, dest: (i, 0)),
                      pl.BlockSpec(memory_space=pl.ANY)],
            out_specs=pl.BlockSpec(memory_space=pl.ANY),
            scratch_shapes=[pltpu.SemaphoreType.DMA(())]),
        out_shape=jax.ShapeDtypeStruct((n_slots, K), F32),
        input_output_aliases={2: 0},
        compiler_params=_cparams(("arbitrary",)),
        name="moe_dispatch",
    )(dest, x, jnp.zeros((n_slots, K), F32))


def _experts_kernel(te_ref, tv_ref, xs_ref, gpre_ref, wg_ref, wu_ref, wd_ref, o_ref, h_ref, acc_ref):
    del te_ref
    i = pl.program_id(0)
    f = pl.program_id(1)
    last = pl.num_programs(1) - 1

    @pl.when(tv_ref[i] == 1)
    def _():
        @pl.when(f == 0)
        def _():
            h_ref[...] = _rms(xs_ref[...], gpre_ref[...]).astype(BF16)
            acc_ref[...] = jnp.zeros_like(acc_ref)

        hb = h_ref[...]
        act = jax.nn.silu(_dot(hb, wg_ref[0])) * _dot(hb, wu_ref[0])
        acc_ref[...] += _dot(act.astype(BF16), wd_ref[0])

        @pl.when(f == last)
        def _():
            o_ref[...] = acc_ref[...]

    @pl.when((tv_ref[i] == 0) & (f == last))
    def _():
        o_ref[...] = jnp.zeros_like(o_ref)


def moe_experts(xs, gpre, tile_expert, tile_valid, wg, wu, wd, tg, tf):
    S, K = xs.shape
    F = wg.shape[2]
    nf = F // tf
    fidx = lambda i, f, te, tv: jnp.where(tv[i] == 1, f, nf - 1)
    return pl.pallas_call(
        _experts_kernel,
        grid_spec=pltpu.PrefetchScalarGridSpec(
            num_scalar_prefetch=2,
            grid=(S // tg, nf),
            in_specs=[pl.BlockSpec((tg, K), lambda i, f, te, tv: (i, 0)),
                      pl.BlockSpec((1, K), lambda i, f, te, tv: (0, 0)),
                      pl.BlockSpec((1, K, tf), lambda i, f, te, tv: (te[i], 0, fidx(i, f, te, tv))),
                      pl.BlockSpec((1, K, tf), lambda i, f, te, tv: (te[i], 0, fidx(i, f, te, tv))),
                      pl.BlockSpec((1, tf, K), lambda i, f, te, tv: (te[i], fidx(i, f, te, tv), 0))],
            out_specs=pl.BlockSpec((tg, K), lambda i, f, te, tv: (i, 0)),
            scratch_shapes=[pltpu.VMEM((tg, K), BF16), pltpu.VMEM((tg, K), F32)]),
        out_shape=jax.ShapeDtypeStruct((S, K), F32),
        compiler_params=_cparams(("parallel", "arbitrary")),
        name="moe_experts",
    )(tile_expert, tile_valid, xs, gpre.reshape(1, K), wg, wu, wd)


def _combine_kernel(dest_ref, x_ref, info_ref, gpost_ref, ys_ref, o_ref, buf_ref, sem):
    tm = x_ref.shape[0]
    i = pl.program_id(0)

    def gather(tile, slot):
        def issue(r, carry):
            for k in range(2):
                d = dest_ref[(tile * tm + r) * 2 + k]
                pltpu.make_async_copy(ys_ref.at[pl.ds(d, 1), :], buf_ref.at[slot, k, pl.ds(r, 1), :],
                                      sem.at[slot]).start()
            return carry

        lax.fori_loop(0, tm, issue, 0, unroll=DMA_UNROLL)

    @pl.when(i == 0)
    def _():
        gather(0, 0)

    @pl.when(i + 1 < pl.num_programs(0))
    def _():
        gather(i + 1, (i + 1) % 2)

    slot = i % 2
    for k in range(2):
        pltpu.make_async_copy(ys_ref.at[pl.ds(0, tm), :], buf_ref.at[slot, k], sem.at[slot]).wait()
    info = info_ref[...]
    y = info[:, INFO_W1:INFO_W1 + 1] * buf_ref[slot, 0] + info[:, INFO_W2:INFO_W2 + 1] * buf_ref[slot, 1]
    o_ref[...] = x_ref[...] + _rms(y, gpost_ref[...])


def moe_combine(x, info, gpost, ys, dest, tm):
    M, K = x.shape
    return pl.pallas_call(
        _combine_kernel,
        grid_spec=pltpu.PrefetchScalarGridSpec(
            num_scalar_prefetch=1,
            grid=(M // tm,),
            in_specs=[pl.BlockSpec((tm, K), lambda i, dest: (i, 0)),
                      pl.BlockSpec((tm, LANES), lambda i, dest: (i, 0)),
                      pl.BlockSpec((1, K), lambda i, dest: (0, 0)),
                      pl.BlockSpec(memory_space=pl.ANY)],
            out_specs=pl.BlockSpec((tm, K), lambda i, dest: (i, 0)),
            scratch_shapes=[pltpu.VMEM((2, 2, tm, K), F32), pltpu.SemaphoreType.DMA((2,))]),
        out_shape=jax.ShapeDtypeStruct((M, K), F32),
        compiler_params=_cparams(("arbitrary",)),
        name="moe_combine",
    )(dest, x, info, gpost.reshape(1, K), ys)


def moe_routed(x, gpre, gpost, router, wg, wu, wd):
    M, K = x.shape
    tg = TG_MOE
    n_slots = 2 * M + N_EXPERTS * tg
    info, cnt = moe_route(x, gpre, router, TM_ROUTE)
    cnt = cnt[0, :N_EXPERTS].astype(jnp.int32)
    padded = (cnt + tg - 1) // tg * tg
    ends = jnp.cumsum(padded)
    offs = ends - padded
    ids = info[:, INFO_E1:INFO_E2 + 1].astype(jnp.int32)
    ranks = info[:, INFO_R1:INFO_R2 + 1].astype(jnp.int32)
    dest = (offs[ids] + ranks).reshape(2 * M)
    tile_start = jnp.arange(n_slots // tg, dtype=jnp.int32) * tg
    tile_valid = (tile_start < ends[-1]).astype(jnp.int32)
    tile_expert = jnp.minimum(jnp.sum((tile_start[:, None] >= ends[None, :]).astype(jnp.int32), axis=1),
                              N_EXPERTS - 1)
    tile_expert = jnp.where(tile_valid == 1, tile_expert, tile_expert[jnp.maximum(ends[-1] // tg - 1, 0)])
    xs = moe_dispatch(x, dest, n_slots, TM_DISPATCH)
    ys = moe_experts(xs, gpre, tile_expert, tile_valid, wg, wu, wd, tg, TF_ROUTED)
    return moe_combine(x, info, gpost, ys, dest, TM_COMBINE)


def _xattn_kernel(x_ref, k_ref, v_ref, wq_ref, wo_ref, gpre_ref, gpost_ref, o_ref):
    x = x_ref[0]
    h = _rms(x, gpre_ref[...]).astype(BF16)
    q = _dot(h, wq_ref[...])
    k = k_ref[0, 0].astype(BF16)
    v = v_ref[0, 0].astype(BF16)
    outs = []
    for hd in range(X_HEADS):
        sl = slice(hd * X_HEAD_DIM, (hd + 1) * X_HEAD_DIM)
        s = _dot_nt(q[:, sl].astype(BF16), k[:, sl]) * (X_HEAD_DIM ** -0.5)
        e = jnp.exp(s - jnp.max(s, axis=-1, keepdims=True))
        p = e / jnp.sum(e, axis=-1, keepdims=True)
        outs.append(_dot(p.astype(BF16), v[:, sl]))
    o = jnp.concatenate(outs, axis=-1).astype(BF16)
    o_ref[0] = x + _rms(_dot(o, wo_ref[...]), gpost_ref[...])


def xattn_prompt(x, k, v, l, wq, wo, gpre, gpost, tm):
    B, T, K = x.shape
    return pl.pallas_call(
        _xattn_kernel,
        grid=(B, T // tm),
        in_specs=[pl.BlockSpec((1, tm, K), lambda b, t: (b, t, 0)),
                  pl.BlockSpec((1, 1, MEM_LEN, K), lambda b, t: (l, b, 0, 0)),
                  pl.BlockSpec((1, 1, MEM_LEN, K), lambda b, t: (l, b, 0, 0)),
                  pl.BlockSpec((K, K), lambda b, t: (0, 0)),
                  pl.BlockSpec((K, K), lambda b, t: (0, 0)),
                  pl.BlockSpec((1, K), lambda b, t: (0, 0)),
                  pl.BlockSpec((1, K), lambda b, t: (0, 0))],
        out_specs=pl.BlockSpec((1, tm, K), lambda b, t: (b, t, 0)),
        out_shape=jax.ShapeDtypeStruct((B, T, K), F32),
        compiler_params=_cparams(("parallel", "parallel")),
        name="xattn_prompt",
    )(x, k, v, wq, wo, gpre.reshape(1, K), gpost.reshape(1, K))


def _xattn_sample_kernel(q_ref, k_ref, v_ref, o_ref):
    for j in range(q_ref.shape[0]):
        q = q_ref[j]
        k = k_ref[0, j]
        v = v_ref[0, j]
        s = jnp.sum(k * q[None], axis=-1, keepdims=True) * (X_HEAD_DIM ** -0.5)
        e = jnp.exp(s - jnp.max(s, axis=0, keepdims=True))
        p = e / jnp.sum(e, axis=0, keepdims=True)
        o_ref[j] = jnp.sum(p * v, axis=0)


def xattn_sample_core(q, k, v, l, nb):
    B = q.shape[0]
    kv_spec = pl.BlockSpec((1, nb, MEM_LEN, X_HEADS, X_HEAD_DIM), lambda b: (l, b, 0, 0, 0))
    return pl.pallas_call(
        _xattn_sample_kernel,
        grid=(B // nb,),
        in_specs=[pl.BlockSpec((nb, X_HEADS, X_HEAD_DIM), lambda b: (b, 0, 0)), kv_spec, kv_spec],
        out_specs=pl.BlockSpec((nb, X_HEADS, X_HEAD_DIM), lambda b: (b, 0, 0)),
        out_shape=jax.ShapeDtypeStruct((B, X_HEADS, X_HEAD_DIM), F32),
        compiler_params=_cparams(("parallel",)),
        name="xattn_sample",
    )(q, k, v)


def _ev_proj_kernel(x_ref, g_ref, w_ref, lng_ref, lnb_ref, cw_ref, cb_ref,
                    gu_ref, vln_ref, sz_ref, xa_ref, dt_ref, tail_ref, xp_ref):
    t = pl.program_id(1)
    tm = x_ref.shape[1]

    @pl.when(t == 0)
    def _():
        xp_ref[0:SUBLANES, :] = jnp.zeros((SUBLANES, B_CONV_DIM), F32)

    h = _rms(x_ref[0], g_ref[...]).astype(BF16)
    gu_ref[0] = jax.nn.gelu(_dot(h, w_ref[:, 0:D]))
    vln_ref[0] = _layernorm(jax.nn.gelu(_dot(h, w_ref[:, D:2 * D])), lng_ref[...], lnb_ref[...]).astype(BF16)
    sz_ref[0] = jax.nn.silu(_dot(h, w_ref[:, 2 * D:3 * D]))
    dt_ref[0] = _dot(h, w_ref[:, DT_COL_BLOCK * LANES:(DT_COL_BLOCK + 1) * LANES])
    x = _dot(h, w_ref[:, 3 * D:3 * D + B_CONV_DIM])
    xp_ref[SUBLANES:SUBLANES + tm, :] = x
    conv = cb_ref[...] + cw_ref[B_CONV - 1:B_CONV, :] * x
    for k in range(B_CONV - 1):
        conv = conv + cw_ref[k:k + 1, :] * xp_ref[pl.ds(SUBLANES - (B_CONV - 1) + k, tm), :]
    xp_ref[0:SUBLANES, :] = x[tm - SUBLANES:tm, :]
    tail_ref[0] = x[tm - SUBLANES:tm, :]
    xa_ref[0] = jax.nn.silu(conv)


def ev_proj(x, g, w, P, tm):
    B, T, K = x.shape
    row = lambda a: a.reshape(1, -1)
    full = lambda shape: pl.BlockSpec(shape, lambda b, t: (0,) * len(shape))
    tile = lambda n: pl.BlockSpec((1, tm, n), lambda b, t: (b, t, 0))
    return pl.pallas_call(
        _ev_proj_kernel,
        grid=(B, T // tm),
        in_specs=[tile(K), full((1, K)), full((K, IN0_PAD)), full((1, D)), full((1, D)),
                  full((B_CONV, B_CONV_DIM)), full((1, B_CONV_DIM))],
        out_specs=[tile(D), tile(D), tile(D), tile(B_CONV_DIM), tile(LANES),
                   pl.BlockSpec((1, SUBLANES, B_CONV_DIM), lambda b, t: (b, 0, 0))],
        out_shape=[jax.ShapeDtypeStruct((B, T, D), F32), jax.ShapeDtypeStruct((B, T, D), BF16),
                   jax.ShapeDtypeStruct((B, T, D), F32), jax.ShapeDtypeStruct((B, T, B_CONV_DIM), F32),
                   jax.ShapeDtypeStruct((B, T, LANES), F32), jax.ShapeDtypeStruct((B, SUBLANES, B_CONV_DIM), F32)],
        scratch_shapes=[pltpu.VMEM((SUBLANES + tm, B_CONV_DIM), F32)],
        compiler_params=_cparams(("parallel", "arbitrary")),
        name="ev_proj",
    )(x, row(g), w, row(P["a_ln_g"][0]), row(P["a_ln_b"][0]), P["b_conv_w"][0], row(P["b_conv_b"][0]))


def _mixer_even_kernel(gu_ref, vln_ref, sz_ref, xa_ref, dt_ref, ws_ref, bst_ref,
                       dtb_ref, anar_ref, dexp_ref, bnorm_ref, tril_ref, expand_ref,
                       yab_ref, ssm_ref, st_ref):
    c = pl.program_id(1)
    L = A_CHUNK

    @pl.when(c == 0)
    def _():
        st_ref[...] = jnp.zeros_like(st_ref)

    row = lax.broadcasted_iota(jnp.int32, (L, L), 0)
    col = lax.broadcasted_iota(jnp.int32, (L, L), 1)
    causal = col <= row

    gu = gu_ref[0]
    vb = vln_ref[0]
    for hh in range(A_HEADS):
        sl = slice(hh * LANES, (hh + 1) * LANES)
        w = jnp.where(causal, ws_ref[hh], 0.0).astype(BF16)
        mix = _dot(w, vb[:, sl]) + bst_ref[:, hh:hh + 1]
        yab_ref[0, :, sl] = (gu[:, sl] * mix).astype(BF16)

    xa = xa_ref[0]
    xs = xa[:, :D]
    bm = xa[:, D:D + B_GROUPS * B_STATE]
    cm = xa[:, D + B_GROUPS * B_STATE:]

    tril = tril_ref[...]
    dtf = _softplus(dt_ref[0] + dtb_ref[...])
    cs_n = _sel_left(tril, dtf * anar_ref[...], _split3)
    dt_x = _sel_right(dtf, expand_ref[...], _split2)
    cs_x = _sel_right(cs_n, expand_ref[...], _split3)
    ecs_x = jnp.exp(cs_x)
    last_x = cs_x[L - 1:L, :]
    xdt = xs * dt_x
    xdt_b = xdt.astype(BF16)
    xdec_b = (xdt * jnp.exp(last_x - cs_x)).astype(BF16)
    lane = lax.broadcasted_iota(jnp.int32, (L, LANES), 1)
    y_parts = []
    for g in range(B_GROUPS):
        gs = slice(g * B_GROUP_W, (g + 1) * B_GROUP_W)
        bg = bm[:, g * B_STATE:(g + 1) * B_STATE]
        cg = cm[:, g * B_STATE:(g + 1) * B_STATE].astype(BF16)
        gmat = _dot_nt(cg, bg.astype(BF16))
        st = st_ref[g]
        y_off = ecs_x[:, gs] * _dot(cg, st.astype(BF16))
        st_new = st * jnp.exp(last_x[:, gs]) + _dot(bg.T.astype(BF16), xdec_b[:, gs])
        st_ref[g] = st_new
        for pair in range(B_GROUP_W // LANES):
            base = g * B_GROUP_W + pair * LANES
            cs_t = cs_x[:, base:base + LANES].T
            ms = []
            for half in range(2):
                ch = half * B_HEAD_DIM
                diff = cs_x[:, base + ch:base + ch + 1] - cs_t[ch:ch + 1, :]
                ms.append((gmat * jnp.exp(jnp.where(causal, diff, -jnp.inf))).astype(BF16))
            xp2 = xdt_b[:, base:base + LANES]
            rhs = jnp.concatenate([jnp.where(lane < B_HEAD_DIM, xp2, jnp.zeros_like(xp2)),
                                   jnp.where(lane >= B_HEAD_DIM, xp2, jnp.zeros_like(xp2))], axis=0)
            y_parts.append(_dot(jnp.concatenate(ms, axis=1), rhs) + y_off[:, pair * LANES:(pair + 1) * LANES])
    y = jnp.concatenate(y_parts, axis=-1) + dexp_ref[...] * xs
    yb = _group_rms(y * sz_ref[0], bnorm_ref[...], B_GROUP_W)
    yab_ref[0, :, D:] = yb.astype(BF16)

    @pl.when(c == pl.num_programs(1) - 1)
    def _():
        for g in range(B_GROUPS):
            ssm_ref[0, g] = st_ref[g].T


def _even_consts(P):
    head_of_ch = jnp.arange(D) // B_HEAD_DIM
    expand = (jnp.arange(LANES)[:, None] == head_of_ch[None, :]).astype(BF16)
    tril = (jnp.arange(A_CHUNK)[None, :] <= jnp.arange(A_CHUNK)[:, None]).astype(BF16)
    aexp = jnp.repeat(-jnp.exp(P["b_a_log"][0].astype(F32)), B_HEAD_DIM).reshape(1, D)
    dexp = jnp.repeat(P["b_d"][0].astype(F32), B_HEAD_DIM).reshape(1, D)
    dtb = jnp.pad(P["b_dt_bias"][0].astype(F32), (0, LANES - B_HEADS)).reshape(1, LANES)
    return expand, tril, aexp, dexp, dtb


def mixer_even_prompt(gu, vln, sz, xa, dt, P):
    B, T, _ = gu.shape
    L = A_CHUNK
    expand, tril, _, dexp, dtb = _even_consts(P)
    anar = jnp.pad(-jnp.exp(P["b_a_log"][0].astype(F32)), (0, LANES - B_HEADS)).reshape(1, LANES)
    row = lambda a: a.reshape(1, -1)
    full = lambda shape: pl.BlockSpec(shape, lambda b, c: (0,) * len(shape))
    chunk = lambda n: pl.BlockSpec((1, L, n), lambda b, c: (b, c, 0))
    return pl.pallas_call(
        _mixer_even_kernel,
        grid=(B, T // L),
        in_specs=[chunk(D), chunk(D), chunk(D), chunk(B_CONV_DIM), chunk(LANES),
                  full((A_HEADS, L, L)), full((L, A_HEADS)), full((1, LANES)), full((1, LANES)), full((1, D)),
                  full((1, D)), full((L, L)), full((LANES, D))],
        out_specs=[pl.BlockSpec((1, L, 2 * D), lambda b, c: (b, c, 0)),
                   pl.BlockSpec((1, B_GROUPS, B_GROUP_W, B_STATE), lambda b, c: (b, 0, 0, 0))],
        out_shape=[jax.ShapeDtypeStruct((B, T, 2 * D), BF16),
                   jax.ShapeDtypeStruct((B, B_GROUPS, B_GROUP_W, B_STATE), F32)],
        scratch_shapes=[pltpu.VMEM((B_GROUPS, B_STATE, B_GROUP_W), F32)],
        compiler_params=_cparams(("parallel", "arbitrary")),
        name="mixer_even_prompt",
    )(gu, vln, sz, xa, dt, P["a_ws"][0], P["a_bs"][0].T, dtb, anar, dexp, row(P["b_norm"][0]), tril, expand)


def _mixer_even_step_kernel(proj_ref, conv_ref, ssm_ref, ws0_ref, bs0_ref, lng_ref, lnb_ref, cw_ref, cb_ref,
                            dtb_ref, aexp_ref, dexp_ref, bnorm_ref, expand_ref,
                            yab_ref, av_ref, convo_ref, ssmo_ref):
    nb = proj_ref.shape[0]
    u = proj_ref[:, 0:D]
    v = proj_ref[:, D:2 * D]
    z = proj_ref[:, 2 * D:3 * D]
    x = proj_ref[:, 3 * D:3 * D + B_CONV_DIM]
    dt = proj_ref[:, DT_COL_BLOCK * LANES:(DT_COL_BLOCK + 1) * LANES]

    vln = _layernorm(jax.nn.gelu(v), lng_ref[...], lnb_ref[...])
    av_ref[...] = vln
    yab_ref[:, 0:D] = jax.nn.gelu(u) * (ws0_ref[...] * vln + bs0_ref[...])

    conv = cb_ref[...] + cw_ref[B_CONV - 1:B_CONV, :] * x
    for k in range(B_CONV - 1):
        conv = conv + cw_ref[k:k + 1, :] * conv_ref[k]
        if k > 0:
            convo_ref[k - 1] = conv_ref[k]
    convo_ref[B_CONV - 2] = x
    xa = jax.nn.silu(conv)
    xs = xa[:, :D]
    bm = xa[:, D:D + B_GROUPS * B_STATE]
    cm = xa[:, D + B_GROUPS * B_STATE:]
    dtf = _softplus(dt + dtb_ref[...])
    dt_x = _sel_right(dtf, expand_ref[...], _split3)
    dec_x = jnp.exp(dt_x * aexp_ref[...])
    xdt = xs * dt_x
    y_rows = []
    for g in range(B_GROUPS):
        gs = slice(g * B_GROUP_W, (g + 1) * B_GROUP_W)
        dec_t = _rows_to_cols(dec_x[:, gs])
        xdt_t = _rows_to_cols(xdt[:, gs])
        ys = []
        for j in range(nb):
            s_new = ssm_ref[j, g] * dec_t[:, j:j + 1] + xdt_t[:, j:j + 1] * bm[j:j + 1, g * B_STATE:(g + 1) * B_STATE]
            ssmo_ref[j, g] = s_new
            cj = jnp.broadcast_to(cm[j:j + 1, g * B_STATE:(g + 1) * B_STATE], (SUBLANES, B_STATE))
            ys.append(_dot_nt_f32(cj, s_new)[0:1, :])
        y_rows.append(jnp.concatenate(ys, axis=0))
    y = jnp.concatenate(y_rows, axis=-1) + dexp_ref[...] * xs
    yb = _group_rms(y * jax.nn.silu(z), bnorm_ref[...], B_GROUP_W)
    yab_ref[:, D:] = yb


def mixer_even_step(proj, conv0, ssm0, P, nb):
    B = proj.shape[0]
    expand, _, aexp, dexp, dtb = _even_consts(P)
    row = lambda a: a.reshape(1, -1)
    rep = lambda a: jnp.repeat(a.astype(F32), LANES).reshape(1, D)
    full = lambda shape: pl.BlockSpec(shape, lambda i: (0,) * len(shape))
    return pl.pallas_call(
        _mixer_even_step_kernel,
        grid=(B // nb,),
        in_specs=[pl.BlockSpec((nb, proj.shape[1]), lambda i: (i, 0)),
                  pl.BlockSpec((B_CONV - 1, nb, B_CONV_DIM), lambda i: (0, i, 0)),
                  pl.BlockSpec((nb, B_GROUPS, B_GROUP_W, B_STATE), lambda i: (i, 0, 0, 0)),
                  full((1, D)), full((1, D)), full((1, D)), full((1, D)),
                  full((B_CONV, B_CONV_DIM)), full((1, B_CONV_DIM)), full((1, LANES)), full((1, D)), full((1, D)),
                  full((1, D)), full((LANES, D))],
        out_specs=[pl.BlockSpec((nb, 2 * D), lambda i: (i, 0)),
                   pl.BlockSpec((nb, D), lambda i: (i, 0)),
                   pl.BlockSpec((B_CONV - 1, nb, B_CONV_DIM), lambda i: (0, i, 0)),
                   pl.BlockSpec((nb, B_GROUPS, B_GROUP_W, B_STATE), lambda i: (i, 0, 0, 0))],
        out_shape=[jax.ShapeDtypeStruct((B, 2 * D), F32),
                   jax.ShapeDtypeStruct((B, D), F32),
                   jax.ShapeDtypeStruct((B_CONV - 1, B, B_CONV_DIM), F32),
                   jax.ShapeDtypeStruct((B, B_GROUPS, B_GROUP_W, B_STATE), F32)],
        compiler_params=_cparams(("parallel",)),
        name="mixer_even_step",
    )(proj, conv0, ssm0, rep(P["a_ws"][0][:, 0, 0]), rep(P["a_bs"][0][:, 0]), row(P["a_ln_g"][0]), row(P["a_ln_b"][0]),
      P["b_conv_w"][0], row(P["b_conv_b"][0]), dtb, aexp, dexp, row(P["b_norm"][0]), expand)


def _gla_gates(q_raw, f_raw, lb):
    fg = lb + (1.0 - lb) * jax.nn.sigmoid(f_raw)
    return jax.nn.silu(q_raw), fg, 1.0 - fg


def _hgrn_kernel(q_ref, f_ref, i_ref, g_ref, lb_ref, cnorm_ref, tril_ref, o_ref, s_ref, st_ref):
    c = pl.program_id(1)
    L = C_CHUNK
    R = q_ref.shape[1]

    @pl.when(c == 0)
    def _():
        st_ref[...] = jnp.zeros_like(st_ref)

    row = lax.broadcasted_iota(jnp.int32, (R, R), 0)
    col = lax.broadcasted_iota(jnp.int32, (R, R), 1)
    causal = (col <= row) & (row // L == col // L)
    chunk_of_row = lax.broadcasted_iota(jnp.int32, (R, 1), 0) // L
    q, fg, k = _gla_gates(q_ref[0], f_ref[0], lb_ref[...])
    v = i_ref[0]
    bc = _sel_left(tril_ref[...], jnp.log(fg), _split3)
    q_in = (q * jnp.exp(bc)).astype(BF16)
    k_in = (k * jnp.exp(-bc)).astype(BF16)
    vb = v.astype(BF16)
    btots, k_decs = [], []
    for s in range(R // L):
        btot = bc[(s + 1) * L - 1:(s + 1) * L, :]
        btots.append(btot)
        k_decs.append(jnp.where(chunk_of_row == s, k * jnp.exp(btot - bc), 0.0).astype(BF16))
    outs = []
    for hh in range(C_HEADS):
        sl = slice(hh * C_KDIM, (hh + 1) * C_KDIM)
        att = jnp.where(causal, _dot_nt(q_in[:, sl], k_in[:, sl]), 0.0)
        v_t = v[:, sl].T.astype(BF16)
        st = st_ref[hh]
        inter = []
        for s in range(R // L):
            inter.append(_dot_nt(q_in[s * L:(s + 1) * L, sl], st.astype(BF16)))
            st = st * jnp.exp(btots[s][:, sl]) + _dot(v_t, k_decs[s][:, sl])
        st_ref[hh] = st
        outs.append(_dot(att.astype(BF16), vb[:, sl]) + jnp.concatenate(inter, axis=0))
    o = _group_rms(jnp.concatenate(outs, axis=-1), cnorm_ref[...], C_KDIM)
    o_ref[0] = (o * jax.nn.silu(g_ref[0])).astype(BF16)

    @pl.when(c == pl.num_programs(1) - 1)
    def _():
        for hh in range(C_HEADS):
            s_ref[0, hh] = st_ref[hh].T


def hgrn_prompt(proj, lb, cnorm):
    B, T, _ = proj.shape
    L = C_ROWS
    r = jnp.arange(L)
    tril = ((r[None, :] <= r[:, None]) & (r[None, :] // C_CHUNK == r[:, None] // C_CHUNK)).astype(BF16)
    full = lambda shape: pl.BlockSpec(shape, lambda b, c: (0,) * len(shape))
    return pl.pallas_call(
        _hgrn_kernel,
        grid=(B, T // L),
        in_specs=[pl.BlockSpec((1, L, D), lambda b, c: (b, c, 0)),
                  pl.BlockSpec((1, L, D), lambda b, c: (b, c, 1)),
                  pl.BlockSpec((1, L, D), lambda b, c: (b, c, 2)),
                  pl.BlockSpec((1, L, D), lambda b, c: (b, c, 3)),
                  full((1, D)), full((1, D)), full((L, L))],
        out_specs=[pl.BlockSpec((1, L, D), lambda b, c: (b, c, 0)),
                   pl.BlockSpec((1, C_HEADS, C_KDIM, C_KDIM), lambda b, c: (b, 0, 0, 0))],
        out_shape=[jax.ShapeDtypeStruct((B, T, D), BF16),
                   jax.ShapeDtypeStruct((B, C_HEADS, C_KDIM, C_KDIM), F32)],
        scratch_shapes=[pltpu.VMEM((C_HEADS, C_KDIM, C_KDIM), F32)],
        compiler_params=_cparams(("parallel", "arbitrary")),
        name="hgrn_prompt",
    )(proj, proj, proj, proj, lb.reshape(1, D), cnorm.reshape(1, D), tril)


def _hgrn_step_kernel(proj_ref, s_ref, lb_ref, cnorm_ref, o_ref, so_ref):
    nb = proj_ref.shape[0]
    q, fg, k = _gla_gates(proj_ref[:, 0:D], proj_ref[:, D:2 * D], lb_ref[...])
    v = proj_ref[:, 2 * D:3 * D]
    g = proj_ref[:, 3 * D:4 * D]
    outs = []
    for hh in range(C_HEADS):
        sl = slice(hh * C_KDIM, (hh + 1) * C_KDIM)
        fg_t = _rows_to_cols(fg[:, sl])
        k_t = _rows_to_cols(k[:, sl])
        rows = []
        for j in range(nb):
            s_new = s_ref[j, hh] * fg_t[:, j:j + 1] + k_t[:, j:j + 1] * v[j:j + 1, sl]
            so_ref[j, hh] = s_new
            qj = jnp.broadcast_to(q[j:j + 1, sl], (SUBLANES, C_KDIM))
            rows.append(_dot_f32(qj, s_new)[0:1, :])
        outs.append(jnp.concatenate(rows, axis=0))
    o = _group_rms(jnp.concatenate(outs, axis=-1), cnorm_ref[...], C_KDIM)
    o_ref[...] = o * jax.nn.silu(g)


def hgrn_step(proj, s0, lb, cnorm, nb):
    B = proj.shape[0]
    full = lambda shape: pl.BlockSpec(shape, lambda i: (0,) * len(shape))
    return pl.pallas_call(
        _hgrn_step_kernel,
        grid=(B // nb,),
        in_specs=[pl.BlockSpec((nb, 4 * D), lambda i: (i, 0)),
                  pl.BlockSpec((nb, C_HEADS, C_KDIM, C_KDIM), lambda i: (i, 0, 0, 0)),
                  full((1, D)), full((1, D))],
        out_specs=[pl.BlockSpec((nb, D), lambda i: (i, 0)),
                   pl.BlockSpec((nb, C_HEADS, C_KDIM, C_KDIM), lambda i: (i, 0, 0, 0))],
        out_shape=[jax.ShapeDtypeStruct((B, D), F32),
                   jax.ShapeDtypeStruct((B, C_HEADS, C_KDIM, C_KDIM), F32)],
        compiler_params=_cparams(("parallel",)),
        name="hgrn_step",
    )(proj, s0, lb.reshape(1, D), cnorm.reshape(1, D))


TM_PROJ = 256
TM_OUT = 512
TM_ATTN = 512
TM_FFN = 1024
TF_DENSE = 256
TF_MOE = 896
TF_ROUTED = 512
TG_MOE = 1024
TM_ROUTE = 512
TM_DISPATCH = 1024
TM_COMBINE = 512
DMA_UNROLL = 8
ROUTED_MIN_TOKENS = 8 * TG_MOE
STEP_NB = 8
XATTN_STEP_NB = 4
TN_STEP = 512
TK_STEP = 512
IN0_STEP_PAD = 5120


def _prep_weights(P):
    W = {}
    W["ev_w_in"] = jnp.pad(P["ev_w_in"][0], ((0, 0), (0, IN0_PAD - IN0))).astype(BF16)
    W["ev_w_out"] = P["ev_w_out"][0].astype(BF16)
    W["od_w_in"] = P["od_w_in"][0].astype(BF16)
    W["od_w_out"] = P["od_w_out"][0].astype(BF16)
    W["xa_wq"] = P["xa_wq"].astype(BF16)
    W["xa_wo"] = P["xa_wo"].astype(BF16)
    W["xa_wk"] = P["xa_wk"].astype(BF16)
    W["xa_wv"] = P["xa_wv"].astype(BF16)
    W["ffn"] = tuple(P[n].astype(BF16) for n in ("ffn_w_gate", "ffn_w_up", "ffn_w_down"))
    W["moe"] = tuple(P[n][0].astype(BF16) for n in ("moe_w_gate", "moe_w_up", "moe_w_down"))
    W["router"] = jnp.pad(P["moe_router"][0].astype(F32), ((0, 0), (0, LANES - N_EXPERTS)))
    lbp = jax.nn.softmax(P["hgrn_lb_logits"].astype(F32), axis=0)
    W["lower_bounds"] = jnp.cumsum(lbp, axis=0) - lbp[0]
    return W


def _channel_mix(x2, l, P, W, tm):
    if l == 0:
        wg, wu, wd = W["ffn"]
        return ffn(x2, P["norm_ffn_pre"][l], P["norm_ffn_post"][l], W["router"], wg, wu, wd, tm, TF_DENSE, False)
    wg, wu, wd = W["moe"]
    if x2.shape[0] >= ROUTED_MIN_TOKENS:
        return moe_routed(x2, P["norm_ffn_pre"][l], P["norm_ffn_post"][l], W["router"], wg, wu, wd)
    return ffn(x2, P["norm_ffn_pre"][l], P["norm_ffn_post"][l], W["router"], wg, wu, wd, tm, TF_MOE, True)


def _trunk_prompt(x, mem_k, mem_v, P, W):
    B, T, _ = x.shape
    M = B * T
    x2 = x.reshape(M, D)
    gu, vln, sz, xa, dt, tail = ev_proj(x, P["norm_mix_pre"][0], W["ev_w_in"], P, TM_PROJ)
    yab, ssm = mixer_even_prompt(gu, vln, sz, xa, dt, P)
    conv = tail[:, SUBLANES - (B_CONV - 1):, :]
    x2 = matmul_norm_res(yab.reshape(M, 2 * D), W["ev_w_out"], P["norm_mix_post"][0], x2, TM_OUT)
    x2 = xattn_prompt(x2.reshape(B, T, D), mem_k, mem_v, 0, W["xa_wq"][0], W["xa_wo"][0],
                      P["norm_x_pre"][0], P["norm_x_post"][0], TM_ATTN).reshape(M, D)
    x2 = _channel_mix(x2, 0, P, W, TM_FFN)
    proj = norm_matmul(x2, P["norm_mix_pre"][1], W["od_w_in"], TM_PROJ).reshape(B, T, 4 * D)
    o, hgrn = hgrn_prompt(proj, W["lower_bounds"][1], P["c_norm"][0])
    x2 = matmul_norm_res(o.reshape(M, D), W["od_w_out"], P["norm_mix_post"][1], x2, TM_OUT)
    x2 = xattn_prompt(x2.reshape(B, T, D), mem_k, mem_v, 1, W["xa_wq"][1], W["xa_wo"][1],
                      P["norm_x_pre"][1], P["norm_x_post"][1], TM_ATTN).reshape(M, D)
    x2 = _channel_mix(x2, 1, P, W, TM_FFN)
    return x2.reshape(B, T, D), conv, ssm, hgrn


def _xattn_step(x2, l, mem_k, mem_v, P):
    B = x2.shape[0]
    q = norm_matmul(x2, P["norm_x_pre"][l], P["xa_wq"][l], B, TN_STEP)
    o = xattn_sample_core(q.reshape(B, X_HEADS, X_HEAD_DIM), mem_k, mem_v, l, XATTN_STEP_NB).reshape(B, D)
    return matmul_norm_res(o, P["xa_wo"][l], P["norm_x_post"][l], x2, B)


def _trunk_step(x, mem_k, mem_v, conv0, ssm0, hgrn0, P, W):
    B = x.shape[0]
    x2 = x.reshape(B, D)
    w_in = jnp.pad(P["ev_w_in"][0], ((0, 0), (0, IN0_STEP_PAD - IN0)))
    proj = norm_matmul(x2, P["norm_mix_pre"][0], w_in, B, TN_STEP)
    yab, av, conv, ssm = mixer_even_step(proj, jnp.swapaxes(conv0, 0, 1), ssm0, P, STEP_NB)
    x2 = matmul_norm_res(yab, P["ev_w_out"][0], P["norm_mix_post"][0], x2, B, TK_STEP)
    x2 = _xattn_step(x2, 0, mem_k, mem_v, P)
    x2 = ffn(x2, P["norm_ffn_pre"][0], P["norm_ffn_post"][0], W["router"],
             P["ffn_w_gate"], P["ffn_w_up"], P["ffn_w_down"], B, TF_DENSE, False)
    proj = norm_matmul(x2, P["norm_mix_pre"][1], P["od_w_in"][0], B, TN_STEP)
    o, hgrn = hgrn_step(proj, hgrn0, W["lower_bounds"][1], P["c_norm"][0], STEP_NB)
    x2 = matmul_norm_res(o, P["od_w_out"][0], P["norm_mix_post"][1], x2, B)
    x2 = _xattn_step(x2, 1, mem_k, mem_v, P)
    x2 = _channel_mix(x2, 1, P, W, B)
    return x2.reshape(B, 1, D), jnp.swapaxes(conv, 0, 1), ssm, hgrn, av


def kernel(x_prompt, x_sample, mem_prompt, cache_mem_k, cache_mem_v, state_conv, state_ssm, state_hgrn,
           norm_mix_pre, norm_mix_post, norm_x_pre, norm_x_post, norm_ffn_pre, norm_ffn_post, norm_mem,
           xa_wq, xa_wk, xa_wv, xa_wo,
           ev_w_in, a_ws, a_bs, a_ln_g, a_ln_b, b_conv_w, b_conv_b, b_dt_bias, b_a_log, b_d, b_norm, ev_w_out,
           ffn_w_gate, ffn_w_up, ffn_w_down,
           od_w_in, hgrn_lb_logits, c_norm, od_w_out,
           moe_router, moe_w_gate, moe_w_up, moe_w_down):
    P = dict(norm_mix_pre=norm_mix_pre, norm_mix_post=norm_mix_post, norm_x_pre=norm_x_pre, norm_x_post=norm_x_post,
             norm_ffn_pre=norm_ffn_pre, norm_ffn_post=norm_ffn_post, xa_wq=xa_wq, xa_wk=xa_wk, xa_wv=xa_wv,
             xa_wo=xa_wo, ev_w_in=ev_w_in, a_ws=a_ws, a_bs=a_bs, a_ln_g=a_ln_g, a_ln_b=a_ln_b, b_conv_w=b_conv_w,
             b_conv_b=b_conv_b, b_dt_bias=b_dt_bias, b_a_log=b_a_log, b_d=b_d, b_norm=b_norm, ev_w_out=ev_w_out,
             ffn_w_gate=ffn_w_gate, ffn_w_up=ffn_w_up, ffn_w_down=ffn_w_down, od_w_in=od_w_in,
             hgrn_lb_logits=hgrn_lb_logits, c_norm=c_norm, od_w_out=od_w_out, moe_router=moe_router,
             moe_w_gate=moe_w_gate, moe_w_up=moe_w_up, moe_w_down=moe_w_down)
    W = _prep_weights(P)
    depth = norm_mem.shape[0]
    bp, T, _ = x_prompt.shape
    bs = x_sample.shape[0]

    mem_k_p, mem_v_p = mem_kv(mem_prompt.reshape(bp * MEM_LEN, D), norm_mem, W["xa_wk"], W["xa_wv"], TM_PROJ)
    mem_k_p = mem_k_p.reshape(depth, bp, MEM_LEN, D)
    mem_v_p = mem_v_p.reshape(depth, bp, MEM_LEN, D)
    y_p, conv_p, ssm_p, hgrn_p = _trunk_prompt(x_prompt, mem_k_p, mem_v_p, P, W)

    y_s, conv_s, ssm_s, hgrn_s, av_s = _trunk_step(
        x_sample, cache_mem_k, cache_mem_v,
        state_conv[0], state_ssm[0].reshape(bs, B_GROUPS, B_GROUP_W, B_STATE), state_hgrn[0], P, W)

    kv_shape = (depth, bp, MEM_LEN, X_HEADS, X_HEAD_DIM)
    ssm_shape = (B_GROUPS, B_GROUP_W // B_HEAD_DIM, B_HEAD_DIM, B_STATE)
    return (y_p, y_s, mem_k_p.reshape(kv_shape), mem_v_p.reshape(kv_shape),
            conv_p[None], ssm_p.reshape((1, bp) + ssm_shape), hgrn_p[None],
            conv_s[None], ssm_s.reshape((1, bs) + ssm_shape), hgrn_s[None], av_s.reshape(1, bs, 1, D))
```

```python
import functools

import jax
import jax.numpy as jnp
from jax import lax
from jax.experimental import pallas as pl
from jax.experimental.pallas import tpu as pltpu

F32 = jnp.float32
BF16 = jnp.bfloat16
EPS = 1e-6

D = 1024
LANES = 128
SUBLANES = 8
A_HEADS = 8
A_CHUNK = 128
A_ROWS = 512
B_HEADS = 16
B_HEAD_DIM = 64
B_GROUPS = 2
B_GROUP_W = 512
B_STATE = 128
B_CONV = 4
B_CONV_DIM = 1536
IN0 = 4624
IN0_PAD = 4736
DT_COL_BLOCK = 36
C_HEADS = 8
C_KDIM = 128
C_CHUNK = 64
C_ROWS = 128
C_STEP_ROWS = 512
X_HEADS = 4
X_HEAD_DIM = 256
MEM_LEN = 256
N_EXPERTS = 8

VMEM_LIMIT = 56 * 1024 * 1024


def _cparams(sem):
    return pltpu.CompilerParams(dimension_semantics=sem, vmem_limit_bytes=VMEM_LIMIT)


def _dot(a, b):
    return jnp.dot(a, b, preferred_element_type=F32)


def _dot_nt(a, b):
    return lax.dot_general(a, b, (((1,), (1,)), ((), ())), preferred_element_type=F32)


def _dot_f32(a, b):
    return jnp.dot(a, b, precision=lax.Precision.HIGHEST, preferred_element_type=F32)


def _dot_nt_f32(a, b):
    return lax.dot_general(a, b, (((1,), (1,)), ((), ())), precision=lax.Precision.HIGHEST,
                           preferred_element_type=F32)


def _mm(a, w):
    if w.dtype == F32:
        return _dot_f32(a.astype(F32), w)
    return _dot(a.astype(BF16), w)


def _rms(x, g):
    return x * lax.rsqrt(jnp.mean(x * x, axis=-1, keepdims=True) + EPS) * g


def _split2(x):
    hi = x.astype(BF16)
    lo = (x - hi.astype(F32)).astype(BF16)
    return hi, lo


def _split3(x):
    hi = x.astype(BF16)
    r = x - hi.astype(F32)
    mid = r.astype(BF16)
    lo = (r - mid.astype(F32)).astype(BF16)
    return hi, mid, lo


def _sel_left(m, x, parts):
    out = None
    for p in parts(x):
        t = _dot(m, p)
        out = t if out is None else out + t
    return out


def _sel_right(x, m, parts):
    out = None
    for p in parts(x):
        t = _dot(p, m)
        out = t if out is None else out + t
    return out


def _rows_to_cols(x):
    n, w = x.shape
    if n < LANES:
        x = jnp.concatenate([x, jnp.zeros((LANES - n, w), x.dtype)], axis=0)
    return x.T


def _softplus(x):
    return jnp.maximum(x, 0.0) + jnp.log1p(jnp.exp(-jnp.abs(x)))


def _layernorm(x, g, b):
    xc = x - jnp.mean(x, axis=-1, keepdims=True)
    return xc * lax.rsqrt(jnp.mean(xc * xc, axis=-1, keepdims=True) + EPS) * g + b


def _group_rms(x, g, width):
    parts = []
    for s in range(0, x.shape[-1], width):
        t = x[:, s:s + width]
        parts.append(t * lax.rsqrt(jnp.mean(t * t, axis=-1, keepdims=True) + EPS))
    return jnp.concatenate(parts, axis=-1) * g


def _norm_matmul_kernel(x_ref, g_ref, w_ref, o_ref):
    o_ref[...] = _mm(_rms(x_ref[...], g_ref[...]), w_ref[...])


def norm_matmul(x, g, w, tm, tn=None):
    M, K = x.shape
    N = w.shape[1]
    tn = N if tn is None else tn
    return pl.pallas_call(
        _norm_matmul_kernel,
        grid=(M // tm, N // tn),
        in_specs=[pl.BlockSpec((tm, K), lambda i, j: (i, 0)),
                  pl.BlockSpec((1, K), lambda i, j: (0, 0)),
                  pl.BlockSpec((K, tn), lambda i, j: (0, j))],
        out_specs=pl.BlockSpec((tm, tn), lambda i, j: (i, j)),
        out_shape=jax.ShapeDtypeStruct((M, N), F32),
        compiler_params=_cparams(("parallel", "parallel")),
        name="norm_matmul",
    )(x, g.reshape(1, K), w)


def _mem_kv_kernel(x_ref, g_ref, wk_ref, wv_ref, k_ref, v_ref):
    h = _rms(x_ref[...], g_ref[0]).astype(BF16)
    k_ref[0] = _dot(h, wk_ref[0])
    v_ref[0] = _dot(h, wv_ref[0])


def mem_kv(mem, g, wk, wv, tm):
    M, K = mem.shape
    depth = g.shape[0]
    wspec = pl.BlockSpec((1, K, K), lambda l, i: (l, 0, 0))
    ospec = pl.BlockSpec((1, tm, K), lambda l, i: (l, i, 0))
    return pl.pallas_call(
        _mem_kv_kernel,
        grid=(depth, M // tm),
        in_specs=[pl.BlockSpec((tm, K), lambda l, i: (i, 0)),
                  pl.BlockSpec((1, 1, K), lambda l, i: (l, 0, 0)), wspec, wspec],
        out_specs=[ospec, ospec],
        out_shape=[jax.ShapeDtypeStruct((depth, M, K), F32)] * 2,
        compiler_params=_cparams(("parallel", "parallel")),
        name="mem_kv",
    )(mem, g.reshape(depth, 1, K), wk, wv)


def _matmul_norm_res_kernel(a_ref, w_ref, g_ref, r_ref, o_ref, acc_ref):
    k = pl.program_id(1)

    @pl.when(k == 0)
    def _():
        acc_ref[...] = jnp.zeros_like(acc_ref)

    acc_ref[...] += _mm(a_ref[...], w_ref[...])

    @pl.when(k == pl.num_programs(1) - 1)
    def _():
        o_ref[...] = r_ref[...] + _rms(acc_ref[...], g_ref[...])


def matmul_norm_res(a, w, g, res, tm, tk=None):
    M, K = a.shape
    N = w.shape[1]
    tk = K if tk is None else tk
    return pl.pallas_call(
        _matmul_norm_res_kernel,
        grid=(M // tm, K // tk),
        in_specs=[pl.BlockSpec((tm, tk), lambda i, k: (i, k)),
                  pl.BlockSpec((tk, N), lambda i, k: (k, 0)),
                  pl.BlockSpec((1, N), lambda i, k: (0, 0)),
                  pl.BlockSpec((tm, N), lambda i, k: (i, 0))],
        out_specs=pl.BlockSpec((tm, N), lambda i, k: (i, 0)),
        out_shape=jax.ShapeDtypeStruct((M, N), F32),
        scratch_shapes=[pltpu.VMEM((tm, N), F32)],
        compiler_params=_cparams(("parallel", "arbitrary")),
        name="matmul_norm_res",
    )(a, w, g.reshape(1, N), res)


def _top2(h, router, three_pass=False):
    if three_pass:
        hh, hl = _split2(h)
        rh, rl = _split2(router)
        lg = _dot(hh, rh) + _dot(hh, rl) + _dot(hl, rh)
    else:
        lg = _dot_f32(h, router)
    lane = lax.broadcasted_iota(jnp.int32, lg.shape, 1)
    lg = jnp.where(lane < N_EXPERTS, lg, -jnp.inf)
    m1 = jnp.max(lg, axis=-1, keepdims=True)
    i1 = jnp.min(jnp.where(lg == m1, lane, LANES), axis=-1, keepdims=True)
    lg2 = jnp.where(lane == i1, -jnp.inf, lg)
    m2 = jnp.max(lg2, axis=-1, keepdims=True)
    i2 = jnp.min(jnp.where(lg2 == m2, lane, LANES), axis=-1, keepdims=True)
    e2 = jnp.exp(m2 - m1)
    den = 1.0 + e2
    return lane, i1, i2, 1.0 / den, e2 / den


def _top2_gates(h, router):
    lane, i1, i2, w1, w2 = _top2(h, router)
    return jnp.where(lane == i1, w1, 0.0) + jnp.where(lane == i2, w2, 0.0)


def _ffn_kernel(x_ref, gpre_ref, gpost_ref, router_ref, wg_ref, wu_ref, wd_ref, o_ref,
                h_ref, acc_ref, gates_ref, *, moe):
    e = pl.program_id(1)
    f = pl.program_id(2)

    @pl.when((e == 0) & (f == 0))
    def _():
        h = _rms(x_ref[...], gpre_ref[...])
        h_ref[...] = h.astype(h_ref.dtype)
        acc_ref[...] = jnp.zeros_like(acc_ref)
        if moe:
            gates_ref[...] = _top2_gates(h, router_ref[...])
        else:
            gates_ref[...] = jnp.zeros_like(gates_ref)

    hb = h_ref[...]
    act = jax.nn.silu(_mm(hb, wg_ref[0].astype(hb.dtype))) * _mm(hb, wu_ref[0].astype(hb.dtype))
    if moe:
        gates = gates_ref[...]
        lane = lax.broadcasted_iota(jnp.int32, gates.shape, 1)
        act = act * jnp.sum(jnp.where(lane == e, gates, 0.0), axis=-1, keepdims=True)
    acc_ref[...] += _mm(act, wd_ref[0].astype(hb.dtype))

    @pl.when((e == pl.num_programs(1) - 1) & (f == pl.num_programs(2) - 1))
    def _():
        o_ref[...] = x_ref[...] + _rms(acc_ref[...], gpost_ref[...])


def ffn(x, gpre, gpost, router, wg, wu, wd, tm, tf, moe, precise=False):
    M, K = x.shape
    E, _, F = wg.shape
    return pl.pallas_call(
        functools.partial(_ffn_kernel, moe=moe),
        grid=(M // tm, E, F // tf),
        in_specs=[pl.BlockSpec((tm, K), lambda i, e, f: (i, 0)),
                  pl.BlockSpec((1, K), lambda i, e, f: (0, 0)),
                  pl.BlockSpec((1, K), lambda i, e, f: (0, 0)),
                  pl.BlockSpec((K, LANES), lambda i, e, f: (0, 0)),
                  pl.BlockSpec((1, K, tf), lambda i, e, f: (e, 0, f)),
                  pl.BlockSpec((1, K, tf), lambda i, e, f: (e, 0, f)),
                  pl.BlockSpec((1, tf, K), lambda i, e, f: (e, f, 0))],
        out_specs=pl.BlockSpec((tm, K), lambda i, e, f: (i, 0)),
        out_shape=jax.ShapeDtypeStruct((M, K), F32),
        scratch_shapes=[pltpu.VMEM((tm, K), F32 if precise else BF16), pltpu.VMEM((tm, K), F32), pltpu.VMEM((tm, LANES), F32)],
        compiler_params=_cparams(("parallel", "arbitrary", "arbitrary")),
        name="moe_ffn" if moe else "dense_ffn",
    )(x, gpre.reshape(1, K), gpost.reshape(1, K), router, wg, wu, wd)


INFO_E1, INFO_E2, INFO_R1, INFO_R2, INFO_W1, INFO_W2 = range(6)


def _route_kernel(x_ref, gpre_ref, router_ref, tril_ref, info_ref, cnt_ref, carry_ref):
    i = pl.program_id(0)

    @pl.when(i == 0)
    def _():
        carry_ref[...] = jnp.zeros_like(carry_ref)

    lane, i1, i2, w1, w2 = _top2(_rms(x_ref[...], gpre_ref[...]), router_ref[...], three_pass=True)
    sel =jnp.where((lane == i1) | (lane == i2), 1.0, 0.0)
    incl = _dot(tril_ref[...], sel.astype(BF16))
    rank = incl - sel + carry_ref[...]
    r1 = jnp.sum(jnp.where(lane == i1, rank, 0.0), axis=-1, keepdims=True)
    r2 = jnp.sum(jnp.where(lane == i2, rank, 0.0), axis=-1, keepdims=True)
    info = jnp.zeros(sel.shape, F32)
    for idx, val in ((INFO_E1, i1.astype(F32)), (INFO_E2, i2.astype(F32)), (INFO_R1, r1), (INFO_R2, r2),
                     (INFO_W1, w1), (INFO_W2, w2)):
        info = jnp.where(lane == idx, val, info)
    info_ref[...] = info
    carry_ref[...] += incl[incl.shape[0] - 1:, :]
    cnt_ref[...] = carry_ref[...]


def moe_route(x, gpre, router, tm):
    M, K = x.shape
    tril = (jnp.arange(tm)[None, :] <= jnp.arange(tm)[:, None]).astype(BF16)
    return pl.pallas_call(
        _route_kernel,
        grid=(M // tm,),
        in_specs=[pl.BlockSpec((tm, K), lambda i: (i, 0)),
                  pl.BlockSpec((1, K), lambda i: (0, 0)),
                  pl.BlockSpec((K, LANES), lambda i: (0, 0)),
                  pl.BlockSpec((tm, tm), lambda i: (0, 0))],
        out_specs=[pl.BlockSpec((tm, LANES), lambda i: (i, 0)),
                   pl.BlockSpec((1, LANES), lambda i: (0, 0))],
        out_shape=[jax.ShapeDtypeStruct((M, LANES), F32), jax.ShapeDtypeStruct((1, LANES), F32)],
        scratch_shapes=[pltpu.VMEM((1, LANES), F32)],
        compiler_params=_cparams(("arbitrary",)),
        name="moe_route",
    )(x, gpre.reshape(1, K), router, tril)


def _dispatch_kernel(dest_ref, x_ref, zeros_ref, xs_ref, sem):
    del zeros_ref
    tm = x_ref.shape[0]
    base = pl.program_id(0) * tm

    def issue(r, carry):
        for k in range(2):
            d = dest_ref[(base + r) * 2 + k]
            pltpu.make_async_copy(x_ref.at[pl.ds(r, 1), :], xs_ref.at[pl.ds(d, 1), :], sem).start()
        return carry

    lax.fori_loop(0, tm, issue, 0, unroll=DMA_UNROLL)
    for k in range(2):
        pltpu.make_async_copy(x_ref, xs_ref.at[pl.ds(0, tm), :], sem).wait()


def moe_dispatch(x, dest, n_slots, tm):
    M, K = x.shape
    return pl.pallas_call(
        _dispatch_kernel,
        grid_spec=pltpu.PrefetchScalarGridSpec(
            num_scalar_prefetch=1,
            grid=(M // tm,),
            in_specs=[pl.BlockSpec((tm, K), lambda i, dest: (i, 0)),
                      pl.BlockSpec(memory_space=pl.ANY)],
            out_specs=pl.BlockSpec(memory_space=pl.ANY),
            scratch_shapes=[pltpu.SemaphoreType.DMA(())]),
        out_shape=jax.ShapeDtypeStruct((n_slots, K), F32),
        input_output_aliases={2: 0},
        compiler_params=_cparams(("arbitrary",)),
        name="moe_dispatch",
    )(dest, x, jnp.zeros((n_slots, K), F32))


def _experts_kernel(te_ref, tv_ref, xs_ref, gpre_ref, wg_ref, wu_ref, wd_ref, o_ref, h_ref, acc_ref):
    del te_ref
    i = pl.program_id(0)
    f = pl.program_id(1)
    last = pl.num_programs(1) - 1

    @pl.when(tv_ref[i] == 1)
    def _():
        @pl.when(f == 0)
        def _():
            h_ref[...] = _rms(xs_ref[...], gpre_ref[...]).astype(BF16)
            acc_ref[...] = jnp.zeros_like(acc_ref)

        hb = h_ref[...]
        act = jax.nn.silu(_dot(hb, wg_ref[0].astype(BF16))) * _dot(hb, wu_ref[0].astype(BF16))
        acc_ref[...] += _dot(act.astype(BF16), wd_ref[0].astype(BF16))

        @pl.when(f == last)
        def _():
            o_ref[...] = acc_ref[...]

    @pl.when((tv_ref[i] == 0) & (f == last))
    def _():
        o_ref[...] = jnp.zeros_like(o_ref)


def moe_experts(xs, gpre, tile_expert, tile_valid, wg, wu, wd, tg, tf):
    S, K = xs.shape
    F = wg.shape[2]
    nf = F // tf
    fidx = lambda i, f, te, tv: jnp.where(tv[i] == 1, f, nf - 1)
    return pl.pallas_call(
        _experts_kernel,
        grid_spec=pltpu.PrefetchScalarGridSpec(
            num_scalar_prefetch=2,
            grid=(S // tg, nf),
            in_specs=[pl.BlockSpec((tg, K), lambda i, f, te, tv: (i, 0)),
                      pl.BlockSpec((1, K), lambda i, f, te, tv: (0, 0)),
                      pl.BlockSpec((1, K, tf), lambda i, f, te, tv: (te[i], 0, fidx(i, f, te, tv))),
                      pl.BlockSpec((1, K, tf), lambda i, f, te, tv: (te[i], 0, fidx(i, f, te, tv))),
                      pl.BlockSpec((1, tf, K), lambda i, f, te, tv: (te[i], fidx(i, f, te, tv), 0))],
            out_specs=pl.BlockSpec((tg, K), lambda i, f, te, tv: (i, 0)),
            scratch_shapes=[pltpu.VMEM((tg, K), BF16), pltpu.VMEM((tg, K), F32)]),
        out_shape=jax.ShapeDtypeStruct((S, K), F32),
        compiler_params=_cparams(("parallel", "arbitrary")),
        name="moe_experts",
    )(tile_expert, tile_valid, xs, gpre.reshape(1, K), wg, wu, wd)


def _combine_kernel(dest_ref, x_ref, info_ref, gpost_ref, ys_ref, o_ref, buf_ref, sem):
    tm = x_ref.shape[0]
    i = pl.program_id(0)

    def gather(tile, slot):
        def issue(r, carry):
            for k in range(2):
                d = dest_ref[(tile * tm + r) * 2 + k]
                pltpu.make_async_copy(ys_ref.at[pl.ds(d, 1), :], buf_ref.at[slot, k, pl.ds(r, 1), :],
                                      sem.at[slot]).start()
            return carry

        lax.fori_loop(0, tm, issue, 0, unroll=DMA_UNROLL)

    @pl.when(i == 0)
    def _():
        gather(0, 0)

    @pl.when(i + 1 < pl.num_programs(0))
    def _():
        gather(i + 1, (i + 1) % 2)

    slot = i % 2
    for k in range(2):
        pltpu.make_async_copy(ys_ref.at[pl.ds(0, tm), :], buf_ref.at[slot, k], sem.at[slot]).wait()
    info = info_ref[...]
    y = info[:, INFO_W1:INFO_W1 + 1] * buf_ref[slot, 0] + info[:, INFO_W2:INFO_W2 + 1] * buf_ref[slot, 1]
    o_ref[...] = x_ref[...] + _rms(y, gpost_ref[...])


def moe_combine(x, info, gpost, ys, dest, tm):
    M, K = x.shape
    return pl.pallas_call(
        _combine_kernel,
        grid_spec=pltpu.PrefetchScalarGridSpec(
            num_scalar_prefetch=1,
            grid=(M // tm,),
            in_specs=[pl.BlockSpec((tm, K), lambda i, dest: (i, 0)),
                      pl.BlockSpec((tm, LANES), lambda i, dest: (i, 0)),
                      pl.BlockSpec((1, K), lambda i, dest: (0, 0)),
                      pl.BlockSpec(memory_space=pl.ANY)],
            out_specs=pl.BlockSpec((tm, K), lambda i, dest: (i, 0)),
            scratch_shapes=[pltpu.VMEM((2, 2, tm, K), F32), pltpu.SemaphoreType.DMA((2,))]),
        out_shape=jax.ShapeDtypeStruct((M, K), F32),
        compiler_params=_cparams(("arbitrary",)),
        name="moe_combine",
    )(dest, x, info, gpost.reshape(1, K), ys)


def moe_routed(x, gpre, gpost, router, wg, wu, wd):
    M, K = x.shape
    tg = TG_MOE
    n_slots = 2 * M + N_EXPERTS * tg
    info, cnt = moe_route(x, gpre, router, TM_ROUTE)
    cnt = cnt[0, :N_EXPERTS].astype(jnp.int32)
    padded = (cnt + tg - 1) // tg * tg
    ends = jnp.cumsum(padded)
    offs = ends - padded
    ids = info[:, INFO_E1:INFO_E2 + 1].astype(jnp.int32)
    ranks = info[:, INFO_R1:INFO_R2 + 1].astype(jnp.int32)
    dest = (offs[ids] + ranks).reshape(2 * M)
    tile_start = jnp.arange(n_slots // tg, dtype=jnp.int32) * tg
    tile_valid = (tile_start < ends[-1]).astype(jnp.int32)
    tile_expert = jnp.minimum(jnp.sum((tile_start[:, None] >= ends[None, :]).astype(jnp.int32), axis=1),
                              N_EXPERTS - 1)
    tile_expert = jnp.where(tile_valid == 1, tile_expert, tile_expert[jnp.maximum(ends[-1] // tg - 1, 0)])
    xs = moe_dispatch(x, dest, n_slots, TM_DISPATCH)
    ys = moe_experts(xs, gpre, tile_expert, tile_valid, wg, wu, wd, tg, TF_ROUTED)
    return moe_combine(x, info, gpost, ys, dest, TM_COMBINE)


def _xattn_kernel(x_ref, k_ref, v_ref, wq_ref, wo_ref, gpre_ref, gpost_ref, o_ref):
    x = x_ref[0]
    h = _rms(x, gpre_ref[...]).astype(BF16)
    q = _dot(h, wq_ref[...])
    k = k_ref[0, 0].astype(BF16)
    v = v_ref[0, 0].astype(BF16)
    outs = []
    for hd in range(X_HEADS):
        sl = slice(hd * X_HEAD_DIM, (hd + 1) * X_HEAD_DIM)
        s = _dot_nt(q[:, sl].astype(BF16), k[:, sl]) * (X_HEAD_DIM ** -0.5)
        e = jnp.exp(s - jnp.max(s, axis=-1, keepdims=True))
        p = e / jnp.sum(e, axis=-1, keepdims=True)
        outs.append(_dot(p.astype(BF16), v[:, sl]))
    o = jnp.concatenate(outs, axis=-1).astype(BF16)
    o_ref[0] = x + _rms(_dot(o, wo_ref[...]), gpost_ref[...])


def xattn_prompt(x, k, v, l, wq, wo, gpre, gpost, tm):
    B, T, K = x.shape
    return pl.pallas_call(
        _xattn_kernel,
        grid=(B, T // tm),
        in_specs=[pl.BlockSpec((1, tm, K), lambda b, t: (b, t, 0)),
                  pl.BlockSpec((1, 1, MEM_LEN, K), lambda b, t: (l, b, 0, 0)),
                  pl.BlockSpec((1, 1, MEM_LEN, K), lambda b, t: (l, b, 0, 0)),
                  pl.BlockSpec((K, K), lambda b, t: (0, 0)),
                  pl.BlockSpec((K, K), lambda b, t: (0, 0)),
                  pl.BlockSpec((1, K), lambda b, t: (0, 0)),
                  pl.BlockSpec((1, K), lambda b, t: (0, 0))],
        out_specs=pl.BlockSpec((1, tm, K), lambda b, t: (b, t, 0)),
        out_shape=jax.ShapeDtypeStruct((B, T, K), F32),
        compiler_params=_cparams(("parallel", "parallel")),
        name="xattn_prompt",
    )(x, k, v, wq, wo, gpre.reshape(1, K), gpost.reshape(1, K))


def _xattn_sample_kernel(q_ref, k_ref, v_ref, o_ref):
    for j in range(q_ref.shape[0]):
        q = q_ref[j]
        k = k_ref[0, j]
        v = v_ref[0, j]
        s = jnp.sum(k * q[None], axis=-1, keepdims=True) * (X_HEAD_DIM ** -0.5)
        e = jnp.exp(s - jnp.max(s, axis=0, keepdims=True))
        p = e / jnp.sum(e, axis=0, keepdims=True)
        o_ref[j] = jnp.sum(p * v, axis=0)


def xattn_sample_core(q, k, v, l, nb):
    B = q.shape[0]
    kv_spec = pl.BlockSpec((1, nb, MEM_LEN, X_HEADS, X_HEAD_DIM), lambda b: (l, b, 0, 0, 0))
    return pl.pallas_call(
        _xattn_sample_kernel,
        grid=(B // nb,),
        in_specs=[pl.BlockSpec((nb, X_HEADS, X_HEAD_DIM), lambda b: (b, 0, 0)), kv_spec, kv_spec],
        out_specs=pl.BlockSpec((nb, X_HEADS, X_HEAD_DIM), lambda b: (b, 0, 0)),
        out_shape=jax.ShapeDtypeStruct((B, X_HEADS, X_HEAD_DIM), F32),
        compiler_params=_cparams(("parallel",)),
        name="xattn_sample",
    )(q, k, v)


def _ev_proj_kernel(x_ref, g_ref, w_ref, lng_ref, lnb_ref, cw_ref, cb_ref,
                    gu_ref, vln_ref, sz_ref, xa_ref, dt_ref, tail_ref, xp_ref):
    t = pl.program_id(1)
    tm = x_ref.shape[1]

    @pl.when(t == 0)
    def _():
        xp_ref[0:SUBLANES, :] = jnp.zeros((SUBLANES, B_CONV_DIM), F32)

    h = _rms(x_ref[0], g_ref[...]).astype(BF16)
    u = _dot(h, w_ref[:, 0:D])
    v = _dot(h, w_ref[:, D:2 * D])
    gu_ref[0] = jax.nn.gelu(u)
    z = _dot(h, w_ref[:, 2 * D:3 * D])
    vln_ref[0] = _layernorm(jax.nn.gelu(v), lng_ref[...], lnb_ref[...]).astype(BF16)
    x = _dot(h, w_ref[:, 3 * D:3 * D + B_CONV_DIM])
    sz_ref[0] = jax.nn.silu(z)
    dt_ref[0] = _dot(h, w_ref[:, DT_COL_BLOCK * LANES:(DT_COL_BLOCK + 1) * LANES])
    xp_ref[SUBLANES:SUBLANES + tm, :] = x
    conv = cb_ref[...] + cw_ref[B_CONV - 1:B_CONV, :] * x
    for k in range(B_CONV - 1):
        conv = conv + cw_ref[k:k + 1, :] * xp_ref[pl.ds(SUBLANES - (B_CONV - 1) + k, tm), :]
    xp_ref[0:SUBLANES, :] = x[tm - SUBLANES:tm, :]
    tail_ref[0] = x[tm - SUBLANES:tm, :]
    xa_ref[0] = jax.nn.silu(conv)


def ev_proj(x, g, w, P, tm):
    B, T, K = x.shape
    row = lambda a: a.reshape(1, -1)
    full = lambda shape: pl.BlockSpec(shape, lambda b, t: (0,) * len(shape))
    tile = lambda n: pl.BlockSpec((1, tm, n), lambda b, t: (b, t, 0))
    return pl.pallas_call(
        _ev_proj_kernel,
        grid=(B, T // tm),
        in_specs=[tile(K), full((1, K)), full((K, IN0_PAD)), full((1, D)), full((1, D)),
                  full((B_CONV, B_CONV_DIM)), full((1, B_CONV_DIM))],
        out_specs=[tile(D), tile(D), tile(D), tile(B_CONV_DIM), tile(LANES),
                   pl.BlockSpec((1, SUBLANES, B_CONV_DIM), lambda b, t: (b, 0, 0))],
        out_shape=[jax.ShapeDtypeStruct((B, T, D), F32), jax.ShapeDtypeStruct((B, T, D), BF16),
                   jax.ShapeDtypeStruct((B, T, D), F32), jax.ShapeDtypeStruct((B, T, B_CONV_DIM), F32),
                   jax.ShapeDtypeStruct((B, T, LANES), F32), jax.ShapeDtypeStruct((B, SUBLANES, B_CONV_DIM), F32)],
        scratch_shapes=[pltpu.VMEM((SUBLANES + tm, B_CONV_DIM), F32)],
        compiler_params=_cparams(("parallel", "arbitrary")),
        name="ev_proj",
    )(x, row(g), w, row(P["a_ln_g"][0]), row(P["a_ln_b"][0]), P["b_conv_w"][0], row(P["b_conv_b"][0]))


def _mixer_even_kernel(gu_ref, vln_ref, sz_ref, xa_ref, dt_ref, ws_ref, bst_ref,
                       dtb_ref, anar_ref, dexp_ref, bnorm_ref, tril_ref, expand_ref,
                       yab_ref, ssm_ref, st_ref):
    c = pl.program_id(1)
    L = A_CHUNK

    @pl.when(c == 0)
    def _():
        st_ref[...] = jnp.zeros_like(st_ref)

    row = lax.broadcasted_iota(jnp.int32, (L, L), 0)
    col = lax.broadcasted_iota(jnp.int32, (L, L), 1)
    causal = col <= row

    tril = tril_ref[...]
    lane = lax.broadcasted_iota(jnp.int32, (L, LANES), 1)

    def chunk(rows):
        gu = gu_ref[0, rows, :]
        vb = vln_ref[0, rows, :]
        a_heads = [slice(hh * LANES, (hh + 1) * LANES) for hh in range(A_HEADS)]
        mixes = [_dot(jnp.where(causal, ws_ref[hh], 0.0).astype(BF16), vb[:, sl]) for hh, sl in enumerate(a_heads)]
        for hh, sl in enumerate(a_heads):
            yab_ref[0, rows, sl] = (gu[:, sl] * (mixes[hh] + bst_ref[:, hh:hh + 1])).astype(BF16)

        xa = xa_ref[0, rows, :]
        xs = xa[:, :D]
        bm = xa[:, D:D + B_GROUPS * B_STATE]
        cm = xa[:, D + B_GROUPS * B_STATE:]

        dtf = _softplus(dt_ref[0, rows, :] + dtb_ref[...])
        cs_n = _sel_left(tril, dtf * anar_ref[...], _split3)
        dt_x = _sel_right(dtf, expand_ref[...], _split2)
        cs_x = _sel_right(cs_n, expand_ref[...], _split3)
        ecs_x = jnp.exp(cs_x)
        last_x = cs_x[L - 1:L, :]
        xdt = xs * dt_x
        xdt_b = xdt.astype(BF16)
        xdec_b = (xdt * jnp.exp(last_x - cs_x)).astype(BF16)
        groups = range(B_GROUPS)
        gsl = [slice(g * B_GROUP_W, (g + 1) * B_GROUP_W) for g in groups]
        bgs = [bm[:, g * B_STATE:(g + 1) * B_STATE] for g in groups]
        cgs = [cm[:, g * B_STATE:(g + 1) * B_STATE].astype(BF16) for g in groups]
        gmats = [_dot_nt(cgs[g], bgs[g].astype(BF16)) for g in groups]
        sts = [st_ref[g] for g in groups]
        y_offs = [ecs_x[:, gsl[g]] * _dot(cgs[g], sts[g].astype(BF16)) for g in groups]
        for g in groups:
            st_ref[g] = sts[g] * jnp.exp(last_x[:, gsl[g]]) + _dot(bgs[g].T.astype(BF16), xdec_b[:, gsl[g]])
        pairs = [(g, pair) for g in groups for pair in range(B_GROUP_W // LANES)]
        bases = [g * B_GROUP_W + pair * LANES for g, pair in pairs]
        cs_ts = [cs_x[:, base:base + LANES].T for base in bases]
        lhs = []
        for (g, _), base, cs_t in zip(pairs, bases, cs_ts):
            ms = []
            for half in range(2):
                ch = half * B_HEAD_DIM
                diff = cs_x[:, base + ch:base + ch + 1] - cs_t[ch:ch + 1, :]
                ms.append((gmats[g] * jnp.exp(jnp.where(causal, diff, -jnp.inf))).astype(BF16))
            lhs.append(jnp.concatenate(ms, axis=1))
        y_parts = []
        for (g, pair), base, m2 in zip(pairs, bases, lhs):
            xp2 = xdt_b[:, base:base + LANES]
            rhs = jnp.concatenate([jnp.where(lane < B_HEAD_DIM, xp2, jnp.zeros_like(xp2)),
                                   jnp.where(lane >= B_HEAD_DIM, xp2, jnp.zeros_like(xp2))], axis=0)
            y_parts.append(_dot(m2, rhs) + y_offs[g][:, pair * LANES:(pair + 1) * LANES])
        y = jnp.concatenate(y_parts, axis=-1) + dexp_ref[...] * xs
        yb = _group_rms(y * sz_ref[0, rows, :], bnorm_ref[...], B_GROUP_W)
        yab_ref[0, rows, D:] = yb.astype(BF16)

    for sub in range(gu_ref.shape[1] // L):
        chunk(slice(sub * L, (sub + 1) * L))

    @pl.when(c == pl.num_programs(1) - 1)
    def _():
        for g in range(B_GROUPS):
            ssm_ref[0, g] = st_ref[g].T


def _even_consts(P):
    head_of_ch = jnp.arange(D) // B_HEAD_DIM
    expand = (jnp.arange(LANES)[:, None] == head_of_ch[None, :]).astype(BF16)
    tril = (jnp.arange(A_CHUNK)[None, :] <= jnp.arange(A_CHUNK)[:, None]).astype(BF16)
    aexp = jnp.repeat(-jnp.exp(P["b_a_log"][0].astype(F32)), B_HEAD_DIM).reshape(1, D)
    dexp = jnp.repeat(P["b_d"][0].astype(F32), B_HEAD_DIM).reshape(1, D)
    dtb = jnp.pad(P["b_dt_bias"][0].astype(F32), (0, LANES - B_HEADS)).reshape(1, LANES)
    return expand, tril, aexp, dexp, dtb


def mixer_even_prompt(gu, vln, sz, xa, dt, P):
    B, T, _ = gu.shape
    L = A_CHUNK
    expand, tril, _, dexp, dtb = _even_consts(P)
    anar = jnp.pad(-jnp.exp(P["b_a_log"][0].astype(F32)), (0, LANES - B_HEADS)).reshape(1, LANES)
    row = lambda a: a.reshape(1, -1)
    full = lambda shape: pl.BlockSpec(shape, lambda b, c: (0,) * len(shape))
    R = A_ROWS
    chunk = lambda n: pl.BlockSpec((1, R, n), lambda b, c: (b, c, 0))
    return pl.pallas_call(
        _mixer_even_kernel,
        grid=(B, T // R),
        in_specs=[chunk(D), chunk(D), chunk(D), chunk(B_CONV_DIM), chunk(LANES),
                  full((A_HEADS, L, L)), full((L, A_HEADS)), full((1, LANES)), full((1, LANES)), full((1, D)),
                  full((1, D)), full((L, L)), full((LANES, D))],
        out_specs=[pl.BlockSpec((1, R, 2 * D), lambda b, c: (b, c, 0)),
                   pl.BlockSpec((1, B_GROUPS, B_GROUP_W, B_STATE), lambda b, c: (b, 0, 0, 0))],
        out_shape=[jax.ShapeDtypeStruct((B, T, 2 * D), BF16),
                   jax.ShapeDtypeStruct((B, B_GROUPS, B_GROUP_W, B_STATE), F32)],
        scratch_shapes=[pltpu.VMEM((B_GROUPS, B_STATE, B_GROUP_W), F32)],
        compiler_params=_cparams(("parallel", "arbitrary")),
        name="mixer_even_prompt",
    )(gu, vln, sz, xa, dt, P["a_ws"][0], P["a_bs"][0].T, dtb, anar, dexp, row(P["b_norm"][0]), tril, expand)


def _mixer_even_step_kernel(proj_ref, conv_ref, ssm_ref, ws0_ref, bs0_ref, lng_ref, lnb_ref, cw_ref, cb_ref,
                            dtb_ref, aexp_ref, dexp_ref, bnorm_ref, expand_ref,
                            yab_ref, av_ref, convo_ref, ssmo_ref):
    nb = proj_ref.shape[0]
    u = proj_ref[:, 0:D]
    v = proj_ref[:, D:2 * D]
    z = proj_ref[:, 2 * D:3 * D]
    x = proj_ref[:, 3 * D:3 * D + B_CONV_DIM]
    dt = proj_ref[:, DT_COL_BLOCK * LANES:(DT_COL_BLOCK + 1) * LANES]

    vln = _layernorm(jax.nn.gelu(v), lng_ref[...], lnb_ref[...])
    av_ref[...] = vln
    yab_ref[:, 0:D] = jax.nn.gelu(u) * (ws0_ref[...] * vln + bs0_ref[...])

    conv = cb_ref[...] + cw_ref[B_CONV - 1:B_CONV, :] * x
    for k in range(B_CONV - 1):
        conv = conv + cw_ref[k:k + 1, :] * conv_ref[k]
        if k > 0:
            convo_ref[k - 1] = conv_ref[k]
    convo_ref[B_CONV - 2] = x
    xa = jax.nn.silu(conv)
    xs = xa[:, :D]
    bm = xa[:, D:D + B_GROUPS * B_STATE]
    cm = xa[:, D + B_GROUPS * B_STATE:]
    dtf = _softplus(dt + dtb_ref[...])
    dt_x = _sel_right(dtf, expand_ref[...], _split3)
    dec_x = jnp.exp(dt_x * aexp_ref[...])
    xdt = xs * dt_x
    y_rows = []
    for g in range(B_GROUPS):
        gs = slice(g * B_GROUP_W, (g + 1) * B_GROUP_W)
        dec_t = _rows_to_cols(dec_x[:, gs])
        xdt_t = _rows_to_cols(xdt[:, gs])
        s_news = [ssm_ref[j, g] * dec_t[:, j:j + 1] + xdt_t[:, j:j + 1] * bm[j:j + 1, g * B_STATE:(g + 1) * B_STATE]
                  for j in range(nb)]
        for j in range(nb):
            ssmo_ref[j, g] = s_news[j]
        cjs = [jnp.broadcast_to(cm[j:j + 1, g * B_STATE:(g + 1) * B_STATE], (SUBLANES, B_STATE)) for j in range(nb)]
        y_rows.append(jnp.concatenate([_dot_nt_f32(cjs[j], s_news[j])[0:1, :] for j in range(nb)], axis=0))
    y = jnp.concatenate(y_rows, axis=-1) + dexp_ref[...] * xs
    yb = _group_rms(y * jax.nn.silu(z), bnorm_ref[...], B_GROUP_W)
    yab_ref[:, D:] = yb


def mixer_even_step(proj, conv0, ssm0, P, nb):
    B = proj.shape[0]
    expand, _, aexp, dexp, dtb = _even_consts(P)
    row = lambda a: a.reshape(1, -1)
    rep = lambda a: jnp.repeat(a.astype(F32), LANES).reshape(1, D)
    full = lambda shape: pl.BlockSpec(shape, lambda i: (0,) * len(shape))
    return pl.pallas_call(
        _mixer_even_step_kernel,
        grid=(B // nb,),
        in_specs=[pl.BlockSpec((nb, proj.shape[1]), lambda i: (i, 0)),
                  pl.BlockSpec((B_CONV - 1, nb, B_CONV_DIM), lambda i: (0, i, 0)),
                  pl.BlockSpec((nb, B_GROUPS, B_GROUP_W, B_STATE), lambda i: (i, 0, 0, 0)),
                  full((1, D)), full((1, D)), full((1, D)), full((1, D)),
                  full((B_CONV, B_CONV_DIM)), full((1, B_CONV_DIM)), full((1, LANES)), full((1, D)), full((1, D)),
                  full((1, D)), full((LANES, D))],
        out_specs=[pl.BlockSpec((nb, 2 * D), lambda i: (i, 0)),
                   pl.BlockSpec((nb, D), lambda i: (i, 0)),
                   pl.BlockSpec((B_CONV - 1, nb, B_CONV_DIM), lambda i: (0, i, 0)),
                   pl.BlockSpec((nb, B_GROUPS, B_GROUP_W, B_STATE), lambda i: (i, 0, 0, 0))],
        out_shape=[jax.ShapeDtypeStruct((B, 2 * D), F32),
                   jax.ShapeDtypeStruct((B, D), F32),
                   jax.ShapeDtypeStruct((B_CONV - 1, B, B_CONV_DIM), F32),
                   jax.ShapeDtypeStruct((B, B_GROUPS, B_GROUP_W, B_STATE), F32)],
        compiler_params=_cparams(("parallel",)),
        name="mixer_even_step",
    )(proj, conv0, ssm0, rep(P["a_ws"][0][:, 0, 0]), rep(P["a_bs"][0][:, 0]), row(P["a_ln_g"][0]), row(P["a_ln_b"][0]),
      P["b_conv_w"][0], row(P["b_conv_b"][0]), dtb, aexp, dexp, row(P["b_norm"][0]), expand)


def _gla_gates(q_raw, f_raw, lb):
    fg = lb + (1.0 - lb) * jax.nn.sigmoid(f_raw)
    return jax.nn.silu(q_raw), fg, 1.0 - fg


def _hgrn_kernel(q_ref, f_ref, i_ref, g_ref, lb_ref, cnorm_ref, tril_ref, o_ref, s_ref, st_ref):
    c = pl.program_id(1)
    L = C_CHUNK
    R = C_ROWS

    @pl.when(c == 0)
    def _():
        st_ref[...] = jnp.zeros_like(st_ref)

    row = lax.broadcasted_iota(jnp.int32, (R, R), 0)
    col = lax.broadcasted_iota(jnp.int32, (R, R), 1)
    causal = (col <= row) & (row // L == col // L)
    chunk_of_row = lax.broadcasted_iota(jnp.int32, (R, 1), 0) // L
    tril = tril_ref[...]
    heads = [slice(hh * C_KDIM, (hh + 1) * C_KDIM) for hh in range(C_HEADS)]

    def block(rows):
        q, fg, k = _gla_gates(q_ref[0, rows, :], f_ref[0, rows, :], lb_ref[...])
        v = i_ref[0, rows, :]
        bc = _sel_left(tril, jnp.log(fg), _split3)
        q_in = (q * jnp.exp(bc)).astype(BF16)
        k_in = (k * jnp.exp(-bc)).astype(BF16)
        vb = v.astype(BF16)
        btots, k_decs = [], []
        for s in range(R // L):
            btot = bc[(s + 1) * L - 1:(s + 1) * L, :]
            btots.append(btot)
            k_decs.append(jnp.where(chunk_of_row == s, k * jnp.exp(btot - bc), 0.0).astype(BF16))
        atts = [jnp.where(causal, _dot_nt(q_in[:, sl], k_in[:, sl]), 0.0).astype(BF16) for sl in heads]
        v_ts = [v[:, sl].T.astype(BF16) for sl in heads]
        sts = [st_ref[hh] for hh in range(C_HEADS)]
        inters = [[] for _ in heads]
        for s in range(R // L):
            for hh, sl in enumerate(heads):
                inters[hh].append(_dot_nt(q_in[s * L:(s + 1) * L, sl], sts[hh].astype(BF16)))
            sts = [sts[hh] * jnp.exp(btots[s][:, sl]) + _dot(v_ts[hh], k_decs[s][:, sl])
                   for hh, sl in enumerate(heads)]
        for hh in range(C_HEADS):
            st_ref[hh] = sts[hh]
        outs = [_dot(atts[hh], vb[:, sl]) + jnp.concatenate(inters[hh], axis=0) for hh, sl in enumerate(heads)]
        o = _group_rms(jnp.concatenate(outs, axis=-1), cnorm_ref[...], C_KDIM)
        o_ref[0, rows, :] = (o * jax.nn.silu(g_ref[0, rows, :])).astype(BF16)

    for blk in range(q_ref.shape[1] // R):
        block(slice(blk * R, (blk + 1) * R))

    @pl.when(c == pl.num_programs(1) - 1)
    def _():
        for hh in range(C_HEADS):
            s_ref[0, hh] = st_ref[hh].T


def hgrn_prompt(proj, lb, cnorm):
    B, T, _ = proj.shape
    L = C_ROWS
    r = jnp.arange(L)
    tril = ((r[None, :] <= r[:, None]) & (r[None, :] // C_CHUNK == r[:, None] // C_CHUNK)).astype(BF16)
    full = lambda shape: pl.BlockSpec(shape, lambda b, c: (0,) * len(shape))
    S = C_STEP_ROWS
    return pl.pallas_call(
        _hgrn_kernel,
        grid=(B, T // S),
        in_specs=[pl.BlockSpec((1, S, D), lambda b, c: (b, c, 0)),
                  pl.BlockSpec((1, S, D), lambda b, c: (b, c, 1)),
                  pl.BlockSpec((1, S, D), lambda b, c: (b, c, 2)),
                  pl.BlockSpec((1, S, D), lambda b, c: (b, c, 3)),
                  full((1, D)), full((1, D)), full((L, L))],
        out_specs=[pl.BlockSpec((1, S, D), lambda b, c: (b, c, 0)),
                   pl.BlockSpec((1, C_HEADS, C_KDIM, C_KDIM), lambda b, c: (b, 0, 0, 0))],
        out_shape=[jax.ShapeDtypeStruct((B, T, D), BF16),
                   jax.ShapeDtypeStruct((B, C_HEADS, C_KDIM, C_KDIM), F32)],
        scratch_shapes=[pltpu.VMEM((C_HEADS, C_KDIM, C_KDIM), F32)],
        compiler_params=_cparams(("parallel", "arbitrary")),
        name="hgrn_prompt",
    )(proj, proj, proj, proj, lb.reshape(1, D), cnorm.reshape(1, D), tril)


def _hgrn_step_kernel(proj_ref, s_ref, lb_ref, cnorm_ref, o_ref, so_ref):
    nb = proj_ref.shape[0]
    q, fg, k = _gla_gates(proj_ref[:, 0:D], proj_ref[:, D:2 * D], lb_ref[...])
    v = proj_ref[:, 2 * D:3 * D]
    g = proj_ref[:, 3 * D:4 * D]
    heads = [slice(hh * C_KDIM, (hh + 1) * C_KDIM) for hh in range(C_HEADS)]
    fg_ts = [_rows_to_cols(fg[:, sl]) for sl in heads]
    k_ts = [_rows_to_cols(k[:, sl]) for sl in heads]
    outs = []
    for hh, sl in enumerate(heads):
        s_news = [s_ref[j, hh] * fg_ts[hh][:, j:j + 1] + k_ts[hh][:, j:j + 1] * v[j:j + 1, sl] for j in range(nb)]
        for j in range(nb):
            so_ref[j, hh] = s_news[j]
        qjs = [jnp.broadcast_to(q[j:j + 1, sl], (SUBLANES, C_KDIM)) for j in range(nb)]
        outs.append(jnp.concatenate([_dot_f32(qjs[j], s_news[j])[0:1, :] for j in range(nb)], axis=0))
    o = _group_rms(jnp.concatenate(outs, axis=-1), cnorm_ref[...], C_KDIM)
    o_ref[...] = o * jax.nn.silu(g)


def hgrn_step(proj, s0, lb, cnorm, nb):
    B = proj.shape[0]
    full = lambda shape: pl.BlockSpec(shape, lambda i: (0,) * len(shape))
    return pl.pallas_call(
        _hgrn_step_kernel,
        grid=(B // nb,),
        in_specs=[pl.BlockSpec((nb, 4 * D), lambda i: (i, 0)),
                  pl.BlockSpec((nb, C_HEADS, C_KDIM, C_KDIM), lambda i: (i, 0, 0, 0)),
                  full((1, D)), full((1, D))],
        out_specs=[pl.BlockSpec((nb, D), lambda i: (i, 0)),
                   pl.BlockSpec((nb, C_HEADS, C_KDIM, C_KDIM), lambda i: (i, 0, 0, 0))],
        out_shape=[jax.ShapeDtypeStruct((B, D), F32),
                   jax.ShapeDtypeStruct((B, C_HEADS, C_KDIM, C_KDIM), F32)],
        compiler_params=_cparams(("parallel",)),
        name="hgrn_step",
    )(proj, s0, lb.reshape(1, D), cnorm.reshape(1, D))


TM_PROJ = 512
TM_EV_PROJ = 256
TM_OUT = 512
TM_ATTN = 512
TM_FFN = 1024
TF_DENSE = 256
TF_MOE = 896
TF_ROUTED = 512
TG_MOE = 1024
TM_ROUTE = 512
TM_DISPATCH = 1024
TM_COMBINE = 512
DMA_UNROLL = 8
ROUTED_MIN_TOKENS = 8 * TG_MOE
STEP_NB = 8
XATTN_STEP_NB = 4
TN_STEP = 512
TK_STEP = 512
IN0_STEP_PAD = 5120


def _prep_weights(P):
    W = {}
    W["ev_w_in"] = jnp.pad(P["ev_w_in"][0], ((0, 0), (0, IN0_PAD - IN0))).astype(BF16)
    W["ev_w_out"] = P["ev_w_out"][0].astype(BF16)
    W["od_w_in"] = P["od_w_in"][0].astype(BF16)
    W["od_w_out"] = P["od_w_out"][0].astype(BF16)
    W["xa_wq"] = P["xa_wq"].astype(BF16)
    W["xa_wo"] = P["xa_wo"].astype(BF16)
    W["xa_wk"] = P["xa_wk"].astype(BF16)
    W["xa_wv"] = P["xa_wv"].astype(BF16)
    W["ffn"] = tuple(P[n].astype(BF16) for n in ("ffn_w_gate", "ffn_w_up", "ffn_w_down"))
    W["moe"] = tuple(P[n][0] for n in ("moe_w_gate", "moe_w_up", "moe_w_down"))
    W["router"] = jnp.pad(P["moe_router"][0].astype(F32), ((0, 0), (0, LANES - N_EXPERTS)))
    lbp = jax.nn.softmax(P["hgrn_lb_logits"].astype(F32), axis=0)
    W["lower_bounds"] = jnp.cumsum(lbp, axis=0) - lbp[0]
    return W


def _channel_mix(x2, l, P, W, tm):
    if l == 0:
        wg, wu, wd = W["ffn"]
        return ffn(x2, P["norm_ffn_pre"][l], P["norm_ffn_post"][l], W["router"], wg, wu, wd, tm, TF_DENSE, False)
    wg, wu, wd = W["moe"]
    if x2.shape[0] >= ROUTED_MIN_TOKENS:
        return moe_routed(x2, P["norm_ffn_pre"][l], P["norm_ffn_post"][l], W["router"], wg, wu, wd)
    return ffn(x2, P["norm_ffn_pre"][l], P["norm_ffn_post"][l], W["router"], wg, wu, wd, tm, TF_MOE, True)


def _trunk_prompt(x, mem_k, mem_v, P, W):
    B, T, _ = x.shape
    M = B * T
    x2 = x.reshape(M, D)
    gu, vln, sz, xa, dt, tail = ev_proj(x, P["norm_mix_pre"][0], W["ev_w_in"], P, TM_EV_PROJ)
    yab, ssm = mixer_even_prompt(gu, vln, sz, xa, dt, P)
    conv = tail[:, SUBLANES - (B_CONV - 1):, :]
    x2 = matmul_norm_res(yab.reshape(M, 2 * D), W["ev_w_out"], P["norm_mix_post"][0], x2, TM_OUT)
    x2 = xattn_prompt(x2.reshape(B, T, D), mem_k, mem_v, 0, W["xa_wq"][0], W["xa_wo"][0],
                      P["norm_x_pre"][0], P["norm_x_post"][0], TM_ATTN).reshape(M, D)
    x2 = _channel_mix(x2, 0, P, W, TM_FFN)
    proj = norm_matmul(x2, P["norm_mix_pre"][1], W["od_w_in"], TM_PROJ).reshape(B, T, 4 * D)
    o, hgrn = hgrn_prompt(proj, W["lower_bounds"][1], P["c_norm"][0])
    x2 = matmul_norm_res(o.reshape(M, D), W["od_w_out"], P["norm_mix_post"][1], x2, TM_OUT)
    x2 = xattn_prompt(x2.reshape(B, T, D), mem_k, mem_v, 1, W["xa_wq"][1], W["xa_wo"][1],
                      P["norm_x_pre"][1], P["norm_x_post"][1], TM_ATTN).reshape(M, D)
    x2 = _channel_mix(x2, 1, P, W, TM_FFN)
    return x2.reshape(B, T, D), conv, ssm, hgrn


def _xattn_step(x2, l, mem_k, mem_v, P):
    B = x2.shape[0]
    q = norm_matmul(x2, P["norm_x_pre"][l], P["xa_wq"][l], B, TN_STEP)
    o = xattn_sample_core(q.reshape(B, X_HEADS, X_HEAD_DIM), mem_k, mem_v, l, XATTN_STEP_NB).reshape(B, D)
    return matmul_norm_res(o, P["xa_wo"][l], P["norm_x_post"][l], x2, B)


def _trunk_step(x, mem_k, mem_v, conv0, ssm0, hgrn0, P, W):
    B = x.shape[0]
    x2 = x.reshape(B, D)
    w_in = jnp.pad(P["ev_w_in"][0], ((0, 0), (0, IN0_STEP_PAD - IN0)))
    proj = norm_matmul(x2, P["norm_mix_pre"][0], w_in, B, TN_STEP)
    yab, av, conv, ssm = mixer_even_step(proj, jnp.swapaxes(conv0, 0, 1), ssm0, P, STEP_NB)
    x2 = matmul_norm_res(yab, P["ev_w_out"][0], P["norm_mix_post"][0], x2, B, TK_STEP)
    x2 = _xattn_step(x2, 0, mem_k, mem_v, P)
    x2 = ffn(x2, P["norm_ffn_pre"][0], P["norm_ffn_post"][0], W["router"],
             P["ffn_w_gate"], P["ffn_w_up"], P["ffn_w_down"], B, TF_DENSE, False, precise=True)
    proj = norm_matmul(x2, P["norm_mix_pre"][1], P["od_w_in"][0], B, TN_STEP)
    o, hgrn = hgrn_step(proj, hgrn0, W["lower_bounds"][1], P["c_norm"][0], STEP_NB)
    x2 = matmul_norm_res(o, P["od_w_out"][0], P["norm_mix_post"][1], x2, B)
    x2 = _xattn_step(x2, 1, mem_k, mem_v, P)
    x2 = _channel_mix(x2, 1, P, W, B)
    return x2.reshape(B, 1, D), jnp.swapaxes(conv, 0, 1), ssm, hgrn, av


def kernel(x_prompt, x_sample, mem_prompt, cache_mem_k, cache_mem_v, state_conv, state_ssm, state_hgrn,
           norm_mix_pre, norm_mix_post, norm_x_pre, norm_x_post, norm_ffn_pre, norm_ffn_post, norm_mem,
           xa_wq, xa_wk, xa_wv, xa_wo,
           ev_w_in, a_ws, a_bs, a_ln_g, a_ln_b, b_conv_w, b_conv_b, b_dt_bias, b_a_log, b_d, b_norm, ev_w_out,
           ffn_w_gate, ffn_w_up, ffn_w_down,
           od_w_in, hgrn_lb_logits, c_norm, od_w_out,
           moe_router, moe_w_gate, moe_w_up, moe_w_down):
    P = dict(norm_mix_pre=norm_mix_pre, norm_mix_post=norm_mix_post, norm_x_pre=norm_x_pre, norm_x_post=norm_x_post,
             norm_ffn_pre=norm_ffn_pre, norm_ffn_post=norm_ffn_post, xa_wq=xa_wq, xa_wk=xa_wk, xa_wv=xa_wv,
             xa_wo=xa_wo, ev_w_in=ev_w_in, a_ws=a_ws, a_bs=a_bs, a_ln_g=a_ln_g, a_ln_b=a_ln_b, b_conv_w=b_conv_w,
             b_conv_b=b_conv_b, b_dt_bias=b_dt_bias, b_a_log=b_a_log, b_d=b_d, b_norm=b_norm, ev_w_out=ev_w_out,
             ffn_w_gate=ffn_w_gate, ffn_w_up=ffn_w_up, ffn_w_down=ffn_w_down, od_w_in=od_w_in,
             hgrn_lb_logits=hgrn_lb_logits, c_norm=c_norm, od_w_out=od_w_out, moe_router=moe_router,
             moe_w_gate=moe_w_gate, moe_w_up=moe_w_up, moe_w_down=moe_w_down)
    W = _prep_weights(P)
    depth = norm_mem.shape[0]
    bp, T, _ = x_prompt.shape
    bs = x_sample.shape[0]

    mem_k_p, mem_v_p = mem_kv(mem_prompt.reshape(bp * MEM_LEN, D), norm_mem, W["xa_wk"], W["xa_wv"], TM_PROJ)
    mem_k_p = mem_k_p.reshape(depth, bp, MEM_LEN, D)
    mem_v_p = mem_v_p.reshape(depth, bp, MEM_LEN, D)
    y_p, conv_p, ssm_p, hgrn_p = _trunk_prompt(x_prompt, mem_k_p, mem_v_p, P, W)

    y_s, conv_s, ssm_s, hgrn_s, av_s = _trunk_step(
        x_sample, cache_mem_k, cache_mem_v,
        state_conv[0], state_ssm[0].reshape(bs, B_GROUPS, B_GROUP_W, B_STATE), state_hgrn[0], P, W)

    kv_shape = (depth, bp, MEM_LEN, X_HEADS, X_HEAD_DIM)
    ssm_shape = (B_GROUPS, B_GROUP_W // B_HEAD_DIM, B_HEAD_DIM, B_STATE)
    return (y_p, y_s, mem_k_p.reshape(kv_shape), mem_v_p.reshape(kv_shape),
            conv_p[None], ssm_p.reshape((1, bp) + ssm_shape), hgrn_p[None],
            conv_s[None], ssm_s.reshape((1, bs) + ssm_shape), hgrn_s[None], av_s.reshape(1, bs, 1, D))
```

```python
import functools

import jax
import jax.numpy as jnp
from jax import lax
from jax.experimental import pallas as pl
from jax.experimental.pallas import tpu as pltpu

F32 = jnp.float32
BF16 = jnp.bfloat16
EPS = 1e-6

D = 1024
LANES = 128
SUBLANES = 8
A_HEADS = 8
A_CHUNK = 128
A_ROWS = 512
B_HEADS = 16
B_HEAD_DIM = 64
B_GROUPS = 2
B_GROUP_W = 512
B_STATE = 128
B_CONV = 4
B_CONV_DIM = 1536
IN0 = 4624
IN0_PAD = 4736
DT_COL_BLOCK = 36
C_HEADS = 8
C_KDIM = 128
C_CHUNK = 64
C_ROWS = 128
C_STEP_ROWS = 512
X_HEADS = 4
X_HEAD_DIM = 256
MEM_LEN = 256
N_EXPERTS = 8

VMEM_LIMIT = 56 * 1024 * 1024


def _cparams(sem):
    return pltpu.CompilerParams(dimension_semantics=sem, vmem_limit_bytes=VMEM_LIMIT)


def _dot(a, b):
    return jnp.dot(a, b, preferred_element_type=F32)


def _dot_nt(a, b):
    return lax.dot_general(a, b, (((1,), (1,)), ((), ())), preferred_element_type=F32)


def _dot_f32(a, b):
    return jnp.dot(a, b, precision=lax.Precision.HIGHEST, preferred_element_type=F32)


def _dot_nt_f32(a, b):
    return lax.dot_general(a, b, (((1,), (1,)), ((), ())), precision=lax.Precision.HIGHEST,
                           preferred_element_type=F32)


def _mm(a, w):
    if w.dtype == F32:
        return _dot_f32(a.astype(F32), w)
    return _dot(a.astype(BF16), w)


def _rms(x, g):
    return x * lax.rsqrt(jnp.mean(x * x, axis=-1, keepdims=True) + EPS) * g


def _split2(x):
    hi = x.astype(BF16)
    lo = (x - hi.astype(F32)).astype(BF16)
    return hi, lo


def _split3(x):
    hi = x.astype(BF16)
    r = x - hi.astype(F32)
    mid = r.astype(BF16)
    lo = (r - mid.astype(F32)).astype(BF16)
    return hi, mid, lo


def _sel_left(m, x, parts):
    out = None
    for p in parts(x):
        t = _dot(m, p)
        out = t if out is None else out + t
    return out


def _sel_right(x, m, parts):
    out = None
    for p in parts(x):
        t = _dot(p, m)
        out = t if out is None else out + t
    return out


def _rows_to_cols(x):
    n, w = x.shape
    if n < LANES:
        x = jnp.concatenate([x, jnp.zeros((LANES - n, w), x.dtype)], axis=0)
    return x.T


def _softplus(x):
    return jnp.maximum(x, 0.0) + jnp.log1p(jnp.exp(-jnp.abs(x)))


def _layernorm(x, g, b):
    xc = x - jnp.mean(x, axis=-1, keepdims=True)
    return xc * lax.rsqrt(jnp.mean(xc * xc, axis=-1, keepdims=True) + EPS) * g + b


def _group_rms(x, g, width):
    parts = []
    for s in range(0, x.shape[-1], width):
        t = x[:, s:s + width]
        parts.append(t * lax.rsqrt(jnp.mean(t * t, axis=-1, keepdims=True) + EPS))
    return jnp.concatenate(parts, axis=-1) * g


def _norm_matmul_kernel(x_ref, g_ref, w_ref, o_ref):
    o_ref[...] = _mm(_rms(x_ref[...], g_ref[...]), w_ref[...])


def norm_matmul(x, g, w, tm, tn=None):
    M, K = x.shape
    N = w.shape[1]
    tn = N if tn is None else tn
    return pl.pallas_call(
        _norm_matmul_kernel,
        grid=(M // tm, N // tn),
        in_specs=[pl.BlockSpec((tm, K), lambda i, j: (i, 0)),
                  pl.BlockSpec((1, K), lambda i, j: (0, 0)),
                  pl.BlockSpec((K, tn), lambda i, j: (0, j))],
        out_specs=pl.BlockSpec((tm, tn), lambda i, j: (i, j)),
        out_shape=jax.ShapeDtypeStruct((M, N), F32),
        compiler_params=_cparams(("parallel", "parallel")),
        name="norm_matmul",
    )(x, g.reshape(1, K), w)


def _mem_kv_kernel(x_ref, g_ref, wk_ref, wv_ref, k_ref, v_ref, k5_ref, v5_ref):
    h = _rms(x_ref[...], g_ref[0]).astype(BF16)
    for w_ref, o_ref, o5_ref in ((wk_ref, k_ref, k5_ref), (wv_ref, v_ref, v5_ref)):
        r = _dot(h, w_ref[0])
        o_ref[0] = r
        for hd in range(X_HEADS):
            o5_ref[0, :, hd, :] = r[:, hd * X_HEAD_DIM:(hd + 1) * X_HEAD_DIM]


def mem_kv(mem, g, wk, wv, tm):
    M, K = mem.shape
    depth = g.shape[0]
    wspec = pl.BlockSpec((1, K, K), lambda l, i: (l, 0, 0))
    ospec = pl.BlockSpec((1, tm, K), lambda l, i: (l, i, 0))
    o5spec = pl.BlockSpec((1, tm, X_HEADS, X_HEAD_DIM), lambda l, i: (l, i, 0, 0))
    return pl.pallas_call(
        _mem_kv_kernel,
        grid=(depth, M // tm),
        in_specs=[pl.BlockSpec((tm, K), lambda l, i: (i, 0)),
                  pl.BlockSpec((1, 1, K), lambda l, i: (l, 0, 0)), wspec, wspec],
        out_specs=[ospec, ospec, o5spec, o5spec],
        out_shape=[jax.ShapeDtypeStruct((depth, M, K), F32)] * 2
        + [jax.ShapeDtypeStruct((depth, M, X_HEADS, X_HEAD_DIM), F32)] * 2,
        compiler_params=_cparams(("parallel", "parallel")),
        name="mem_kv",
    )(mem, g.reshape(depth, 1, K), wk, wv)


def _matmul_norm_res_kernel(a_ref, w_ref, g_ref, r_ref, o_ref, acc_ref):
    k = pl.program_id(1)

    @pl.when(k == 0)
    def _():
        acc_ref[...] = jnp.zeros_like(acc_ref)

    acc_ref[...] += _mm(a_ref[...], w_ref[...])

    @pl.when(k == pl.num_programs(1) - 1)
    def _():
        o_ref[...] = r_ref[...] + _rms(acc_ref[...], g_ref[...])


def _matmul_norm_res_whole_k_kernel(a_ref, w_ref, g_ref, r_ref, o_ref):
    o_ref[...] = r_ref[...] + _rms(_mm(a_ref[...], w_ref[...]), g_ref[...])


def matmul_norm_res(a, w, g, res, tm, tk=None):
    M, K = a.shape
    N = w.shape[1]
    tk = K if tk is None else tk
    if tk == K:
        return pl.pallas_call(
            _matmul_norm_res_whole_k_kernel,
            grid=(M // tm,),
            in_specs=[pl.BlockSpec((tm, K), lambda i: (i, 0)),
                      pl.BlockSpec((K, N), lambda i: (0, 0)),
                      pl.BlockSpec((1, N), lambda i: (0, 0)),
                      pl.BlockSpec((tm, N), lambda i: (i, 0))],
            out_specs=pl.BlockSpec((tm, N), lambda i: (i, 0)),
            out_shape=jax.ShapeDtypeStruct((M, N), F32),
            compiler_params=_cparams(("parallel",)),
            name="matmul_norm_res",
        )(a, w, g.reshape(1, N), res)
    return pl.pallas_call(
        _matmul_norm_res_kernel,
        grid=(M // tm, K // tk),
        in_specs=[pl.BlockSpec((tm, tk), lambda i, k: (i, k)),
                  pl.BlockSpec((tk, N), lambda i, k: (k, 0)),
                  pl.BlockSpec((1, N), lambda i, k: (0, 0)),
                  pl.BlockSpec((tm, N), lambda i, k: (i, 0))],
        out_specs=pl.BlockSpec((tm, N), lambda i, k: (i, 0)),
        out_shape=jax.ShapeDtypeStruct((M, N), F32),
        scratch_shapes=[pltpu.VMEM((tm, N), F32)],
        compiler_params=_cparams(("parallel", "arbitrary")),
        name="matmul_norm_res",
    )(a, w, g.reshape(1, N), res)


def _top2(h, router, three_pass=False):
    if three_pass:
        hh, hl = _split2(h)
        rh, rl = _split2(router)
        lg = _dot(hh, rh) + _dot(hh, rl) + _dot(hl, rh)
    else:
        lg = _dot_f32(h, router)
    lane = lax.broadcasted_iota(jnp.int32, lg.shape, 1)
    lg = jnp.where(lane < N_EXPERTS, lg, -jnp.inf)
    m1 = jnp.max(lg, axis=-1, keepdims=True)
    i1 = jnp.min(jnp.where(lg == m1, lane, LANES), axis=-1, keepdims=True)
    lg2 = jnp.where(lane == i1, -jnp.inf, lg)
    m2 = jnp.max(lg2, axis=-1, keepdims=True)
    i2 = jnp.min(jnp.where(lg2 == m2, lane, LANES), axis=-1, keepdims=True)
    e2 = jnp.exp(m2 - m1)
    den = 1.0 + e2
    return lane, i1, i2, 1.0 / den, e2 / den


def _top2_gates(h, router):
    lane, i1, i2, w1, w2 = _top2(h, router)
    return jnp.where(lane == i1, w1, 0.0) + jnp.where(lane == i2, w2, 0.0)


def _ffn_kernel(x_ref, gpre_ref, gpost_ref, router_ref, wg_ref, wu_ref, wd_ref, o_ref,
                h_ref, acc_ref, gates_ref, *, moe):
    e = pl.program_id(1)
    f = pl.program_id(2)

    @pl.when((e == 0) & (f == 0))
    def _():
        h = _rms(x_ref[...], gpre_ref[...])
        h_ref[...] = h.astype(h_ref.dtype)
        acc_ref[...] = jnp.zeros_like(acc_ref)
        if moe:
            gates_ref[...] = _top2_gates(h, router_ref[...])
        else:
            gates_ref[...] = jnp.zeros_like(gates_ref)

    hb = h_ref[...]
    act = jax.nn.silu(_mm(hb, wg_ref[0].astype(hb.dtype))) * _mm(hb, wu_ref[0].astype(hb.dtype))
    if moe:
        gates = gates_ref[...]
        lane = lax.broadcasted_iota(jnp.int32, gates.shape, 1)
        act = act * jnp.sum(jnp.where(lane == e, gates, 0.0), axis=-1, keepdims=True)
    acc_ref[...] += _mm(act, wd_ref[0].astype(hb.dtype))

    @pl.when((e == pl.num_programs(1) - 1) & (f == pl.num_programs(2) - 1))
    def _():
        o_ref[...] = x_ref[...] + _rms(acc_ref[...], gpost_ref[...])


def ffn(x, gpre, gpost, router, wg, wu, wd, tm, tf, moe, precise=False):
    M, K = x.shape
    E, _, F = wg.shape
    return pl.pallas_call(
        functools.partial(_ffn_kernel, moe=moe),
        grid=(M // tm, E, F // tf),
        in_specs=[pl.BlockSpec((tm, K), lambda i, e, f: (i, 0)),
                  pl.BlockSpec((1, K), lambda i, e, f: (0, 0)),
                  pl.BlockSpec((1, K), lambda i, e, f: (0, 0)),
                  pl.BlockSpec((K, LANES), lambda i, e, f: (0, 0)),
                  pl.BlockSpec((1, K, tf), lambda i, e, f: (e, 0, f)),
                  pl.BlockSpec((1, K, tf), lambda i, e, f: (e, 0, f)),
                  pl.BlockSpec((1, tf, K), lambda i, e, f: (e, f, 0))],
        out_specs=pl.BlockSpec((tm, K), lambda i, e, f: (i, 0)),
        out_shape=jax.ShapeDtypeStruct((M, K), F32),
        scratch_shapes=[pltpu.VMEM((tm, K), F32 if precise else BF16), pltpu.VMEM((tm, K), F32), pltpu.VMEM((tm, LANES), F32)],
        compiler_params=_cparams(("parallel", "arbitrary", "arbitrary")),
        name="moe_ffn" if moe else "dense_ffn",
    )(x, gpre.reshape(1, K), gpost.reshape(1, K), router, wg, wu, wd)


INFO_E1, INFO_E2, INFO_R1, INFO_R2, INFO_W1, INFO_W2 = range(6)


def _route_kernel(x_ref, gpre_ref, router_ref, tril_ref, info_ref, cnt_ref, carry_ref):
    i = pl.program_id(0)

    @pl.when(i == 0)
    def _():
        carry_ref[...] = jnp.zeros_like(carry_ref)

    lane, i1, i2, w1, w2 = _top2(_rms(x_ref[...], gpre_ref[...]), router_ref[...], three_pass=True)
    sel =jnp.where((lane == i1) | (lane == i2), 1.0, 0.0)
    incl = _dot(tril_ref[...], sel.astype(BF16))
    rank = incl - sel + carry_ref[...]
    r1 = jnp.sum(jnp.where(lane == i1, rank, 0.0), axis=-1, keepdims=True)
    r2 = jnp.sum(jnp.where(lane == i2, rank, 0.0), axis=-1, keepdims=True)
    info = jnp.zeros(sel.shape, F32)
    for idx, val in ((INFO_E1, i1.astype(F32)), (INFO_E2, i2.astype(F32)), (INFO_R1, r1), (INFO_R2, r2),
                     (INFO_W1, w1), (INFO_W2, w2)):
        info = jnp.where(lane == idx, val, info)
    info_ref[...] = info
    carry_ref[...] += incl[incl.shape[0] - 1:, :]
    cnt_ref[...] = carry_ref[...]


def moe_route(x, gpre, router, tm):
    M, K = x.shape
    tril = (jnp.arange(tm)[None, :] <= jnp.arange(tm)[:, None]).astype(BF16)
    return pl.pallas_call(
        _route_kernel,
        grid=(M // tm,),
        in_specs=[pl.BlockSpec((tm, K), lambda i: (i, 0)),
                  pl.BlockSpec((1, K), lambda i: (0, 0)),
                  pl.BlockSpec((K, LANES), lambda i: (0, 0)),
                  pl.BlockSpec((tm, tm), lambda i: (0, 0))],
        out_specs=[pl.BlockSpec((tm, LANES), lambda i: (i, 0)),
                   pl.BlockSpec((1, LANES), lambda i: (0, 0))],
        out_shape=[jax.ShapeDtypeStruct((M, LANES), F32), jax.ShapeDtypeStruct((1, LANES), F32)],
        scratch_shapes=[pltpu.VMEM((1, LANES), F32)],
        compiler_params=_cparams(("arbitrary",)),
        name="moe_route",
    )(x, gpre.reshape(1, K), router, tril)


def _dispatch_kernel(dest_ref, x_ref, zeros_ref, xs_ref, sem):
    del zeros_ref
    tm = x_ref.shape[0]
    base = pl.program_id(0) * tm

    def issue(r, carry):
        for k in range(2):
            d = dest_ref[(base + r) * 2 + k]
            pltpu.make_async_copy(x_ref.at[pl.ds(r, 1), :], xs_ref.at[pl.ds(d, 1), :], sem).start()
        return carry

    lax.fori_loop(0, tm, issue, 0, unroll=DMA_UNROLL)
    for k in range(2):
        pltpu.make_async_copy(x_ref, xs_ref.at[pl.ds(0, tm), :], sem).wait()


def moe_dispatch(x, dest, n_slots, tm):
    M, K = x.shape
    return pl.pallas_call(
        _dispatch_kernel,
        grid_spec=pltpu.PrefetchScalarGridSpec(
            num_scalar_prefetch=1,
            grid=(M // tm,),
            in_specs=[pl.BlockSpec((tm, K), lambda i, dest: (i, 0)),
                      pl.BlockSpec(memory_space=pl.ANY)],
            out_specs=pl.BlockSpec(memory_space=pl.ANY),
            scratch_shapes=[pltpu.SemaphoreType.DMA(())]),
        out_shape=jax.ShapeDtypeStruct((n_slots, K), F32),
        input_output_aliases={2: 0},
        compiler_params=_cparams(("arbitrary",)),
        name="moe_dispatch",
    )(dest, x, jnp.zeros((n_slots, K), F32))


def _experts_kernel(te_ref, tv_ref, xs_ref, gpre_ref, wg_ref, wu_ref, wd_ref, o_ref, h_ref, acc_ref):
    del te_ref
    i = pl.program_id(0)
    f = pl.program_id(1)
    last = pl.num_programs(1) - 1

    @pl.when(tv_ref[i] == 1)
    def _():
        @pl.when(f == 0)
        def _():
            h_ref[...] = _rms(xs_ref[...], gpre_ref[...]).astype(BF16)
            acc_ref[...] = jnp.zeros_like(acc_ref)

        hb = h_ref[...]
        act = jax.nn.silu(_dot(hb, wg_ref[0].astype(BF16))) * _dot(hb, wu_ref[0].astype(BF16))
        acc_ref[...] += _dot(act.astype(BF16), wd_ref[0].astype(BF16))

        @pl.when(f == last)
        def _():
            o_ref[...] = acc_ref[...]

    @pl.when((tv_ref[i] == 0) & (f == last))
    def _():
        o_ref[...] = jnp.zeros_like(o_ref)


def moe_experts(xs, gpre, tile_expert, tile_valid, wg, wu, wd, tg, tf):
    S, K = xs.shape
    F = wg.shape[2]
    nf = F // tf
    fidx = lambda i, f, te, tv: jnp.where(tv[i] == 1, f, nf - 1)
    return pl.pallas_call(
        _experts_kernel,
        grid_spec=pltpu.PrefetchScalarGridSpec(
            num_scalar_prefetch=2,
            grid=(S // tg, nf),
            in_specs=[pl.BlockSpec((tg, K), lambda i, f, te, tv: (i, 0)),
                      pl.BlockSpec((1, K), lambda i, f, te, tv: (0, 0)),
                      pl.BlockSpec((1, K, tf), lambda i, f, te, tv: (te[i], 0, fidx(i, f, te, tv))),
                      pl.BlockSpec((1, K, tf), lambda i, f, te, tv: (te[i], 0, fidx(i, f, te, tv))),
                      pl.BlockSpec((1, tf, K), lambda i, f, te, tv: (te[i], fidx(i, f, te, tv), 0))],
            out_specs=pl.BlockSpec((tg, K), lambda i, f, te, tv: (i, 0)),
            scratch_shapes=[pltpu.VMEM((tg, K), BF16), pltpu.VMEM((tg, K), F32)]),
        out_shape=jax.ShapeDtypeStruct((S, K), F32),
        compiler_params=_cparams(("parallel", "arbitrary")),
        name="moe_experts",
    )(tile_expert, tile_valid, xs, gpre.reshape(1, K), wg, wu, wd)


def _combine_kernel(dest_ref, x_ref, info_ref, gpost_ref, ys_ref, o_ref, buf_ref, sem):
    tm = x_ref.shape[0]
    i = pl.program_id(0)

    def gather(tile, slot):
        def issue(r, carry):
            for k in range(2):
                d = dest_ref[(tile * tm + r) * 2 + k]
                pltpu.make_async_copy(ys_ref.at[pl.ds(d, 1), :], buf_ref.at[slot, k, pl.ds(r, 1), :],
                                      sem.at[slot]).start()
            return carry

        lax.fori_loop(0, tm, issue, 0, unroll=DMA_UNROLL)

    @pl.when(i == 0)
    def _():
        gather(0, 0)

    @pl.when(i + 1 < pl.num_programs(0))
    def _():
        gather(i + 1, (i + 1) % 2)

    slot = i % 2
    for k in range(2):
        pltpu.make_async_copy(ys_ref.at[pl.ds(0, tm), :], buf_ref.at[slot, k], sem.at[slot]).wait()
    info = info_ref[...]
    y = info[:, INFO_W1:INFO_W1 + 1] * buf_ref[slot, 0] + info[:, INFO_W2:INFO_W2 + 1] * buf_ref[slot, 1]
    o_ref[...] = x_ref[...] + _rms(y, gpost_ref[...])


def moe_combine(x, info, gpost, ys, dest, tm):
    M, K = x.shape
    return pl.pallas_call(
        _combine_kernel,
        grid_spec=pltpu.PrefetchScalarGridSpec(
            num_scalar_prefetch=1,
            grid=(M // tm,),
            in_specs=[pl.BlockSpec((tm, K), lambda i, dest: (i, 0)),
                      pl.BlockSpec((tm, LANES), lambda i, dest: (i, 0)),
                      pl.BlockSpec((1, K), lambda i, dest: (0, 0)),
                      pl.BlockSpec(memory_space=pl.ANY)],
            out_specs=pl.BlockSpec((tm, K), lambda i, dest: (i, 0)),
            scratch_shapes=[pltpu.VMEM((2, 2, tm, K), F32), pltpu.SemaphoreType.DMA((2,))]),
        out_shape=jax.ShapeDtypeStruct((M, K), F32),
        compiler_params=_cparams(("arbitrary",)),
        name="moe_combine",
    )(dest, x, info, gpost.reshape(1, K), ys)


def moe_routed(x, gpre, gpost, router, wg, wu, wd):
    M, K = x.shape
    tg = TG_MOE
    n_slots = 2 * M + N_EXPERTS * tg
    info, cnt = moe_route(x, gpre, router, TM_ROUTE)
    cnt = cnt[0, :N_EXPERTS].astype(jnp.int32)
    padded = (cnt + tg - 1) // tg * tg
    ends = jnp.cumsum(padded)
    offs = ends - padded
    ids = info[:, INFO_E1:INFO_E2 + 1].astype(jnp.int32)
    ranks = info[:, INFO_R1:INFO_R2 + 1].astype(jnp.int32)
    dest = (offs[ids] + ranks).reshape(2 * M)
    tile_start = jnp.arange(n_slots // tg, dtype=jnp.int32) * tg
    tile_valid = (tile_start < ends[-1]).astype(jnp.int32)
    tile_expert = jnp.minimum(jnp.sum((tile_start[:, None] >= ends[None, :]).astype(jnp.int32), axis=1),
                              N_EXPERTS - 1)
    tile_expert = jnp.where(tile_valid == 1, tile_expert, tile_expert[jnp.maximum(ends[-1] // tg - 1, 0)])
    xs = moe_dispatch(x, dest, n_slots, TM_DISPATCH)
    ys = moe_experts(xs, gpre, tile_expert, tile_valid, wg, wu, wd, tg, TF_ROUTED)
    return moe_combine(x, info, gpost, ys, dest, TM_COMBINE)


def _xattn_kernel(x_ref, k_ref, v_ref, wq_ref, wo_ref, gpre_ref, gpost_ref, o_ref):
    x = x_ref[0]
    h = _rms(x, gpre_ref[...]).astype(BF16)
    q = _dot(h, wq_ref[...])
    k = k_ref[0, 0].astype(BF16)
    v = v_ref[0, 0].astype(BF16)
    heads = [slice(hd * X_HEAD_DIM, (hd + 1) * X_HEAD_DIM) for hd in range(X_HEADS)]
    ss = [_dot_nt(q[:, sl].astype(BF16), k[:, sl]) * (X_HEAD_DIM ** -0.5) for sl in heads]
    es = [jnp.exp(s - jnp.max(s, axis=-1, keepdims=True)) for s in ss]
    ps = [(e / jnp.sum(e, axis=-1, keepdims=True)).astype(BF16) for e in es]
    o = jnp.concatenate([_dot(p, v[:, sl]) for p, sl in zip(ps, heads)], axis=-1).astype(BF16)
    o_ref[0] = x + _rms(_dot(o, wo_ref[...]), gpost_ref[...])


def xattn_prompt(x, k, v, l, wq, wo, gpre, gpost, tm):
    B, T, K = x.shape
    return pl.pallas_call(
        _xattn_kernel,
        grid=(B, T // tm),
        in_specs=[pl.BlockSpec((1, tm, K), lambda b, t: (b, t, 0)),
                  pl.BlockSpec((1, 1, MEM_LEN, K), lambda b, t: (l, b, 0, 0)),
                  pl.BlockSpec((1, 1, MEM_LEN, K), lambda b, t: (l, b, 0, 0)),
                  pl.BlockSpec((K, K), lambda b, t: (0, 0)),
                  pl.BlockSpec((K, K), lambda b, t: (0, 0)),
                  pl.BlockSpec((1, K), lambda b, t: (0, 0)),
                  pl.BlockSpec((1, K), lambda b, t: (0, 0))],
        out_specs=pl.BlockSpec((1, tm, K), lambda b, t: (b, t, 0)),
        out_shape=jax.ShapeDtypeStruct((B, T, K), F32),
        compiler_params=_cparams(("parallel", "parallel")),
        name="xattn_prompt",
    )(x, k, v, wq, wo, gpre.reshape(1, K), gpost.reshape(1, K))


def _xattn_sample_kernel(q_ref, k_ref, v_ref, o_ref):
    mc = XATTN_MEM_CHUNK
    chunks = [slice(c * mc, (c + 1) * mc) for c in range(MEM_LEN // mc)]
    for j in range(q_ref.shape[0]):
        q = q_ref[j]
        s = jnp.concatenate([jnp.sum(k_ref[0, j, rows] * q[None], axis=-1, keepdims=True) for rows in chunks],
                            axis=0) * (X_HEAD_DIM ** -0.5)
        e = jnp.exp(s - jnp.max(s, axis=0, keepdims=True))
        p = e / jnp.sum(e, axis=0, keepdims=True)
        o = None
        for rows in chunks:
            part = jnp.sum(p[rows] * v_ref[0, j, rows], axis=0)
            o = part if o is None else o + part
        o_ref[j] = o


def xattn_sample_core(q, k, v, l, nb):
    B = q.shape[0]
    kv_spec = pl.BlockSpec((1, nb, MEM_LEN, X_HEADS, X_HEAD_DIM), lambda b: (l, b, 0, 0, 0))
    return pl.pallas_call(
        _xattn_sample_kernel,
        grid=(B // nb,),
        in_specs=[pl.BlockSpec((nb, X_HEADS, X_HEAD_DIM), lambda b: (b, 0, 0)), kv_spec, kv_spec],
        out_specs=pl.BlockSpec((nb, X_HEADS, X_HEAD_DIM), lambda b: (b, 0, 0)),
        out_shape=jax.ShapeDtypeStruct((B, X_HEADS, X_HEAD_DIM), F32),
        compiler_params=_cparams(("parallel",)),
        name="xattn_sample",
    )(q, k, v)


def _ev_proj_kernel(x_ref, g_ref, w_ref, lng_ref, lnb_ref, cw_ref, cb_ref,
                    gu_ref, vln_ref, sz_ref, xa_ref, dt_ref, tail_ref, xp_ref):
    t = pl.program_id(1)
    tm = x_ref.shape[1]

    @pl.when(t == 0)
    def _():
        xp_ref[0:SUBLANES, :] = jnp.zeros((SUBLANES, B_CONV_DIM), F32)

    h = _rms(x_ref[0], g_ref[...]).astype(BF16)
    u = _dot(h, w_ref[:, 0:D])
    v = _dot(h, w_ref[:, D:2 * D])
    gu_ref[0] = jax.nn.gelu(u)
    z = _dot(h, w_ref[:, 2 * D:3 * D])
    vln_ref[0] = _layernorm(jax.nn.gelu(v), lng_ref[...], lnb_ref[...]).astype(BF16)
    x = _dot(h, w_ref[:, 3 * D:3 * D + B_CONV_DIM])
    sz_ref[0] = jax.nn.silu(z)
    dt_ref[0] = _dot(h, w_ref[:, DT_COL_BLOCK * LANES:(DT_COL_BLOCK + 1) * LANES])
    xp_ref[SUBLANES:SUBLANES + tm, :] = x
    conv = cb_ref[...] + cw_ref[B_CONV - 1:B_CONV, :] * x
    for k in range(B_CONV - 1):
        conv = conv + cw_ref[k:k + 1, :] * xp_ref[pl.ds(SUBLANES - (B_CONV - 1) + k, tm), :]
    xp_ref[0:SUBLANES, :] = x[tm - SUBLANES:tm, :]
    tail_ref[0] = x[tm - SUBLANES:tm, :]
    xa_ref[0] = jax.nn.silu(conv)


def ev_proj(x, g, w, P, tm):
    B, T, K = x.shape
    row = lambda a: a.reshape(1, -1)
    full = lambda shape: pl.BlockSpec(shape, lambda b, t: (0,) * len(shape))
    tile = lambda n: pl.BlockSpec((1, tm, n), lambda b, t: (b, t, 0))
    return pl.pallas_call(
        _ev_proj_kernel,
        grid=(B, T // tm),
        in_specs=[tile(K), full((1, K)), full((K, IN0_PAD)), full((1, D)), full((1, D)),
                  full((B_CONV, B_CONV_DIM)), full((1, B_CONV_DIM))],
        out_specs=[tile(D), tile(D), tile(D), tile(B_CONV_DIM), tile(LANES),
                   pl.BlockSpec((1, SUBLANES, B_CONV_DIM), lambda b, t: (b, 0, 0))],
        out_shape=[jax.ShapeDtypeStruct((B, T, D), F32), jax.ShapeDtypeStruct((B, T, D), BF16),
                   jax.ShapeDtypeStruct((B, T, D), F32), jax.ShapeDtypeStruct((B, T, B_CONV_DIM), F32),
                   jax.ShapeDtypeStruct((B, T, LANES), F32), jax.ShapeDtypeStruct((B, SUBLANES, B_CONV_DIM), F32)],
        scratch_shapes=[pltpu.VMEM((SUBLANES + tm, B_CONV_DIM), F32)],
        compiler_params=_cparams(("parallel", "arbitrary")),
        name="ev_proj",
    )(x, row(g), w, row(P["a_ln_g"][0]), row(P["a_ln_b"][0]), P["b_conv_w"][0], row(P["b_conv_b"][0]))


def _mixer_even_kernel(gu_ref, vln_ref, sz_ref, xa_ref, dt_ref, ws_ref, bst_ref,
                       dtb_ref, anar_ref, dexp_ref, bnorm_ref, tril_ref, expand_ref,
                       yab_ref, ssm_ref, st_ref):
    c = pl.program_id(1)
    L = A_CHUNK

    @pl.when(c == 0)
    def _():
        st_ref[...] = jnp.zeros_like(st_ref)

    row = lax.broadcasted_iota(jnp.int32, (L, L), 0)
    col = lax.broadcasted_iota(jnp.int32, (L, L), 1)
    causal = col <= row

    tril = tril_ref[...]
    lane = lax.broadcasted_iota(jnp.int32, (L, LANES), 1)

    def chunk(rows):
        gu = gu_ref[0, rows, :]
        vb = vln_ref[0, rows, :]
        a_heads = [slice(hh * LANES, (hh + 1) * LANES) for hh in range(A_HEADS)]
        mixes = [_dot(jnp.where(causal, ws_ref[hh], 0.0).astype(BF16), vb[:, sl]) for hh, sl in enumerate(a_heads)]
        for hh, sl in enumerate(a_heads):
            yab_ref[0, rows, sl] = (gu[:, sl] * (mixes[hh] + bst_ref[:, hh:hh + 1])).astype(BF16)

        xa = xa_ref[0, rows, :]
        xs = xa[:, :D]
        bm = xa[:, D:D + B_GROUPS * B_STATE]
        cm = xa[:, D + B_GROUPS * B_STATE:]

        dtf = _softplus(dt_ref[0, rows, :] + dtb_ref[...])
        cs_n = _sel_left(tril, dtf * anar_ref[...], _split3)
        dt_x = _sel_right(dtf, expand_ref[...], _split2)
        cs_x = _sel_right(cs_n, expand_ref[...], _split3)
        ecs_x = jnp.exp(cs_x)
        last_x = cs_x[L - 1:L, :]
        xdt = xs * dt_x
        xdt_b = xdt.astype(BF16)
        xdec_b = (xdt * jnp.exp(last_x - cs_x)).astype(BF16)
        groups = range(B_GROUPS)
        gsl = [slice(g * B_GROUP_W, (g + 1) * B_GROUP_W) for g in groups]
        bgs = [bm[:, g * B_STATE:(g + 1) * B_STATE] for g in groups]
        cgs = [cm[:, g * B_STATE:(g + 1) * B_STATE].astype(BF16) for g in groups]
        gmats = [_dot_nt(cgs[g], bgs[g].astype(BF16)) for g in groups]
        sts = [st_ref[g] for g in groups]
        y_offs = [ecs_x[:, gsl[g]] * _dot(cgs[g], sts[g].astype(BF16)) for g in groups]
        for g in groups:
            st_ref[g] = sts[g] * jnp.exp(last_x[:, gsl[g]]) + _dot(bgs[g].T.astype(BF16), xdec_b[:, gsl[g]])
        pairs = [(g, pair) for g in groups for pair in range(B_GROUP_W // LANES)]
        bases = [g * B_GROUP_W + pair * LANES for g, pair in pairs]
        cs_ts = [cs_x[:, base:base + LANES].T for base in bases]
        lhs = []
        for (g, _), base, cs_t in zip(pairs, bases, cs_ts):
            ms = []
            for half in range(2):
                ch = half * B_HEAD_DIM
                diff = cs_x[:, base + ch:base + ch + 1] - cs_t[ch:ch + 1, :]
                ms.append((gmats[g] * jnp.exp(jnp.where(causal, diff, -jnp.inf))).astype(BF16))
            lhs.append(jnp.concatenate(ms, axis=1))
        y_parts = []
        for (g, pair), base, m2 in zip(pairs, bases, lhs):
            xp2 = xdt_b[:, base:base + LANES]
            rhs = jnp.concatenate([jnp.where(lane < B_HEAD_DIM, xp2, jnp.zeros_like(xp2)),
                                   jnp.where(lane >= B_HEAD_DIM, xp2, jnp.zeros_like(xp2))], axis=0)
            y_parts.append(_dot(m2, rhs) + y_offs[g][:, pair * LANES:(pair + 1) * LANES])
        y = jnp.concatenate(y_parts, axis=-1) + dexp_ref[...] * xs
        yb = _group_rms(y * sz_ref[0, rows, :], bnorm_ref[...], B_GROUP_W)
        yab_ref[0, rows, D:] = yb.astype(BF16)

    for sub in range(gu_ref.shape[1] // L):
        chunk(slice(sub * L, (sub + 1) * L))

    @pl.when(c == pl.num_programs(1) - 1)
    def _():
        for g in range(B_GROUPS):
            ssm_ref[0, g] = st_ref[g].T


def _even_consts(P):
    head_of_ch = jnp.arange(D) // B_HEAD_DIM
    expand = (jnp.arange(LANES)[:, None] == head_of_ch[None, :]).astype(BF16)
    tril = (jnp.arange(A_CHUNK)[None, :] <= jnp.arange(A_CHUNK)[:, None]).astype(BF16)
    aexp = jnp.repeat(-jnp.exp(P["b_a_log"][0].astype(F32)), B_HEAD_DIM).reshape(1, D)
    dexp = jnp.repeat(P["b_d"][0].astype(F32), B_HEAD_DIM).reshape(1, D)
    dtb = jnp.pad(P["b_dt_bias"][0].astype(F32), (0, LANES - B_HEADS)).reshape(1, LANES)
    return expand, tril, aexp, dexp, dtb


def mixer_even_prompt(gu, vln, sz, xa, dt, P):
    B, T, _ = gu.shape
    L = A_CHUNK
    expand, tril, _, dexp, dtb = _even_consts(P)
    anar = jnp.pad(-jnp.exp(P["b_a_log"][0].astype(F32)), (0, LANES - B_HEADS)).reshape(1, LANES)
    row = lambda a: a.reshape(1, -1)
    full = lambda shape: pl.BlockSpec(shape, lambda b, c: (0,) * len(shape))
    R = A_ROWS
    chunk = lambda n: pl.BlockSpec((1, R, n), lambda b, c: (b, c, 0))
    return pl.pallas_call(
        _mixer_even_kernel,
        grid=(B, T // R),
        in_specs=[chunk(D), chunk(D), chunk(D), chunk(B_CONV_DIM), chunk(LANES),
                  full((A_HEADS, L, L)), full((L, A_HEADS)), full((1, LANES)), full((1, LANES)), full((1, D)),
                  full((1, D)), full((L, L)), full((LANES, D))],
        out_specs=[pl.BlockSpec((1, R, 2 * D), lambda b, c: (b, c, 0)),
                   pl.BlockSpec((1, B_GROUPS, B_GROUP_W, B_STATE), lambda b, c: (b, 0, 0, 0))],
        out_shape=[jax.ShapeDtypeStruct((B, T, 2 * D), BF16),
                   jax.ShapeDtypeStruct((B, B_GROUPS, B_GROUP_W, B_STATE), F32)],
        scratch_shapes=[pltpu.VMEM((B_GROUPS, B_STATE, B_GROUP_W), F32)],
        compiler_params=_cparams(("parallel", "arbitrary")),
        name="mixer_even_prompt",
    )(gu, vln, sz, xa, dt, P["a_ws"][0], P["a_bs"][0].T, dtb, anar, dexp, row(P["b_norm"][0]), tril, expand)


def _mixer_even_step_kernel(proj_ref, conv_ref, ssm_ref, ws0_ref, bs0_ref, lng_ref, lnb_ref, cw_ref, cb_ref,
                            dtb_ref, aexp_ref, dexp_ref, bnorm_ref, expand_ref,
                            yab_ref, av_ref, convo_ref, ssmo_ref):
    nb = proj_ref.shape[0]
    u = proj_ref[:, 0:D]
    v = proj_ref[:, D:2 * D]
    z = proj_ref[:, 2 * D:3 * D]
    x = proj_ref[:, 3 * D:3 * D + B_CONV_DIM]
    dt = proj_ref[:, DT_COL_BLOCK * LANES:(DT_COL_BLOCK + 1) * LANES]

    vln = _layernorm(jax.nn.gelu(v), lng_ref[...], lnb_ref[...])
    av_ref[...] = vln
    yab_ref[:, 0:D] = jax.nn.gelu(u) * (ws0_ref[...] * vln + bs0_ref[...])

    conv = cb_ref[...] + cw_ref[B_CONV - 1:B_CONV, :] * x
    for k in range(B_CONV - 1):
        conv = conv + cw_ref[k:k + 1, :] * conv_ref[k]
        if k > 0:
            convo_ref[k - 1] = conv_ref[k]
    convo_ref[B_CONV - 2] = x
    xa = jax.nn.silu(conv)
    xs = xa[:, :D]
    bm = xa[:, D:D + B_GROUPS * B_STATE]
    cm = xa[:, D + B_GROUPS * B_STATE:]
    dtf = _softplus(dt + dtb_ref[...])
    dt_x = _sel_right(dtf, expand_ref[...], _split3)
    dec_x = jnp.exp(dt_x * aexp_ref[...])
    xdt = xs * dt_x
    y_rows = []
    for g in range(B_GROUPS):
        gs = slice(g * B_GROUP_W, (g + 1) * B_GROUP_W)
        dec_t = _rows_to_cols(dec_x[:, gs])
        xdt_t = _rows_to_cols(xdt[:, gs])
        s_news = [ssm_ref[j, g] * dec_t[:, j:j + 1] + xdt_t[:, j:j + 1] * bm[j:j + 1, g * B_STATE:(g + 1) * B_STATE]
                  for j in range(nb)]
        for j in range(nb):
            ssmo_ref[j, g] = s_news[j]
        cjs = [jnp.broadcast_to(cm[j:j + 1, g * B_STATE:(g + 1) * B_STATE], (SUBLANES, B_STATE)) for j in range(nb)]
        y_rows.append(jnp.concatenate([_dot_nt_f32(cjs[j], s_news[j])[0:1, :] for j in range(nb)], axis=0))
    y = jnp.concatenate(y_rows, axis=-1) + dexp_ref[...] * xs
    yb = _group_rms(y * jax.nn.silu(z), bnorm_ref[...], B_GROUP_W)
    yab_ref[:, D:] = yb


def mixer_even_step(proj, conv0, ssm0, P, nb):
    B = proj.shape[0]
    expand, _, aexp, dexp, dtb = _even_consts(P)
    row = lambda a: a.reshape(1, -1)
    rep = lambda a: jnp.repeat(a.astype(F32), LANES).reshape(1, D)
    full = lambda shape: pl.BlockSpec(shape, lambda i: (0,) * len(shape))
    return pl.pallas_call(
        _mixer_even_step_kernel,
        grid=(B // nb,),
        in_specs=[pl.BlockSpec((nb, proj.shape[1]), lambda i: (i, 0)),
                  pl.BlockSpec((B_CONV - 1, nb, B_CONV_DIM), lambda i: (0, i, 0)),
                  pl.BlockSpec((nb, B_GROUPS, B_GROUP_W, B_STATE), lambda i: (i, 0, 0, 0)),
                  full((1, D)), full((1, D)), full((1, D)), full((1, D)),
                  full((B_CONV, B_CONV_DIM)), full((1, B_CONV_DIM)), full((1, LANES)), full((1, D)), full((1, D)),
                  full((1, D)), full((LANES, D))],
        out_specs=[pl.BlockSpec((nb, 2 * D), lambda i: (i, 0)),
                   pl.BlockSpec((nb, D), lambda i: (i, 0)),
                   pl.BlockSpec((B_CONV - 1, nb, B_CONV_DIM), lambda i: (0, i, 0)),
                   pl.BlockSpec((nb, B_GROUPS, B_GROUP_W, B_STATE), lambda i: (i, 0, 0, 0))],
        out_shape=[jax.ShapeDtypeStruct((B, 2 * D), F32),
                   jax.ShapeDtypeStruct((B, D), F32),
                   jax.ShapeDtypeStruct((B_CONV - 1, B, B_CONV_DIM), F32),
                   jax.ShapeDtypeStruct((B, B_GROUPS, B_GROUP_W, B_STATE), F32)],
        compiler_params=_cparams(("parallel",)),
        name="mixer_even_step",
    )(proj, conv0, ssm0, rep(P["a_ws"][0][:, 0, 0]), rep(P["a_bs"][0][:, 0]), row(P["a_ln_g"][0]), row(P["a_ln_b"][0]),
      P["b_conv_w"][0], row(P["b_conv_b"][0]), dtb, aexp, dexp, row(P["b_norm"][0]), expand)


def _gla_gates(q_raw, f_raw, lb):
    fg = lb + (1.0 - lb) * jax.nn.sigmoid(f_raw)
    return jax.nn.silu(q_raw), fg, 1.0 - fg


def _hgrn_kernel(q_ref, f_ref, i_ref, g_ref, lb_ref, cnorm_ref, tril_ref, o_ref, s_ref, st_ref):
    c = pl.program_id(1)
    L = C_CHUNK
    R = C_ROWS

    @pl.when(c == 0)
    def _():
        st_ref[...] = jnp.zeros_like(st_ref)

    row = lax.broadcasted_iota(jnp.int32, (R, R), 0)
    col = lax.broadcasted_iota(jnp.int32, (R, R), 1)
    causal = (col <= row) & (row // L == col // L)
    chunk_of_row = lax.broadcasted_iota(jnp.int32, (R, 1), 0) // L
    tril = tril_ref[...]
    heads = [slice(hh * C_KDIM, (hh + 1) * C_KDIM) for hh in range(C_HEADS)]

    def block(rows):
        q, fg, k = _gla_gates(q_ref[0, rows, :], f_ref[0, rows, :], lb_ref[...])
        v = i_ref[0, rows, :]
        bc = _sel_left(tril, jnp.log(fg), _split3)
        q_in = (q * jnp.exp(bc)).astype(BF16)
        k_in = (k * jnp.exp(-bc)).astype(BF16)
        vb = v.astype(BF16)
        btots, k_decs = [], []
        for s in range(R // L):
            btot = bc[(s + 1) * L - 1:(s + 1) * L, :]
            btots.append(btot)
            k_decs.append(jnp.where(chunk_of_row == s, k * jnp.exp(btot - bc), 0.0).astype(BF16))
        atts = [jnp.where(causal, _dot_nt(q_in[:, sl], k_in[:, sl]), 0.0).astype(BF16) for sl in heads]
        v_ts = [v[:, sl].T.astype(BF16) for sl in heads]
        sts = [st_ref[hh] for hh in range(C_HEADS)]
        inters = [[] for _ in heads]
        for s in range(R // L):
            for hh, sl in enumerate(heads):
                inters[hh].append(_dot_nt(q_in[s * L:(s + 1) * L, sl], sts[hh].astype(BF16)))
            sts = [sts[hh] * jnp.exp(btots[s][:, sl]) + _dot(v_ts[hh], k_decs[s][:, sl])
                   for hh, sl in enumerate(heads)]
        for hh in range(C_HEADS):
            st_ref[hh] = sts[hh]
        outs = [_dot(atts[hh], vb[:, sl]) + jnp.concatenate(inters[hh], axis=0) for hh, sl in enumerate(heads)]
        o = _group_rms(jnp.concatenate(outs, axis=-1), cnorm_ref[...], C_KDIM)
        o_ref[0, rows, :] = (o * jax.nn.silu(g_ref[0, rows, :])).astype(BF16)

    for blk in range(q_ref.shape[1] // R):
        block(slice(blk * R, (blk + 1) * R))

    @pl.when(c == pl.num_programs(1) - 1)
    def _():
        for hh in range(C_HEADS):
            s_ref[0, hh] = st_ref[hh].T


def hgrn_prompt(proj, lb, cnorm):
    B, T, _ = proj.shape
    L = C_ROWS
    r = jnp.arange(L)
    tril = ((r[None, :] <= r[:, None]) & (r[None, :] // C_CHUNK == r[:, None] // C_CHUNK)).astype(BF16)
    full = lambda shape: pl.BlockSpec(shape, lambda b, c: (0,) * len(shape))
    S = C_STEP_ROWS
    return pl.pallas_call(
        _hgrn_kernel,
        grid=(B, T // S),
        in_specs=[pl.BlockSpec((1, S, D), lambda b, c: (b, c, 0)),
                  pl.BlockSpec((1, S, D), lambda b, c: (b, c, 1)),
                  pl.BlockSpec((1, S, D), lambda b, c: (b, c, 2)),
                  pl.BlockSpec((1, S, D), lambda b, c: (b, c, 3)),
                  full((1, D)), full((1, D)), full((L, L))],
        out_specs=[pl.BlockSpec((1, S, D), lambda b, c: (b, c, 0)),
                   pl.BlockSpec((1, C_HEADS, C_KDIM, C_KDIM), lambda b, c: (b, 0, 0, 0))],
        out_shape=[jax.ShapeDtypeStruct((B, T, D), BF16),
                   jax.ShapeDtypeStruct((B, C_HEADS, C_KDIM, C_KDIM), F32)],
        scratch_shapes=[pltpu.VMEM((C_HEADS, C_KDIM, C_KDIM), F32)],
        compiler_params=_cparams(("parallel", "arbitrary")),
        name="hgrn_prompt",
    )(proj, proj, proj, proj, lb.reshape(1, D), cnorm.reshape(1, D), tril)


def _hgrn_step_kernel(proj_ref, s_ref, lb_ref, cnorm_ref, o_ref, so_ref):
    nb = proj_ref.shape[0]
    q, fg, k = _gla_gates(proj_ref[:, 0:D], proj_ref[:, D:2 * D], lb_ref[...])
    v = proj_ref[:, 2 * D:3 * D]
    g = proj_ref[:, 3 * D:4 * D]
    heads = [slice(hh * C_KDIM, (hh + 1) * C_KDIM) for hh in range(C_HEADS)]
    fg_ts = [_rows_to_cols(fg[:, sl]) for sl in heads]
    k_ts = [_rows_to_cols(k[:, sl]) for sl in heads]
    outs = []
    for hh, sl in enumerate(heads):
        s_news = [s_ref[j, hh] * fg_ts[hh][:, j:j + 1] + k_ts[hh][:, j:j + 1] * v[j:j + 1, sl] for j in range(nb)]
        for j in range(nb):
            so_ref[j, hh] = s_news[j]
        qjs = [jnp.broadcast_to(q[j:j + 1, sl], (SUBLANES, C_KDIM)) for j in range(nb)]
        outs.append(jnp.concatenate([_dot_f32(qjs[j], s_news[j])[0:1, :] for j in range(nb)], axis=0))
    o = _group_rms(jnp.concatenate(outs, axis=-1), cnorm_ref[...], C_KDIM)
    o_ref[...] = o * jax.nn.silu(g)


def hgrn_step(proj, s0, lb, cnorm, nb):
    B = proj.shape[0]
    full = lambda shape: pl.BlockSpec(shape, lambda i: (0,) * len(shape))
    return pl.pallas_call(
        _hgrn_step_kernel,
        grid=(B // nb,),
        in_specs=[pl.BlockSpec((nb, 4 * D), lambda i: (i, 0)),
                  pl.BlockSpec((nb, C_HEADS, C_KDIM, C_KDIM), lambda i: (i, 0, 0, 0)),
                  full((1, D)), full((1, D))],
        out_specs=[pl.BlockSpec((nb, D), lambda i: (i, 0)),
                   pl.BlockSpec((nb, C_HEADS, C_KDIM, C_KDIM), lambda i: (i, 0, 0, 0))],
        out_shape=[jax.ShapeDtypeStruct((B, D), F32),
                   jax.ShapeDtypeStruct((B, C_HEADS, C_KDIM, C_KDIM), F32)],
        compiler_params=_cparams(("parallel",)),
        name="hgrn_step",
    )(proj, s0, lb.reshape(1, D), cnorm.reshape(1, D))


TM_PROJ = 512
TM_EV_PROJ = 256
TM_OUT = 512
TM_ATTN = 512
TM_FFN = 1024
TF_DENSE = 256
TF_MOE = 896
TF_ROUTED = 512
TG_MOE = 1024
TM_ROUTE = 512
TM_DISPATCH = 1024
TM_COMBINE = 512
DMA_UNROLL = 8
ROUTED_MIN_TOKENS = 8 * TG_MOE
STEP_NB = 8
XATTN_STEP_NB = 4
XATTN_MEM_CHUNK = 64
TN_STEP = 512
TK_STEP = 512
IN0_STEP_PAD = 5120


def _prep_weights(P):
    W = {}
    W["ev_w_in"] = jnp.pad(P["ev_w_in"][0], ((0, 0), (0, IN0_PAD - IN0))).astype(BF16)
    W["ev_w_out"] = P["ev_w_out"][0].astype(BF16)
    W["od_w_in"] = P["od_w_in"][0].astype(BF16)
    W["od_w_out"] = P["od_w_out"][0].astype(BF16)
    W["xa_wq"] = P["xa_wq"].astype(BF16)
    W["xa_wo"] = P["xa_wo"].astype(BF16)
    W["xa_wk"] = P["xa_wk"].astype(BF16)
    W["xa_wv"] = P["xa_wv"].astype(BF16)
    W["ffn"] = tuple(P[n].astype(BF16) for n in ("ffn_w_gate", "ffn_w_up", "ffn_w_down"))
    W["moe"] = tuple(P[n][0] for n in ("moe_w_gate", "moe_w_up", "moe_w_down"))
    W["router"] = jnp.pad(P["moe_router"][0].astype(F32), ((0, 0), (0, LANES - N_EXPERTS)))
    lbp = jax.nn.softmax(P["hgrn_lb_logits"].astype(F32), axis=0)
    W["lower_bounds"] = jnp.cumsum(lbp, axis=0) - lbp[0]
    return W


def _channel_mix(x2, l, P, W, tm):
    if l == 0:
        wg, wu, wd = W["ffn"]
        return ffn(x2, P["norm_ffn_pre"][l], P["norm_ffn_post"][l], W["router"], wg, wu, wd, tm, TF_DENSE, False)
    wg, wu, wd = W["moe"]
    if x2.shape[0] >= ROUTED_MIN_TOKENS:
        return moe_routed(x2, P["norm_ffn_pre"][l], P["norm_ffn_post"][l], W["router"], wg, wu, wd)
    return ffn(x2, P["norm_ffn_pre"][l], P["norm_ffn_post"][l], W["router"], wg, wu, wd, tm, TF_MOE, True)


def _trunk_prompt(x, mem_k, mem_v, P, W):
    B, T, _ = x.shape
    M = B * T
    x2 = x.reshape(M, D)
    gu, vln, sz, xa, dt, tail = ev_proj(x, P["norm_mix_pre"][0], W["ev_w_in"], P, TM_EV_PROJ)
    yab, ssm = mixer_even_prompt(gu, vln, sz, xa, dt, P)
    conv = tail[:, SUBLANES - (B_CONV - 1):, :]
    x2 = matmul_norm_res(yab.reshape(M, 2 * D), W["ev_w_out"], P["norm_mix_post"][0], x2, TM_OUT)
    x2 = xattn_prompt(x2.reshape(B, T, D), mem_k, mem_v, 0, W["xa_wq"][0], W["xa_wo"][0],
                      P["norm_x_pre"][0], P["norm_x_post"][0], TM_ATTN).reshape(M, D)
    x2 = _channel_mix(x2, 0, P, W, TM_FFN)
    proj = norm_matmul(x2, P["norm_mix_pre"][1], W["od_w_in"], TM_PROJ).reshape(B, T, 4 * D)
    o, hgrn = hgrn_prompt(proj, W["lower_bounds"][1], P["c_norm"][0])
    x2 = matmul_norm_res(o.reshape(M, D), W["od_w_out"], P["norm_mix_post"][1], x2, TM_OUT)
    x2 = xattn_prompt(x2.reshape(B, T, D), mem_k, mem_v, 1, W["xa_wq"][1], W["xa_wo"][1],
                      P["norm_x_pre"][1], P["norm_x_post"][1], TM_ATTN).reshape(M, D)
    x2 = _channel_mix(x2, 1, P, W, TM_FFN)
    return x2.reshape(B, T, D), conv, ssm, hgrn


def _xattn_step(x2, l, mem_k, mem_v, P):
    B = x2.shape[0]
    q = norm_matmul(x2, P["norm_x_pre"][l], P["xa_wq"][l], B, TN_STEP)
    o = xattn_sample_core(q.reshape(B, X_HEADS, X_HEAD_DIM), mem_k, mem_v, l, XATTN_STEP_NB).reshape(B, D)
    return matmul_norm_res(o, P["xa_wo"][l], P["norm_x_post"][l], x2, B)


def _trunk_step(x, mem_k, mem_v, conv0, ssm0, hgrn0, P, W):
    B = x.shape[0]
    x2 = x.reshape(B, D)
    w_in = jnp.pad(P["ev_w_in"][0], ((0, 0), (0, IN0_STEP_PAD - IN0)))
    proj = norm_matmul(x2, P["norm_mix_pre"][0], w_in, B, TN_STEP)
    yab, av, conv, ssm = mixer_even_step(proj, jnp.swapaxes(conv0, 0, 1), ssm0, P, STEP_NB)
    x2 = matmul_norm_res(yab, P["ev_w_out"][0], P["norm_mix_post"][0], x2, B, TK_STEP)
    x2 = _xattn_step(x2, 0, mem_k, mem_v, P)
    x2 = ffn(x2, P["norm_ffn_pre"][0], P["norm_ffn_post"][0], W["router"],
             P["ffn_w_gate"], P["ffn_w_up"], P["ffn_w_down"], B, TF_DENSE, False, precise=True)
    proj = norm_matmul(x2, P["norm_mix_pre"][1], P["od_w_in"][0], B, TN_STEP)
    o, hgrn = hgrn_step(proj, hgrn0, W["lower_bounds"][1], P["c_norm"][0], STEP_NB)
    x2 = matmul_norm_res(o, P["od_w_out"][0], P["norm_mix_post"][1], x2, B)
    x2 = _xattn_step(x2, 1, mem_k, mem_v, P)
    x2 = _channel_mix(x2, 1, P, W, B)
    return x2.reshape(B, 1, D), jnp.swapaxes(conv, 0, 1), ssm, hgrn, av


def kernel(x_prompt, x_sample, mem_prompt, cache_mem_k, cache_mem_v, state_conv, state_ssm, state_hgrn,
           norm_mix_pre, norm_mix_post, norm_x_pre, norm_x_post, norm_ffn_pre, norm_ffn_post, norm_mem,
           xa_wq, xa_wk, xa_wv, xa_wo,
           ev_w_in, a_ws, a_bs, a_ln_g, a_ln_b, b_conv_w, b_conv_b, b_dt_bias, b_a_log, b_d, b_norm, ev_w_out,
           ffn_w_gate, ffn_w_up, ffn_w_down,
           od_w_in, hgrn_lb_logits, c_norm, od_w_out,
           moe_router, moe_w_gate, moe_w_up, moe_w_down):
    P = dict(norm_mix_pre=norm_mix_pre, norm_mix_post=norm_mix_post, norm_x_pre=norm_x_pre, norm_x_post=norm_x_post,
             norm_ffn_pre=norm_ffn_pre, norm_ffn_post=norm_ffn_post, xa_wq=xa_wq, xa_wk=xa_wk, xa_wv=xa_wv,
             xa_wo=xa_wo, ev_w_in=ev_w_in, a_ws=a_ws, a_bs=a_bs, a_ln_g=a_ln_g, a_ln_b=a_ln_b, b_conv_w=b_conv_w,
             b_conv_b=b_conv_b, b_dt_bias=b_dt_bias, b_a_log=b_a_log, b_d=b_d, b_norm=b_norm, ev_w_out=ev_w_out,
             ffn_w_gate=ffn_w_gate, ffn_w_up=ffn_w_up, ffn_w_down=ffn_w_down, od_w_in=od_w_in,
             hgrn_lb_logits=hgrn_lb_logits, c_norm=c_norm, od_w_out=od_w_out, moe_router=moe_router,
             moe_w_gate=moe_w_gate, moe_w_up=moe_w_up, moe_w_down=moe_w_down)
    W = _prep_weights(P)
    depth = norm_mem.shape[0]
    bp, T, _ = x_prompt.shape
    bs = x_sample.shape[0]

    mem_k_p, mem_v_p, mem_k_out, mem_v_out = mem_kv(mem_prompt.reshape(bp * MEM_LEN, D), norm_mem,
                                                    W["xa_wk"], W["xa_wv"], TM_PROJ)
    mem_k_p = mem_k_p.reshape(depth, bp, MEM_LEN, D)
    mem_v_p = mem_v_p.reshape(depth, bp, MEM_LEN, D)
    y_p, conv_p, ssm_p, hgrn_p = _trunk_prompt(x_prompt, mem_k_p, mem_v_p, P, W)

    y_s, conv_s, ssm_s, hgrn_s, av_s = _trunk_step(
        x_sample, cache_mem_k, cache_mem_v,
        state_conv[0], state_ssm[0].reshape(bs, B_GROUPS, B_GROUP_W, B_STATE), state_hgrn[0], P, W)

    kv_shape = (depth, bp, MEM_LEN, X_HEADS, X_HEAD_DIM)
    ssm_shape = (B_GROUPS, B_GROUP_W // B_HEAD_DIM, B_HEAD_DIM, B_STATE)
    return (y_p, y_s, mem_k_out.reshape(kv_shape), mem_v_out.reshape(kv_shape),
            conv_p[None], ssm_p.reshape((1, bp) + ssm_shape), hgrn_p[None],
            conv_s[None], ssm_s.reshape((1, bs) + ssm_shape), hgrn_s[None], av_s.reshape(1, bs, 1, D))
```

```python
import functools

import jax
import jax.numpy as jnp
from jax import lax
from jax.experimental import pallas as pl
from jax.experimental.pallas import tpu as pltpu

F32 = jnp.float32
BF16 = jnp.bfloat16
EPS = 1e-6

D = 1024
LANES = 128
SUBLANES = 8
A_HEADS = 8
A_CHUNK = 128
A_ROWS = 512
B_HEADS = 16
B_HEAD_DIM = 64
B_GROUPS = 2
B_GROUP_W = 512
B_STATE = 128
B_CONV = 4
B_CONV_DIM = 1536
IN0 = 4624
IN0_PAD = 4736
DT_COL_BLOCK = 36
C_HEADS = 8
C_KDIM = 128
C_CHUNK = 64
C_ROWS = 128
C_STEP_ROWS = 512
X_HEADS = 4
X_HEAD_DIM = 256
MEM_LEN = 256
N_EXPERTS = 8

VMEM_LIMIT = 56 * 1024 * 1024


def _cparams(sem):
    return pltpu.CompilerParams(dimension_semantics=sem, vmem_limit_bytes=VMEM_LIMIT)


def _dot(a, b):
    return jnp.dot(a, b, preferred_element_type=F32)


def _dot_nt(a, b):
    return lax.dot_general(a, b, (((1,), (1,)), ((), ())), preferred_element_type=F32)


def _dot_f32(a, b):
    return jnp.dot(a, b, precision=lax.Precision.HIGHEST, preferred_element_type=F32)


def _dot_nt_f32(a, b):
    return lax.dot_general(a, b, (((1,), (1,)), ((), ())), precision=lax.Precision.HIGHEST,
                           preferred_element_type=F32)


def _mm(a, w):
    if w.dtype == F32:
        return _dot_f32(a.astype(F32), w)
    return _dot(a.astype(BF16), w)


def _rms(x, g):
    return x * lax.rsqrt(jnp.mean(x * x, axis=-1, keepdims=True) + EPS) * g


def _split2(x):
    hi = x.astype(BF16)
    lo = (x - hi.astype(F32)).astype(BF16)
    return hi, lo


def _split3(x):
    hi = x.astype(BF16)
    r = x - hi.astype(F32)
    mid = r.astype(BF16)
    lo = (r - mid.astype(F32)).astype(BF16)
    return hi, mid, lo


def _sel_left(m, x, parts):
    out = None
    for p in parts(x):
        t = _dot(m, p)
        out = t if out is None else out + t
    return out


def _sel_right(x, m, parts):
    out = None
    for p in parts(x):
        t = _dot(p, m)
        out = t if out is None else out + t
    return out


def _rows_to_cols(x):
    n, w = x.shape
    if n < LANES:
        x = jnp.concatenate([x, jnp.zeros((LANES - n, w), x.dtype)], axis=0)
    return x.T


def _softplus(x):
    return jnp.maximum(x, 0.0) + jnp.log1p(jnp.exp(-jnp.abs(x)))


def _layernorm(x, g, b):
    xc = x - jnp.mean(x, axis=-1, keepdims=True)
    return xc * lax.rsqrt(jnp.mean(xc * xc, axis=-1, keepdims=True) + EPS) * g + b


def _group_rms(x, g, width):
    parts = []
    for s in range(0, x.shape[-1], width):
        t = x[:, s:s + width]
        parts.append(t * lax.rsqrt(jnp.mean(t * t, axis=-1, keepdims=True) + EPS))
    return jnp.concatenate(parts, axis=-1) * g


def _norm_matmul_kernel(x_ref, g_ref, w_ref, o_ref):
    o_ref[...] = _mm(_rms(x_ref[...], g_ref[...]), w_ref[...])


def norm_matmul(x, g, w, tm, tn=None):
    M, K = x.shape
    N = w.shape[1]
    tn = N if tn is None else tn
    return pl.pallas_call(
        _norm_matmul_kernel,
        grid=(M // tm, N // tn),
        in_specs=[pl.BlockSpec((tm, K), lambda i, j: (i, 0)),
                  pl.BlockSpec((1, K), lambda i, j: (0, 0)),
                  pl.BlockSpec((K, tn), lambda i, j: (0, j))],
        out_specs=pl.BlockSpec((tm, tn), lambda i, j: (i, j)),
        out_shape=jax.ShapeDtypeStruct((M, N), F32),
        compiler_params=_cparams(("parallel", "parallel")),
        name="norm_matmul",
    )(x, g.reshape(1, K), w)


def _mem_kv_kernel(x_ref, g_ref, wk_ref, wv_ref, k_ref, v_ref, k5_ref, v5_ref):
    h = _rms(x_ref[...], g_ref[0]).astype(BF16)
    for w_ref, o_ref, o5_ref in ((wk_ref, k_ref, k5_ref), (wv_ref, v_ref, v5_ref)):
        r = _dot(h, w_ref[0])
        o_ref[0] = r
        for hd in range(X_HEADS):
            o5_ref[0, :, hd, :] = r[:, hd * X_HEAD_DIM:(hd + 1) * X_HEAD_DIM]


def mem_kv(mem, g, wk, wv, tm):
    M, K = mem.shape
    depth = g.shape[0]
    wspec = pl.BlockSpec((1, K, K), lambda l, i: (l, 0, 0))
    ospec = pl.BlockSpec((1, tm, K), lambda l, i: (l, i, 0))
    o5spec = pl.BlockSpec((1, tm, X_HEADS, X_HEAD_DIM), lambda l, i: (l, i, 0, 0))
    return pl.pallas_call(
        _mem_kv_kernel,
        grid=(depth, M // tm),
        in_specs=[pl.BlockSpec((tm, K), lambda l, i: (i, 0)),
                  pl.BlockSpec((1, 1, K), lambda l, i: (l, 0, 0)), wspec, wspec],
        out_specs=[ospec, ospec, o5spec, o5spec],
        out_shape=[jax.ShapeDtypeStruct((depth, M, K), F32)] * 2
        + [jax.ShapeDtypeStruct((depth, M, X_HEADS, X_HEAD_DIM), F32)] * 2,
        compiler_params=_cparams(("parallel", "parallel")),
        name="mem_kv",
    )(mem, g.reshape(depth, 1, K), wk, wv)


def _matmul_norm_res_kernel(a_ref, w_ref, g_ref, r_ref, o_ref, acc_ref):
    k = pl.program_id(1)

    @pl.when(k == 0)
    def _():
        acc_ref[...] = jnp.zeros_like(acc_ref)

    acc_ref[...] += _mm(a_ref[...], w_ref[...])

    @pl.when(k == pl.num_programs(1) - 1)
    def _():
        o_ref[...] = r_ref[...] + _rms(acc_ref[...], g_ref[...])


def _matmul_norm_res_whole_k_kernel(a_ref, w_ref, g_ref, r_ref, o_ref):
    o_ref[...] = r_ref[...] + _rms(_mm(a_ref[...], w_ref[...]), g_ref[...])


def matmul_norm_res(a, w, g, res, tm, tk=None):
    M, K = a.shape
    N = w.shape[1]
    tk = K if tk is None else tk
    if tk == K:
        return pl.pallas_call(
            _matmul_norm_res_whole_k_kernel,
            grid=(M // tm,),
            in_specs=[pl.BlockSpec((tm, K), lambda i: (i, 0)),
                      pl.BlockSpec((K, N), lambda i: (0, 0)),
                      pl.BlockSpec((1, N), lambda i: (0, 0)),
                      pl.BlockSpec((tm, N), lambda i: (i, 0))],
            out_specs=pl.BlockSpec((tm, N), lambda i: (i, 0)),
            out_shape=jax.ShapeDtypeStruct((M, N), F32),
            compiler_params=_cparams(("parallel",)),
            name="matmul_norm_res",
        )(a, w, g.reshape(1, N), res)
    return pl.pallas_call(
        _matmul_norm_res_kernel,
        grid=(M // tm, K // tk),
        in_specs=[pl.BlockSpec((tm, tk), lambda i, k: (i, k)),
                  pl.BlockSpec((tk, N), lambda i, k: (k, 0)),
                  pl.BlockSpec((1, N), lambda i, k: (0, 0)),
                  pl.BlockSpec((tm, N), lambda i, k: (i, 0))],
        out_specs=pl.BlockSpec((tm, N), lambda i, k: (i, 0)),
        out_shape=jax.ShapeDtypeStruct((M, N), F32),
        scratch_shapes=[pltpu.VMEM((tm, N), F32)],
        compiler_params=_cparams(("parallel", "arbitrary")),
        name="matmul_norm_res",
    )(a, w, g.reshape(1, N), res)


def _top2(h, router, three_pass=False):
    if three_pass:
        hh, hl = _split2(h)
        rh, rl = _split2(router)
        lg = _dot(hh, rh) + _dot(hh, rl) + _dot(hl, rh)
    else:
        lg = _dot_f32(h, router)
    lane = lax.broadcasted_iota(jnp.int32, lg.shape, 1)
    lg = jnp.where(lane < N_EXPERTS, lg, -jnp.inf)
    m1 = jnp.max(lg, axis=-1, keepdims=True)
    i1 = jnp.min(jnp.where(lg == m1, lane, LANES), axis=-1, keepdims=True)
    lg2 = jnp.where(lane == i1, -jnp.inf, lg)
    m2 = jnp.max(lg2, axis=-1, keepdims=True)
    i2 = jnp.min(jnp.where(lg2 == m2, lane, LANES), axis=-1, keepdims=True)
    e2 = jnp.exp(m2 - m1)
    den = 1.0 + e2
    return lane, i1, i2, 1.0 / den, e2 / den


def _top2_gates(h, router):
    lane, i1, i2, w1, w2 = _top2(h, router)
    return jnp.where(lane == i1, w1, 0.0) + jnp.where(lane == i2, w2, 0.0)


def _ffn_kernel(x_ref, gpre_ref, gpost_ref, router_ref, wg_ref, wu_ref, wd_ref, o_ref,
                h_ref, acc_ref, gates_ref, *, moe):
    e = pl.program_id(1)
    f = pl.program_id(2)

    @pl.when((e == 0) & (f == 0))
    def _():
        h = _rms(x_ref[...], gpre_ref[...])
        h_ref[...] = h.astype(h_ref.dtype)
        acc_ref[...] = jnp.zeros_like(acc_ref)
        if moe:
            gates_ref[...] = _top2_gates(h, router_ref[...])
        else:
            gates_ref[...] = jnp.zeros_like(gates_ref)

    hb = h_ref[...]
    act = jax.nn.silu(_mm(hb, wg_ref[0].astype(hb.dtype))) * _mm(hb, wu_ref[0].astype(hb.dtype))
    if moe:
        gates = gates_ref[...]
        lane = lax.broadcasted_iota(jnp.int32, gates.shape, 1)
        act = act * jnp.sum(jnp.where(lane == e, gates, 0.0), axis=-1, keepdims=True)
    acc_ref[...] += _mm(act, wd_ref[0].astype(hb.dtype))

    @pl.when((e == pl.num_programs(1) - 1) & (f == pl.num_programs(2) - 1))
    def _():
        o_ref[...] = x_ref[...] + _rms(acc_ref[...], gpost_ref[...])


def ffn(x, gpre, gpost, router, wg, wu, wd, tm, tf, moe, precise=False):
    M, K = x.shape
    E, _, F = wg.shape
    return pl.pallas_call(
        functools.partial(_ffn_kernel, moe=moe),
        grid=(M // tm, E, F // tf),
        in_specs=[pl.BlockSpec((tm, K), lambda i, e, f: (i, 0)),
                  pl.BlockSpec((1, K), lambda i, e, f: (0, 0)),
                  pl.BlockSpec((1, K), lambda i, e, f: (0, 0)),
                  pl.BlockSpec((K, LANES), lambda i, e, f: (0, 0)),
                  pl.BlockSpec((1, K, tf), lambda i, e, f: (e, 0, f)),
                  pl.BlockSpec((1, K, tf), lambda i, e, f: (e, 0, f)),
                  pl.BlockSpec((1, tf, K), lambda i, e, f: (e, f, 0))],
        out_specs=pl.BlockSpec((tm, K), lambda i, e, f: (i, 0)),
        out_shape=jax.ShapeDtypeStruct((M, K), F32),
        scratch_shapes=[pltpu.VMEM((tm, K), F32 if precise else BF16), pltpu.VMEM((tm, K), F32), pltpu.VMEM((tm, LANES), F32)],
        compiler_params=_cparams(("parallel", "arbitrary", "arbitrary")),
        name="moe_ffn" if moe else "dense_ffn",
    )(x, gpre.reshape(1, K), gpost.reshape(1, K), router, wg, wu, wd)


INFO_E1, INFO_E2, INFO_R1, INFO_R2, INFO_W1, INFO_W2 = range(6)


def _route_kernel(x_ref, gpre_ref, router_ref, tril_ref, info_ref, cnt_ref, carry_ref):
    i = pl.program_id(0)

    @pl.when(i == 0)
    def _():
        carry_ref[...] = jnp.zeros_like(carry_ref)

    rb = tril_ref.shape[0]
    blocks = [slice(r, r + rb) for r in range(0, x_ref.shape[0], rb)]
    tops = [_top2(_rms(x_ref[rows, :], gpre_ref[...]), router_ref[...], three_pass=True) for rows in blocks]
    sels = [jnp.where((lane == i1) | (lane == i2), 1.0, 0.0) for lane, i1, i2, _, _ in tops]
    incls = [_dot(tril_ref[...], sel.astype(BF16)) for sel in sels]
    base = carry_ref[...]
    for rows, (lane, i1, i2, w1, w2), sel, incl in zip(blocks, tops, sels, incls):
        rank = incl - sel + base
        r1 = jnp.sum(jnp.where(lane == i1, rank, 0.0), axis=-1, keepdims=True)
        r2 = jnp.sum(jnp.where(lane == i2, rank, 0.0), axis=-1, keepdims=True)
        info = jnp.zeros(sel.shape, F32)
        for idx, val in ((INFO_E1, i1.astype(F32)), (INFO_E2, i2.astype(F32)), (INFO_R1, r1), (INFO_R2, r2),
                         (INFO_W1, w1), (INFO_W2, w2)):
            info = jnp.where(lane == idx, val, info)
        info_ref[rows, :] = info
        base = base + incl[rb - 1:, :]
    carry_ref[...] = base
    cnt_ref[...] = base


def moe_route(x, gpre, router, tm):
    M, K = x.shape
    rb = ROUTE_ROWS
    tril = (jnp.arange(rb)[None, :] <= jnp.arange(rb)[:, None]).astype(BF16)
    return pl.pallas_call(
        _route_kernel,
        grid=(M // tm,),
        in_specs=[pl.BlockSpec((tm, K), lambda i: (i, 0)),
                  pl.BlockSpec((1, K), lambda i: (0, 0)),
                  pl.BlockSpec((K, LANES), lambda i: (0, 0)),
                  pl.BlockSpec((rb, rb), lambda i: (0, 0))],
        out_specs=[pl.BlockSpec((tm, LANES), lambda i: (i, 0)),
                   pl.BlockSpec((1, LANES), lambda i: (0, 0))],
        out_shape=[jax.ShapeDtypeStruct((M, LANES), F32), jax.ShapeDtypeStruct((1, LANES), F32)],
        scratch_shapes=[pltpu.VMEM((1, LANES), F32)],
        compiler_params=_cparams(("arbitrary",)),
        name="moe_route",
    )(x, gpre.reshape(1, K), router, tril)


def _dispatch_kernel(dest_ref, x_ref, zeros_ref, xs_ref, sem):
    del zeros_ref
    tm = x_ref.shape[0]
    base = pl.program_id(0) * tm

    def issue(r, carry):
        for k in range(2):
            d = dest_ref[(base + r) * 2 + k]
            pltpu.make_async_copy(x_ref.at[pl.ds(r, 1), :], xs_ref.at[pl.ds(d, 1), :], sem).start()
        return carry

    lax.fori_loop(0, tm, issue, 0, unroll=DMA_UNROLL)
    for k in range(2):
        pltpu.make_async_copy(x_ref, xs_ref.at[pl.ds(0, tm), :], sem).wait()


def moe_dispatch(x, dest, n_slots, tm):
    M, K = x.shape
    return pl.pallas_call(
        _dispatch_kernel,
        grid_spec=pltpu.PrefetchScalarGridSpec(
            num_scalar_prefetch=1,
            grid=(M // tm,),
            in_specs=[pl.BlockSpec((tm, K), lambda i, dest: (i, 0)),
                      pl.BlockSpec(memory_space=pl.ANY)],
            out_specs=pl.BlockSpec(memory_space=pl.ANY),
            scratch_shapes=[pltpu.SemaphoreType.DMA(())]),
        out_shape=jax.ShapeDtypeStruct((n_slots, K), F32),
        input_output_aliases={2: 0},
        compiler_params=_cparams(("arbitrary",)),
        name="moe_dispatch",
    )(dest, x, jnp.zeros((n_slots, K), F32))


def _experts_kernel(te_ref, tv_ref, xs_ref, gpre_ref, wg_ref, wu_ref, wd_ref, o_ref, h_ref, acc_ref):
    del te_ref
    i = pl.program_id(0)
    f = pl.program_id(1)
    last = pl.num_programs(1) - 1

    @pl.when(tv_ref[i] == 1)
    def _():
        @pl.when(f == 0)
        def _():
            h_ref[...] = _rms(xs_ref[...], gpre_ref[...]).astype(BF16)
            acc_ref[...] = jnp.zeros_like(acc_ref)

        hb = h_ref[...]
        act = jax.nn.silu(_dot(hb, wg_ref[0].astype(BF16))) * _dot(hb, wu_ref[0].astype(BF16))
        acc_ref[...] += _dot(act.astype(BF16), wd_ref[0].astype(BF16))

        @pl.when(f == last)
        def _():
            o_ref[...] = acc_ref[...]

    @pl.when((tv_ref[i] == 0) & (f == last))
    def _():
        o_ref[...] = jnp.zeros_like(o_ref)


def moe_experts(xs, gpre, tile_expert, tile_valid, wg, wu, wd, tg, tf):
    S, K = xs.shape
    F = wg.shape[2]
    nf = F // tf
    fidx = lambda i, f, te, tv: jnp.where(tv[i] == 1, f, nf - 1)
    return pl.pallas_call(
        _experts_kernel,
        grid_spec=pltpu.PrefetchScalarGridSpec(
            num_scalar_prefetch=2,
            grid=(S // tg, nf),
            in_specs=[pl.BlockSpec((tg, K), lambda i, f, te, tv: (i, 0)),
                      pl.BlockSpec((1, K), lambda i, f, te, tv: (0, 0)),
                      pl.BlockSpec((1, K, tf), lambda i, f, te, tv: (te[i], 0, fidx(i, f, te, tv))),
                      pl.BlockSpec((1, K, tf), lambda i, f, te, tv: (te[i], 0, fidx(i, f, te, tv))),
                      pl.BlockSpec((1, tf, K), lambda i, f, te, tv: (te[i], fidx(i, f, te, tv), 0))],
            out_specs=pl.BlockSpec((tg, K), lambda i, f, te, tv: (i, 0)),
            scratch_shapes=[pltpu.VMEM((tg, K), BF16), pltpu.VMEM((tg, K), F32)]),
        out_shape=jax.ShapeDtypeStruct((S, K), F32),
        compiler_params=_cparams(("parallel", "arbitrary")),
        name="moe_experts",
    )(tile_expert, tile_valid, xs, gpre.reshape(1, K), wg, wu, wd)


def _combine_kernel(dest_ref, x_ref, info_ref, gpost_ref, ys_ref, o_ref, buf_ref, sem):
    tm = x_ref.shape[0]
    i = pl.program_id(0)

    def gather(tile, slot):
        def issue(r, carry):
            for k in range(2):
                d = dest_ref[(tile * tm + r) * 2 + k]
                pltpu.make_async_copy(ys_ref.at[pl.ds(d, 1), :], buf_ref.at[slot, k, pl.ds(r, 1), :],
                                      sem.at[slot]).start()
            return carry

        lax.fori_loop(0, tm, issue, 0, unroll=DMA_UNROLL)

    @pl.when(i == 0)
    def _():
        gather(0, 0)

    @pl.when(i + 1 < pl.num_programs(0))
    def _():
        gather(i + 1, (i + 1) % 2)

    slot = i % 2
    for k in range(2):
        pltpu.make_async_copy(ys_ref.at[pl.ds(0, tm), :], buf_ref.at[slot, k], sem.at[slot]).wait()
    info = info_ref[...]
    y = info[:, INFO_W1:INFO_W1 + 1] * buf_ref[slot, 0] + info[:, INFO_W2:INFO_W2 + 1] * buf_ref[slot, 1]
    o_ref[...] = x_ref[...] + _rms(y, gpost_ref[...])


def moe_combine(x, info, gpost, ys, dest, tm):
    M, K = x.shape
    return pl.pallas_call(
        _combine_kernel,
        grid_spec=pltpu.PrefetchScalarGridSpec(
            num_scalar_prefetch=1,
            grid=(M // tm,),
            in_specs=[pl.BlockSpec((tm, K), lambda i, dest: (i, 0)),
                      pl.BlockSpec((tm, LANES), lambda i, dest: (i, 0)),
                      pl.BlockSpec((1, K), lambda i, dest: (0, 0)),
                      pl.BlockSpec(memory_space=pl.ANY)],
            out_specs=pl.BlockSpec((tm, K), lambda i, dest: (i, 0)),
            scratch_shapes=[pltpu.VMEM((2, 2, tm, K), F32), pltpu.SemaphoreType.DMA((2,))]),
        out_shape=jax.ShapeDtypeStruct((M, K), F32),
        compiler_params=_cparams(("arbitrary",)),
        name="moe_combine",
    )(dest, x, info, gpost.reshape(1, K), ys)


def moe_routed(x, gpre, gpost, router, wg, wu, wd):
    M, K = x.shape
    tg = TG_MOE
    n_slots = 2 * M + N_EXPERTS * tg
    info, cnt = moe_route(x, gpre, router, TM_ROUTE)
    cnt = cnt[0, :N_EXPERTS].astype(jnp.int32)
    padded = (cnt + tg - 1) // tg * tg
    ends = jnp.cumsum(padded)
    offs = ends - padded
    ids = info[:, INFO_E1:INFO_E2 + 1].astype(jnp.int32)
    ranks = info[:, INFO_R1:INFO_R2 + 1].astype(jnp.int32)
    dest = (offs[ids] + ranks).reshape(2 * M)
    tile_start = jnp.arange(n_slots // tg, dtype=jnp.int32) * tg
    tile_valid = (tile_start < ends[-1]).astype(jnp.int32)
    tile_expert = jnp.minimum(jnp.sum((tile_start[:, None] >= ends[None, :]).astype(jnp.int32), axis=1),
                              N_EXPERTS - 1)
    tile_expert = jnp.where(tile_valid == 1, tile_expert, tile_expert[jnp.maximum(ends[-1] // tg - 1, 0)])
    xs = moe_dispatch(x, dest, n_slots, TM_DISPATCH)
    ys = moe_experts(xs, gpre, tile_expert, tile_valid, wg, wu, wd, tg, TF_ROUTED)
    return moe_combine(x, info, gpost, ys, dest, TM_COMBINE)


def _xattn_kernel(x_ref, k_ref, v_ref, wq_ref, wo_ref, gpre_ref, gpost_ref, o_ref):
    x = x_ref[0]
    h = _rms(x, gpre_ref[...]).astype(BF16)
    q = _dot(h, wq_ref[...])
    k = k_ref[0, 0].astype(BF16)
    v = v_ref[0, 0].astype(BF16)
    heads = [slice(hd * X_HEAD_DIM, (hd + 1) * X_HEAD_DIM) for hd in range(X_HEADS)]
    ss = [_dot_nt(q[:, sl].astype(BF16), k[:, sl]) * (X_HEAD_DIM ** -0.5) for sl in heads]
    es = [jnp.exp(s - jnp.max(s, axis=-1, keepdims=True)) for s in ss]
    ps = [(e / jnp.sum(e, axis=-1, keepdims=True)).astype(BF16) for e in es]
    o = jnp.concatenate([_dot(p, v[:, sl]) for p, sl in zip(ps, heads)], axis=-1).astype(BF16)
    o_ref[0] = x + _rms(_dot(o, wo_ref[...]), gpost_ref[...])


def xattn_prompt(x, k, v, l, wq, wo, gpre, gpost, tm):
    B, T, K = x.shape
    return pl.pallas_call(
        _xattn_kernel,
        grid=(B, T // tm),
        in_specs=[pl.BlockSpec((1, tm, K), lambda b, t: (b, t, 0)),
                  pl.BlockSpec((1, 1, MEM_LEN, K), lambda b, t: (l, b, 0, 0)),
                  pl.BlockSpec((1, 1, MEM_LEN, K), lambda b, t: (l, b, 0, 0)),
                  pl.BlockSpec((K, K), lambda b, t: (0, 0)),
                  pl.BlockSpec((K, K), lambda b, t: (0, 0)),
                  pl.BlockSpec((1, K), lambda b, t: (0, 0)),
                  pl.BlockSpec((1, K), lambda b, t: (0, 0))],
        out_specs=pl.BlockSpec((1, tm, K), lambda b, t: (b, t, 0)),
        out_shape=jax.ShapeDtypeStruct((B, T, K), F32),
        compiler_params=_cparams(("parallel", "parallel")),
        name="xattn_prompt",
    )(x, k, v, wq, wo, gpre.reshape(1, K), gpost.reshape(1, K))


def _xattn_sample_kernel(q_ref, k_ref, v_ref, o_ref):
    mc = XATTN_MEM_CHUNK
    chunks = [slice(c * mc, (c + 1) * mc) for c in range(MEM_LEN // mc)]
    for j in range(q_ref.shape[0]):
        q = q_ref[j]
        s = jnp.concatenate([jnp.sum(k_ref[0, j, rows] * q[None], axis=-1, keepdims=True) for rows in chunks],
                            axis=0) * (X_HEAD_DIM ** -0.5)
        e = jnp.exp(s - jnp.max(s, axis=0, keepdims=True))
        p = e / jnp.sum(e, axis=0, keepdims=True)
        o = None
        for rows in chunks:
            part = jnp.sum(p[rows] * v_ref[0, j, rows], axis=0)
            o = part if o is None else o + part
        o_ref[j] = o


def xattn_sample_core(q, k, v, l, nb):
    B = q.shape[0]
    kv_spec = pl.BlockSpec((1, nb, MEM_LEN, X_HEADS, X_HEAD_DIM), lambda b: (l, b, 0, 0, 0))
    return pl.pallas_call(
        _xattn_sample_kernel,
        grid=(B // nb,),
        in_specs=[pl.BlockSpec((nb, X_HEADS, X_HEAD_DIM), lambda b: (b, 0, 0)), kv_spec, kv_spec],
        out_specs=pl.BlockSpec((nb, X_HEADS, X_HEAD_DIM), lambda b: (b, 0, 0)),
        out_shape=jax.ShapeDtypeStruct((B, X_HEADS, X_HEAD_DIM), F32),
        compiler_params=_cparams(("parallel",)),
        name="xattn_sample",
    )(q, k, v)


def _ev_proj_kernel(x_ref, g_ref, w_ref, lng_ref, lnb_ref, cw_ref, cb_ref,
                    gu_ref, vln_ref, sz_ref, xa_ref, dt_ref, tail_ref, xp_ref):
    t = pl.program_id(1)
    tm = x_ref.shape[1]

    @pl.when(t == 0)
    def _():
        xp_ref[0:SUBLANES, :] = jnp.zeros((SUBLANES, B_CONV_DIM), F32)

    h = _rms(x_ref[0], g_ref[...]).astype(BF16)
    u = _dot(h, w_ref[:, 0:D])
    v = _dot(h, w_ref[:, D:2 * D])
    gu_ref[0] = jax.nn.gelu(u)
    z = _dot(h, w_ref[:, 2 * D:3 * D])
    vln_ref[0] = _layernorm(jax.nn.gelu(v), lng_ref[...], lnb_ref[...]).astype(BF16)
    x = _dot(h, w_ref[:, 3 * D:3 * D + B_CONV_DIM])
    sz_ref[0] = jax.nn.silu(z)
    dt_ref[0] = _dot(h, w_ref[:, DT_COL_BLOCK * LANES:(DT_COL_BLOCK + 1) * LANES])
    xp_ref[SUBLANES:SUBLANES + tm, :] = x
    conv = cb_ref[...] + cw_ref[B_CONV - 1:B_CONV, :] * x
    for k in range(B_CONV - 1):
        conv = conv + cw_ref[k:k + 1, :] * xp_ref[pl.ds(SUBLANES - (B_CONV - 1) + k, tm), :]
    xp_ref[0:SUBLANES, :] = x[tm - SUBLANES:tm, :]
    tail_ref[0] = x[tm - SUBLANES:tm, :]
    xa_ref[0] = jax.nn.silu(conv)


def ev_proj(x, g, w, P, tm):
    B, T, K = x.shape
    row = lambda a: a.reshape(1, -1)
    full = lambda shape: pl.BlockSpec(shape, lambda b, t: (0,) * len(shape))
    tile = lambda n: pl.BlockSpec((1, tm, n), lambda b, t: (b, t, 0))
    return pl.pallas_call(
        _ev_proj_kernel,
        grid=(B, T // tm),
        in_specs=[tile(K), full((1, K)), full((K, IN0_PAD)), full((1, D)), full((1, D)),
                  full((B_CONV, B_CONV_DIM)), full((1, B_CONV_DIM))],
        out_specs=[tile(D), tile(D), tile(D), tile(B_CONV_DIM), tile(LANES),
                   pl.BlockSpec((1, SUBLANES, B_CONV_DIM), lambda b, t: (b, 0, 0))],
        out_shape=[jax.ShapeDtypeStruct((B, T, D), F32), jax.ShapeDtypeStruct((B, T, D), BF16),
                   jax.ShapeDtypeStruct((B, T, D), F32), jax.ShapeDtypeStruct((B, T, B_CONV_DIM), F32),
                   jax.ShapeDtypeStruct((B, T, LANES), F32), jax.ShapeDtypeStruct((B, SUBLANES, B_CONV_DIM), F32)],
        scratch_shapes=[pltpu.VMEM((SUBLANES + tm, B_CONV_DIM), F32)],
        compiler_params=_cparams(("parallel", "arbitrary")),
        name="ev_proj",
    )(x, row(g), w, row(P["a_ln_g"][0]), row(P["a_ln_b"][0]), P["b_conv_w"][0], row(P["b_conv_b"][0]))


def _mixer_even_kernel(gu_ref, vln_ref, sz_ref, xa_ref, dt_ref, ws_ref, bst_ref,
                       dtb_ref, anar_ref, dexp_ref, bnorm_ref, tril_ref, expand_ref,
                       yab_ref, ssm_ref, st_ref):
    c = pl.program_id(1)
    L = A_CHUNK

    @pl.when(c == 0)
    def _():
        st_ref[...] = jnp.zeros_like(st_ref)

    row = lax.broadcasted_iota(jnp.int32, (L, L), 0)
    col = lax.broadcasted_iota(jnp.int32, (L, L), 1)
    causal = col <= row

    tril = tril_ref[...]
    lane = lax.broadcasted_iota(jnp.int32, (L, LANES), 1)

    def chunk(rows):
        gu = gu_ref[0, rows, :]
        vb = vln_ref[0, rows, :]
        a_heads = [slice(hh * LANES, (hh + 1) * LANES) for hh in range(A_HEADS)]
        mixes = [_dot(jnp.where(causal, ws_ref[hh], 0.0).astype(BF16), vb[:, sl]) for hh, sl in enumerate(a_heads)]
        for hh, sl in enumerate(a_heads):
            yab_ref[0, rows, sl] = (gu[:, sl] * (mixes[hh] + bst_ref[:, hh:hh + 1])).astype(BF16)

        xa = xa_ref[0, rows, :]
        xs = xa[:, :D]
        bm = xa[:, D:D + B_GROUPS * B_STATE]
        cm = xa[:, D + B_GROUPS * B_STATE:]

        dtf = _softplus(dt_ref[0, rows, :] + dtb_ref[...])
        cs_n = _sel_left(tril, dtf * anar_ref[...], _split3)
        dt_x = _sel_right(dtf, expand_ref[...], _split2)
        cs_x = _sel_right(cs_n, expand_ref[...], _split3)
        ecs_x = jnp.exp(cs_x)
        last_x = cs_x[L - 1:L, :]
        xdt = xs * dt_x
        xdt_b = xdt.astype(BF16)
        xdec_b = (xdt * jnp.exp(last_x - cs_x)).astype(BF16)
        groups = range(B_GROUPS)
        gsl = [slice(g * B_GROUP_W, (g + 1) * B_GROUP_W) for g in groups]
        bgs = [bm[:, g * B_STATE:(g + 1) * B_STATE] for g in groups]
        cgs = [cm[:, g * B_STATE:(g + 1) * B_STATE].astype(BF16) for g in groups]
        gmats = [_dot_nt(cgs[g], bgs[g].astype(BF16)) for g in groups]
        sts = [st_ref[g] for g in groups]
        y_offs = [ecs_x[:, gsl[g]] * _dot(cgs[g], sts[g].astype(BF16)) for g in groups]
        for g in groups:
            st_ref[g] = sts[g] * jnp.exp(last_x[:, gsl[g]]) + _dot(bgs[g].T.astype(BF16), xdec_b[:, gsl[g]])
        pairs = [(g, pair) for g in groups for pair in range(B_GROUP_W // LANES)]
        bases = [g * B_GROUP_W + pair * LANES for g, pair in pairs]
        cs_ts = [cs_x[:, base:base + LANES].T for base in bases]
        lhs = []
        for (g, _), base, cs_t in zip(pairs, bases, cs_ts):
            ms = []
            for half in range(2):
                ch = half * B_HEAD_DIM
                diff = cs_x[:, base + ch:base + ch + 1] - cs_t[ch:ch + 1, :]
                ms.append((gmats[g] * jnp.exp(jnp.where(causal, diff, -jnp.inf))).astype(BF16))
            lhs.append(jnp.concatenate(ms, axis=1))
        y_parts = []
        for (g, pair), base, m2 in zip(pairs, bases, lhs):
            xp2 = xdt_b[:, base:base + LANES]
            rhs = jnp.concatenate([jnp.where(lane < B_HEAD_DIM, xp2, jnp.zeros_like(xp2)),
                                   jnp.where(lane >= B_HEAD_DIM, xp2, jnp.zeros_like(xp2))], axis=0)
            y_parts.append(_dot(m2, rhs) + y_offs[g][:, pair * LANES:(pair + 1) * LANES])
        y = jnp.concatenate(y_parts, axis=-1) + dexp_ref[...] * xs
        yb = _group_rms(y * sz_ref[0, rows, :], bnorm_ref[...], B_GROUP_W)
        yab_ref[0, rows, D:] = yb.astype(BF16)

    for sub in range(gu_ref.shape[1] // L):
        chunk(slice(sub * L, (sub + 1) * L))

    @pl.when(c == pl.num_programs(1) - 1)
    def _():
        for g in range(B_GROUPS):
            ssm_ref[0, g] = st_ref[g].T


def _even_consts(P):
    head_of_ch = jnp.arange(D) // B_HEAD_DIM
    expand = (jnp.arange(LANES)[:, None] == head_of_ch[None, :]).astype(BF16)
    tril = (jnp.arange(A_CHUNK)[None, :] <= jnp.arange(A_CHUNK)[:, None]).astype(BF16)
    aexp = jnp.repeat(-jnp.exp(P["b_a_log"][0].astype(F32)), B_HEAD_DIM).reshape(1, D)
    dexp = jnp.repeat(P["b_d"][0].astype(F32), B_HEAD_DIM).reshape(1, D)
    dtb = jnp.pad(P["b_dt_bias"][0].astype(F32), (0, LANES - B_HEADS)).reshape(1, LANES)
    return expand, tril, aexp, dexp, dtb


def mixer_even_prompt(gu, vln, sz, xa, dt, P):
    B, T, _ = gu.shape
    L = A_CHUNK
    expand, tril, _, dexp, dtb = _even_consts(P)
    anar = jnp.pad(-jnp.exp(P["b_a_log"][0].astype(F32)), (0, LANES - B_HEADS)).reshape(1, LANES)
    row = lambda a: a.reshape(1, -1)
    full = lambda shape: pl.BlockSpec(shape, lambda b, c: (0,) * len(shape))
    R = A_ROWS
    chunk = lambda n: pl.BlockSpec((1, R, n), lambda b, c: (b, c, 0))
    return pl.pallas_call(
        _mixer_even_kernel,
        grid=(B, T // R),
        in_specs=[chunk(D), chunk(D), chunk(D), chunk(B_CONV_DIM), chunk(LANES),
                  full((A_HEADS, L, L)), full((L, A_HEADS)), full((1, LANES)), full((1, LANES)), full((1, D)),
                  full((1, D)), full((L, L)), full((LANES, D))],
        out_specs=[pl.BlockSpec((1, R, 2 * D), lambda b, c: (b, c, 0)),
                   pl.BlockSpec((1, B_GROUPS, B_GROUP_W, B_STATE), lambda b, c: (b, 0, 0, 0))],
        out_shape=[jax.ShapeDtypeStruct((B, T, 2 * D), BF16),
                   jax.ShapeDtypeStruct((B, B_GROUPS, B_GROUP_W, B_STATE), F32)],
        scratch_shapes=[pltpu.VMEM((B_GROUPS, B_STATE, B_GROUP_W), F32)],
        compiler_params=_cparams(("parallel", "arbitrary")),
        name="mixer_even_prompt",
    )(gu, vln, sz, xa, dt, P["a_ws"][0], P["a_bs"][0].T, dtb, anar, dexp, row(P["b_norm"][0]), tril, expand)


def _mixer_even_step_kernel(proj_ref, conv_ref, ssm_ref, ws0_ref, bs0_ref, lng_ref, lnb_ref, cw_ref, cb_ref,
                            dtb_ref, aexp_ref, dexp_ref, bnorm_ref, expand_ref,
                            yab_ref, av_ref, convo_ref, ssmo_ref):
    nb = proj_ref.shape[0]
    u = proj_ref[:, 0:D]
    v = proj_ref[:, D:2 * D]
    z = proj_ref[:, 2 * D:3 * D]
    x = proj_ref[:, 3 * D:3 * D + B_CONV_DIM]
    dt = proj_ref[:, DT_COL_BLOCK * LANES:(DT_COL_BLOCK + 1) * LANES]

    vln = _layernorm(jax.nn.gelu(v), lng_ref[...], lnb_ref[...])
    av_ref[...] = vln
    yab_ref[:, 0:D] = jax.nn.gelu(u) * (ws0_ref[...] * vln + bs0_ref[...])

    conv = cb_ref[...] + cw_ref[B_CONV - 1:B_CONV, :] * x
    for k in range(B_CONV - 1):
        conv = conv + cw_ref[k:k + 1, :] * conv_ref[k]
        if k > 0:
            convo_ref[k - 1] = conv_ref[k]
    convo_ref[B_CONV - 2] = x
    xa = jax.nn.silu(conv)
    xs = xa[:, :D]
    bm = xa[:, D:D + B_GROUPS * B_STATE]
    cm = xa[:, D + B_GROUPS * B_STATE:]
    dtf = _softplus(dt + dtb_ref[...])
    dt_x = _sel_right(dtf, expand_ref[...], _split3)
    dec_x = jnp.exp(dt_x * aexp_ref[...])
    xdt = xs * dt_x
    y_rows = []
    for g in range(B_GROUPS):
        gs = slice(g * B_GROUP_W, (g + 1) * B_GROUP_W)
        dec_t = _rows_to_cols(dec_x[:, gs])
        xdt_t = _rows_to_cols(xdt[:, gs])
        s_news = [ssm_ref[j, g] * dec_t[:, j:j + 1] + xdt_t[:, j:j + 1] * bm[j:j + 1, g * B_STATE:(g + 1) * B_STATE]
                  for j in range(nb)]
        for j in range(nb):
            ssmo_ref[j, g] = s_news[j]
        cjs = [jnp.broadcast_to(cm[j:j + 1, g * B_STATE:(g + 1) * B_STATE], (SUBLANES, B_STATE)) for j in range(nb)]
        y_rows.append(jnp.concatenate([_dot_nt_f32(cjs[j], s_news[j])[0:1, :] for j in range(nb)], axis=0))
    y = jnp.concatenate(y_rows, axis=-1) + dexp_ref[...] * xs
    yb = _group_rms(y * jax.nn.silu(z), bnorm_ref[...], B_GROUP_W)
    yab_ref[:, D:] = yb


def mixer_even_step(proj, conv0, ssm0, P, nb):
    B = proj.shape[0]
    expand, _, aexp, dexp, dtb = _even_consts(P)
    row = lambda a: a.reshape(1, -1)
    rep = lambda a: jnp.repeat(a.astype(F32), LANES).reshape(1, D)
    full = lambda shape: pl.BlockSpec(shape, lambda i: (0,) * len(shape))
    return pl.pallas_call(
        _mixer_even_step_kernel,
        grid=(B // nb,),
        in_specs=[pl.BlockSpec((nb, proj.shape[1]), lambda i: (i, 0)),
                  pl.BlockSpec((B_CONV - 1, nb, B_CONV_DIM), lambda i: (0, i, 0)),
                  pl.BlockSpec((nb, B_GROUPS, B_GROUP_W, B_STATE), lambda i: (i, 0, 0, 0)),
                  full((1, D)), full((1, D)), full((1, D)), full((1, D)),
                  full((B_CONV, B_CONV_DIM)), full((1, B_CONV_DIM)), full((1, LANES)), full((1, D)), full((1, D)),
                  full((1, D)), full((LANES, D))],
        out_specs=[pl.BlockSpec((nb, 2 * D), lambda i: (i, 0)),
                   pl.BlockSpec((nb, D), lambda i: (i, 0)),
                   pl.BlockSpec((B_CONV - 1, nb, B_CONV_DIM), lambda i: (0, i, 0)),
                   pl.BlockSpec((nb, B_GROUPS, B_GROUP_W, B_STATE), lambda i: (i, 0, 0, 0))],
        out_shape=[jax.ShapeDtypeStruct((B, 2 * D), F32),
                   jax.ShapeDtypeStruct((B, D), F32),
                   jax.ShapeDtypeStruct((B_CONV - 1, B, B_CONV_DIM), F32),
                   jax.ShapeDtypeStruct((B, B_GROUPS, B_GROUP_W, B_STATE), F32)],
        compiler_params=_cparams(("parallel",)),
        name="mixer_even_step",
    )(proj, conv0, ssm0, rep(P["a_ws"][0][:, 0, 0]), rep(P["a_bs"][0][:, 0]), row(P["a_ln_g"][0]), row(P["a_ln_b"][0]),
      P["b_conv_w"][0], row(P["b_conv_b"][0]), dtb, aexp, dexp, row(P["b_norm"][0]), expand)


def _gla_gates(q_raw, f_raw, lb):
    fg = lb + (1.0 - lb) * jax.nn.sigmoid(f_raw)
    return jax.nn.silu(q_raw), fg, 1.0 - fg


def _hgrn_kernel(q_ref, f_ref, i_ref, g_ref, lb_ref, cnorm_ref, tril_ref, o_ref, s_ref, st_ref):
    c = pl.program_id(1)
    L = C_CHUNK
    R = C_ROWS

    @pl.when(c == 0)
    def _():
        st_ref[...] = jnp.zeros_like(st_ref)

    row = lax.broadcasted_iota(jnp.int32, (R, R), 0)
    col = lax.broadcasted_iota(jnp.int32, (R, R), 1)
    causal = (col <= row) & (row // L == col // L)
    chunk_of_row = lax.broadcasted_iota(jnp.int32, (R, 1), 0) // L
    tril = tril_ref[...]
    heads = [slice(hh * C_KDIM, (hh + 1) * C_KDIM) for hh in range(C_HEADS)]

    def block(rows):
        q, fg, k = _gla_gates(q_ref[0, rows, :], f_ref[0, rows, :], lb_ref[...])
        v = i_ref[0, rows, :]
        bc = _sel_left(tril, jnp.log(fg), _split3)
        q_in = (q * jnp.exp(bc)).astype(BF16)
        k_in = (k * jnp.exp(-bc)).astype(BF16)
        vb = v.astype(BF16)
        btots, k_decs = [], []
        for s in range(R // L):
            btot = bc[(s + 1) * L - 1:(s + 1) * L, :]
            btots.append(btot)
            k_decs.append(jnp.where(chunk_of_row == s, k * jnp.exp(btot - bc), 0.0).astype(BF16))
        atts = [jnp.where(causal, _dot_nt(q_in[:, sl], k_in[:, sl]), 0.0).astype(BF16) for sl in heads]
        v_ts = [v[:, sl].T.astype(BF16) for sl in heads]
        sts = [st_ref[hh] for hh in range(C_HEADS)]
        inters = [[] for _ in heads]
        for s in range(R // L):
            for hh, sl in enumerate(heads):
                inters[hh].append(_dot_nt(q_in[s * L:(s + 1) * L, sl], sts[hh].astype(BF16)))
            sts = [sts[hh] * jnp.exp(btots[s][:, sl]) + _dot(v_ts[hh], k_decs[s][:, sl])
                   for hh, sl in enumerate(heads)]
        for hh in range(C_HEADS):
            st_ref[hh] = sts[hh]
        outs = [_dot(atts[hh], vb[:, sl]) + jnp.concatenate(inters[hh], axis=0) for hh, sl in enumerate(heads)]
        o = _group_rms(jnp.concatenate(outs, axis=-1), cnorm_ref[...], C_KDIM)
        o_ref[0, rows, :] = (o * jax.nn.silu(g_ref[0, rows, :])).astype(BF16)

    for blk in range(q_ref.shape[1] // R):
        block(slice(blk * R, (blk + 1) * R))

    @pl.when(c == pl.num_programs(1) - 1)
    def _():
        for hh in range(C_HEADS):
            s_ref[0, hh] = st_ref[hh].T


def hgrn_prompt(proj, lb, cnorm):
    B, T, _ = proj.shape
    L = C_ROWS
    r = jnp.arange(L)
    tril = ((r[None, :] <= r[:, None]) & (r[None, :] // C_CHUNK == r[:, None] // C_CHUNK)).astype(BF16)
    full = lambda shape: pl.BlockSpec(shape, lambda b, c: (0,) * len(shape))
    S = C_STEP_ROWS
    return pl.pallas_call(
        _hgrn_kernel,
        grid=(B, T // S),
        in_specs=[pl.BlockSpec((1, S, D), lambda b, c: (b, c, 0)),
                  pl.BlockSpec((1, S, D), lambda b, c: (b, c, 1)),
                  pl.BlockSpec((1, S, D), lambda b, c: (b, c, 2)),
                  pl.BlockSpec((1, S, D), lambda b, c: (b, c, 3)),
                  full((1, D)), full((1, D)), full((L, L))],
        out_specs=[pl.BlockSpec((1, S, D), lambda b, c: (b, c, 0)),
                   pl.BlockSpec((1, C_HEADS, C_KDIM, C_KDIM), lambda b, c: (b, 0, 0, 0))],
        out_shape=[jax.ShapeDtypeStruct((B, T, D), BF16),
                   jax.ShapeDtypeStruct((B, C_HEADS, C_KDIM, C_KDIM), F32)],
        scratch_shapes=[pltpu.VMEM((C_HEADS, C_KDIM, C_KDIM), F32)],
        compiler_params=_cparams(("parallel", "arbitrary")),
        name="hgrn_prompt",
    )(proj, proj, proj, proj, lb.reshape(1, D), cnorm.reshape(1, D), tril)


def _hgrn_step_kernel(proj_ref, s_ref, lb_ref, cnorm_ref, o_ref, so_ref):
    nb = proj_ref.shape[0]
    q, fg, k = _gla_gates(proj_ref[:, 0:D], proj_ref[:, D:2 * D], lb_ref[...])
    v = proj_ref[:, 2 * D:3 * D]
    g = proj_ref[:, 3 * D:4 * D]
    heads = [slice(hh * C_KDIM, (hh + 1) * C_KDIM) for hh in range(C_HEADS)]
    fg_ts = [_rows_to_cols(fg[:, sl]) for sl in heads]
    k_ts = [_rows_to_cols(k[:, sl]) for sl in heads]
    outs = []
    for hh, sl in enumerate(heads):
        s_news = [s_ref[j, hh] * fg_ts[hh][:, j:j + 1] + k_ts[hh][:, j:j + 1] * v[j:j + 1, sl] for j in range(nb)]
        for j in range(nb):
            so_ref[j, hh] = s_news[j]
        qjs = [jnp.broadcast_to(q[j:j + 1, sl], (SUBLANES, C_KDIM)) for j in range(nb)]
        outs.append(jnp.concatenate([_dot_f32(qjs[j], s_news[j])[0:1, :] for j in range(nb)], axis=0))
    o = _group_rms(jnp.concatenate(outs, axis=-1), cnorm_ref[...], C_KDIM)
    o_ref[...] = o * jax.nn.silu(g)


def hgrn_step(proj, s0, lb, cnorm, nb):
    B = proj.shape[0]
    full = lambda shape: pl.BlockSpec(shape, lambda i: (0,) * len(shape))
    return pl.pallas_call(
        _hgrn_step_kernel,
        grid=(B // nb,),
        in_specs=[pl.BlockSpec((nb, 4 * D), lambda i: (i, 0)),
                  pl.BlockSpec((nb, C_HEADS, C_KDIM, C_KDIM), lambda i: (i, 0, 0, 0)),
                  full((1, D)), full((1, D))],
        out_specs=[pl.BlockSpec((nb, D), lambda i: (i, 0)),
                   pl.BlockSpec((nb, C_HEADS, C_KDIM, C_KDIM), lambda i: (i, 0, 0, 0))],
        out_shape=[jax.ShapeDtypeStruct((B, D), F32),
                   jax.ShapeDtypeStruct((B, C_HEADS, C_KDIM, C_KDIM), F32)],
        compiler_params=_cparams(("parallel",)),
        name="hgrn_step",
    )(proj, s0, lb.reshape(1, D), cnorm.reshape(1, D))


TM_PROJ = 512
TM_EV_PROJ = 256
TM_OUT = 1024
TM_ATTN = 512
TM_FFN = 1024
TF_DENSE = 256
TF_MOE = 896
TF_ROUTED = 512
TG_MOE = 1024
TM_ROUTE = 1024
ROUTE_ROWS = 256
TM_DISPATCH = 1024
TM_COMBINE = 512
DMA_UNROLL = 8
ROUTED_MIN_TOKENS = 8 * TG_MOE
STEP_NB = 8
XATTN_STEP_NB = 4
XATTN_MEM_CHUNK = 64
TN_STEP = 512
TK_STEP = 512
IN0_STEP_PAD = 5120


def _prep_weights(P):
    W = {}
    W["ev_w_in"] = jnp.pad(P["ev_w_in"][0], ((0, 0), (0, IN0_PAD - IN0))).astype(BF16)
    W["ev_w_out"] = P["ev_w_out"][0].astype(BF16)
    W["od_w_in"] = P["od_w_in"][0].astype(BF16)
    W["od_w_out"] = P["od_w_out"][0].astype(BF16)
    W["xa_wq"] = P["xa_wq"].astype(BF16)
    W["xa_wo"] = P["xa_wo"].astype(BF16)
    W["xa_wk"] = P["xa_wk"].astype(BF16)
    W["xa_wv"] = P["xa_wv"].astype(BF16)
    W["ffn"] = tuple(P[n].astype(BF16) for n in ("ffn_w_gate", "ffn_w_up", "ffn_w_down"))
    W["moe"] = tuple(P[n][0] for n in ("moe_w_gate", "moe_w_up", "moe_w_down"))
    W["router"] = jnp.pad(P["moe_router"][0].astype(F32), ((0, 0), (0, LANES - N_EXPERTS)))
    lbp = jax.nn.softmax(P["hgrn_lb_logits"].astype(F32), axis=0)
    W["lower_bounds"] = jnp.cumsum(lbp, axis=0) - lbp[0]
    return W


def _channel_mix(x2, l, P, W, tm):
    if l == 0:
        wg, wu, wd = W["ffn"]
        return ffn(x2, P["norm_ffn_pre"][l], P["norm_ffn_post"][l], W["router"], wg, wu, wd, tm, TF_DENSE, False)
    wg, wu, wd = W["moe"]
    if x2.shape[0] >= ROUTED_MIN_TOKENS:
        return moe_routed(x2, P["norm_ffn_pre"][l], P["norm_ffn_post"][l], W["router"], wg, wu, wd)
    return ffn(x2, P["norm_ffn_pre"][l], P["norm_ffn_post"][l], W["router"], wg, wu, wd, tm, TF_MOE, True)


def _trunk_prompt(x, mem_k, mem_v, P, W):
    B, T, _ = x.shape
    M = B * T
    x2 = x.reshape(M, D)
    gu, vln, sz, xa, dt, tail = ev_proj(x, P["norm_mix_pre"][0], W["ev_w_in"], P, TM_EV_PROJ)
    yab, ssm = mixer_even_prompt(gu, vln, sz, xa, dt, P)
    conv = tail[:, SUBLANES - (B_CONV - 1):, :]
    x2 = matmul_norm_res(yab.reshape(M, 2 * D), W["ev_w_out"], P["norm_mix_post"][0], x2, TM_OUT)
    x2 = xattn_prompt(x2.reshape(B, T, D), mem_k, mem_v, 0, W["xa_wq"][0], W["xa_wo"][0],
                      P["norm_x_pre"][0], P["norm_x_post"][0], TM_ATTN).reshape(M, D)
    x2 = _channel_mix(x2, 0, P, W, TM_FFN)
    proj = norm_matmul(x2, P["norm_mix_pre"][1], W["od_w_in"], TM_PROJ).reshape(B, T, 4 * D)
    o, hgrn = hgrn_prompt(proj, W["lower_bounds"][1], P["c_norm"][0])
    x2 = matmul_norm_res(o.reshape(M, D), W["od_w_out"], P["norm_mix_post"][1], x2, TM_OUT)
    x2 = xattn_prompt(x2.reshape(B, T, D), mem_k, mem_v, 1, W["xa_wq"][1], W["xa_wo"][1],
                      P["norm_x_pre"][1], P["norm_x_post"][1], TM_ATTN).reshape(M, D)
    x2 = _channel_mix(x2, 1, P, W, TM_FFN)
    return x2.reshape(B, T, D), conv, ssm, hgrn


def _xattn_step(x2, l, mem_k, mem_v, P):
    B = x2.shape[0]
    q = norm_matmul(x2, P["norm_x_pre"][l], P["xa_wq"][l], B, TN_STEP)
    o = xattn_sample_core(q.reshape(B, X_HEADS, X_HEAD_DIM), mem_k, mem_v, l, XATTN_STEP_NB).reshape(B, D)
    return matmul_norm_res(o, P["xa_wo"][l], P["norm_x_post"][l], x2, B)


def _trunk_step(x, mem_k, mem_v, conv0, ssm0, hgrn0, P, W):
    B = x.shape[0]
    x2 = x.reshape(B, D)
    w_in = jnp.pad(P["ev_w_in"][0], ((0, 0), (0, IN0_STEP_PAD - IN0)))
    proj = norm_matmul(x2, P["norm_mix_pre"][0], w_in, B, TN_STEP)
    yab, av, conv, ssm = mixer_even_step(proj, jnp.swapaxes(conv0, 0, 1), ssm0, P, STEP_NB)
    x2 = matmul_norm_res(yab, P["ev_w_out"][0], P["norm_mix_post"][0], x2, B, TK_STEP)
    x2 = _xattn_step(x2, 0, mem_k, mem_v, P)
    x2 = ffn(x2, P["norm_ffn_pre"][0], P["norm_ffn_post"][0], W["router"],
             P["ffn_w_gate"], P["ffn_w_up"], P["ffn_w_down"], B, TF_DENSE, False, precise=True)
    proj = norm_matmul(x2, P["norm_mix_pre"][1], P["od_w_in"][0], B, TN_STEP)
    o, hgrn = hgrn_step(proj, hgrn0, W["lower_bounds"][1], P["c_norm"][0], STEP_NB)
    x2 = matmul_norm_res(o, P["od_w_out"][0], P["norm_mix_post"][1], x2, B)
    x2 = _xattn_step(x2, 1, mem_k, mem_v, P)
    x2 = _channel_mix(x2, 1, P, W, B)
    return x2.reshape(B, 1, D), jnp.swapaxes(conv, 0, 1), ssm, hgrn, av


def kernel(x_prompt, x_sample, mem_prompt, cache_mem_k, cache_mem_v, state_conv, state_ssm, state_hgrn,
           norm_mix_pre, norm_mix_post, norm_x_pre, norm_x_post, norm_ffn_pre, norm_ffn_post, norm_mem,
           xa_wq, xa_wk, xa_wv, xa_wo,
           ev_w_in, a_ws, a_bs, a_ln_g, a_ln_b, b_conv_w, b_conv_b, b_dt_bias, b_a_log, b_d, b_norm, ev_w_out,
           ffn_w_gate, ffn_w_up, ffn_w_down,
           od_w_in, hgrn_lb_logits, c_norm, od_w_out,
           moe_router, moe_w_gate, moe_w_up, moe_w_down):
    P = dict(norm_mix_pre=norm_mix_pre, norm_mix_post=norm_mix_post, norm_x_pre=norm_x_pre, norm_x_post=norm_x_post,
             norm_ffn_pre=norm_ffn_pre, norm_ffn_post=norm_ffn_post, xa_wq=xa_wq, xa_wk=xa_wk, xa_wv=xa_wv,
             xa_wo=xa_wo, ev_w_in=ev_w_in, a_ws=a_ws, a_bs=a_bs, a_ln_g=a_ln_g, a_ln_b=a_ln_b, b_conv_w=b_conv_w,
             b_conv_b=b_conv_b, b_dt_bias=b_dt_bias, b_a_log=b_a_log, b_d=b_d, b_norm=b_norm, ev_w_out=ev_w_out,
             ffn_w_gate=ffn_w_gate, ffn_w_up=ffn_w_up, ffn_w_down=ffn_w_down, od_w_in=od_w_in,
             hgrn_lb_logits=hgrn_lb_logits, c_norm=c_norm, od_w_out=od_w_out, moe_router=moe_router,
             moe_w_gate=moe_w_gate, moe_w_up=moe_w_up, moe_w_down=moe_w_down)
    W = _prep_weights(P)
    depth = norm_mem.shape[0]
    bp, T, _ = x_prompt.shape
    bs = x_sample.shape[0]

    mem_k_p, mem_v_p, mem_k_out, mem_v_out = mem_kv(mem_prompt.reshape(bp * MEM_LEN, D), norm_mem,
                                                    W["xa_wk"], W["xa_wv"], TM_PROJ)
    mem_k_p = mem_k_p.reshape(depth, bp, MEM_LEN, D)
    mem_v_p = mem_v_p.reshape(depth, bp, MEM_LEN, D)
    y_p, conv_p, ssm_p, hgrn_p = _trunk_prompt(x_prompt, mem_k_p, mem_v_p, P, W)

    y_s, conv_s, ssm_s, hgrn_s, av_s = _trunk_step(
        x_sample, cache_mem_k, cache_mem_v,
        state_conv[0], state_ssm[0].reshape(bs, B_GROUPS, B_GROUP_W, B_STATE), state_hgrn[0], P, W)

    kv_shape = (depth, bp, MEM_LEN, X_HEADS, X_HEAD_DIM)
    ssm_shape = (B_GROUPS, B_GROUP_W // B_HEAD_DIM, B_HEAD_DIM, B_STATE)
    return (y_p, y_s, mem_k_out.reshape(kv_shape), mem_v_out.reshape(kv_shape),
            conv_p[None], ssm_p.reshape((1, bp) + ssm_shape), hgrn_p[None],
            conv_s[None], ssm_s.reshape((1, bs) + ssm_shape), hgrn_s[None], av_s.reshape(1, bs, 1, D))
```

```python
import functools

import jax
import jax.numpy as jnp
from jax import lax
from jax.experimental import pallas as pl
from jax.experimental.pallas import tpu as pltpu

F32 = jnp.float32
BF16 = jnp.bfloat16
EPS = 1e-6

D = 1024
LANES = 128
SUBLANES = 8
A_HEADS = 8
A_CHUNK = 128
A_ROWS = 512
B_HEADS = 16
B_HEAD_DIM = 64
B_GROUPS = 2
B_GROUP_W = 512
B_STATE = 128
B_CONV = 4
B_CONV_DIM = 1536
IN0 = 4624
IN0_PAD = 4736
DT_COL_BLOCK = 36
C_HEADS = 8
C_KDIM = 128
C_CHUNK = 64
C_ROWS = 128
C_STEP_ROWS = 512
X_HEADS = 4
X_HEAD_DIM = 256
MEM_LEN = 256
N_EXPERTS = 8

VMEM_LIMIT = 56 * 1024 * 1024


def _cparams(sem):
    return pltpu.CompilerParams(dimension_semantics=sem, vmem_limit_bytes=VMEM_LIMIT)


def _dot(a, b):
    return jnp.dot(a, b, preferred_element_type=F32)


def _dot_nt(a, b):
    return lax.dot_general(a, b, (((1,), (1,)), ((), ())), preferred_element_type=F32)


def _dot_f32(a, b):
    return jnp.dot(a, b, precision=lax.Precision.HIGHEST, preferred_element_type=F32)


def _dot_nt_f32(a, b):
    return lax.dot_general(a, b, (((1,), (1,)), ((), ())), precision=lax.Precision.HIGHEST,
                           preferred_element_type=F32)


def _mm(a, w):
    if w.dtype == F32:
        return _dot_f32(a.astype(F32), w)
    return _dot(a.astype(BF16), w)


def _rms(x, g):
    return x * lax.rsqrt(jnp.mean(x * x, axis=-1, keepdims=True) + EPS) * g


def _split2(x):
    hi = x.astype(BF16)
    lo = (x - hi.astype(F32)).astype(BF16)
    return hi, lo


def _split3(x):
    hi = x.astype(BF16)
    r = x - hi.astype(F32)
    mid = r.astype(BF16)
    lo = (r - mid.astype(F32)).astype(BF16)
    return hi, mid, lo


def _sel_left(m, x, parts):
    out = None
    for p in parts(x):
        t = _dot(m, p)
        out = t if out is None else out + t
    return out


def _sel_right(x, m, parts):
    out = None
    for p in parts(x):
        t = _dot(p, m)
        out = t if out is None else out + t
    return out


def _rows_to_cols(x):
    n, w = x.shape
    if n < LANES:
        x = jnp.concatenate([x, jnp.zeros((LANES - n, w), x.dtype)], axis=0)
    return x.T


def _softplus(x):
    return jnp.maximum(x, 0.0) + jnp.log1p(jnp.exp(-jnp.abs(x)))


def _layernorm(x, g, b):
    xc = x - jnp.mean(x, axis=-1, keepdims=True)
    return xc * lax.rsqrt(jnp.mean(xc * xc, axis=-1, keepdims=True) + EPS) * g + b


def _group_rms(x, g, width):
    parts = []
    for s in range(0, x.shape[-1], width):
        t = x[:, s:s + width]
        parts.append(t * lax.rsqrt(jnp.mean(t * t, axis=-1, keepdims=True) + EPS))
    return jnp.concatenate(parts, axis=-1) * g


def _norm_matmul_kernel(x_ref, g_ref, w_ref, o_ref):
    o_ref[...] = _mm(_rms(x_ref[...], g_ref[...]), w_ref[...])


def norm_matmul(x, g, w, tm, tn=None):
    M, K = x.shape
    N = w.shape[1]
    tn = N if tn is None else tn
    return pl.pallas_call(
        _norm_matmul_kernel,
        grid=(M // tm, N // tn),
        in_specs=[pl.BlockSpec((tm, K), lambda i, j: (i, 0)),
                  pl.BlockSpec((1, K), lambda i, j: (0, 0)),
                  pl.BlockSpec((K, tn), lambda i, j: (0, j))],
        out_specs=pl.BlockSpec((tm, tn), lambda i, j: (i, j)),
        out_shape=jax.ShapeDtypeStruct((M, N), F32),
        compiler_params=_cparams(("parallel", "parallel")),
        name="norm_matmul",
    )(x, g.reshape(1, K), w)


def _mem_kv_kernel(x_ref, g_ref, wk_ref, wv_ref, k_ref, v_ref, k5_ref, v5_ref):
    h = _rms(x_ref[...], g_ref[0]).astype(BF16)
    for w_ref, o_ref, o5_ref in ((wk_ref, k_ref, k5_ref), (wv_ref, v_ref, v5_ref)):
        r = _dot(h, w_ref[0])
        o_ref[0] = r
        for hd in range(X_HEADS):
            o5_ref[0, :, hd, :] = r[:, hd * X_HEAD_DIM:(hd + 1) * X_HEAD_DIM]


def mem_kv(mem, g, wk, wv, tm):
    M, K = mem.shape
    depth = g.shape[0]
    wspec = pl.BlockSpec((1, K, K), lambda l, i: (l, 0, 0))
    ospec = pl.BlockSpec((1, tm, K), lambda l, i: (l, i, 0))
    o5spec = pl.BlockSpec((1, tm, X_HEADS, X_HEAD_DIM), lambda l, i: (l, i, 0, 0))
    return pl.pallas_call(
        _mem_kv_kernel,
        grid=(depth, M // tm),
        in_specs=[pl.BlockSpec((tm, K), lambda l, i: (i, 0)),
                  pl.BlockSpec((1, 1, K), lambda l, i: (l, 0, 0)), wspec, wspec],
        out_specs=[ospec, ospec, o5spec, o5spec],
        out_shape=[jax.ShapeDtypeStruct((depth, M, K), F32)] * 2
        + [jax.ShapeDtypeStruct((depth, M, X_HEADS, X_HEAD_DIM), F32)] * 2,
        compiler_params=_cparams(("parallel", "parallel")),
        name="mem_kv",
    )(mem, g.reshape(depth, 1, K), wk, wv)


def _matmul_norm_res_kernel(a_ref, w_ref, g_ref, r_ref, o_ref, acc_ref):
    k = pl.program_id(1)

    @pl.when(k == 0)
    def _():
        acc_ref[...] = jnp.zeros_like(acc_ref)

    acc_ref[...] += _mm(a_ref[...], w_ref[...])

    @pl.when(k == pl.num_programs(1) - 1)
    def _():
        o_ref[...] = r_ref[...] + _rms(acc_ref[...], g_ref[...])


def _matmul_norm_res_whole_k_kernel(a_ref, w_ref, g_ref, r_ref, o_ref):
    o_ref[...] = r_ref[...] + _rms(_mm(a_ref[...], w_ref[...]), g_ref[...])


def matmul_norm_res(a, w, g, res, tm, tk=None):
    M, K = a.shape
    N = w.shape[1]
    tk = K if tk is None else tk
    if tk == K:
        return pl.pallas_call(
            _matmul_norm_res_whole_k_kernel,
            grid=(M // tm,),
            in_specs=[pl.BlockSpec((tm, K), lambda i: (i, 0)),
                      pl.BlockSpec((K, N), lambda i: (0, 0)),
                      pl.BlockSpec((1, N), lambda i: (0, 0)),
                      pl.BlockSpec((tm, N), lambda i: (i, 0))],
            out_specs=pl.BlockSpec((tm, N), lambda i: (i, 0)),
            out_shape=jax.ShapeDtypeStruct((M, N), F32),
            compiler_params=_cparams(("parallel",)),
            name="matmul_norm_res",
        )(a, w, g.reshape(1, N), res)
    return pl.pallas_call(
        _matmul_norm_res_kernel,
        grid=(M // tm, K // tk),
        in_specs=[pl.BlockSpec((tm, tk), lambda i, k: (i, k)),
                  pl.BlockSpec((tk, N), lambda i, k: (k, 0)),
                  pl.BlockSpec((1, N), lambda i, k: (0, 0)),
                  pl.BlockSpec((tm, N), lambda i, k: (i, 0))],
        out_specs=pl.BlockSpec((tm, N), lambda i, k: (i, 0)),
        out_shape=jax.ShapeDtypeStruct((M, N), F32),
        scratch_shapes=[pltpu.VMEM((tm, N), F32)],
        compiler_params=_cparams(("parallel", "arbitrary")),
        name="matmul_norm_res",
    )(a, w, g.reshape(1, N), res)


def _top2(h, router, three_pass=False):
    if three_pass:
        hh, hl = _split2(h)
        rh, rl = _split2(router)
        lg = _dot(hh, rh) + _dot(hh, rl) + _dot(hl, rh)
    else:
        lg = _dot_f32(h, router)
    lane = lax.broadcasted_iota(jnp.int32, lg.shape, 1)
    lg = jnp.where(lane < N_EXPERTS, lg, -jnp.inf)
    m1 = jnp.max(lg, axis=-1, keepdims=True)
    i1 = jnp.min(jnp.where(lg == m1, lane, LANES), axis=-1, keepdims=True)
    lg2 = jnp.where(lane == i1, -jnp.inf, lg)
    m2 = jnp.max(lg2, axis=-1, keepdims=True)
    i2 = jnp.min(jnp.where(lg2 == m2, lane, LANES), axis=-1, keepdims=True)
    e2 = jnp.exp(m2 - m1)
    den = 1.0 + e2
    return lane, i1, i2, 1.0 / den, e2 / den


def _top2_gates(h, router):
    lane, i1, i2, w1, w2 = _top2(h, router)
    return jnp.where(lane == i1, w1, 0.0) + jnp.where(lane == i2, w2, 0.0)


def _ffn_kernel(x_ref, gpre_ref, gpost_ref, router_ref, wg_ref, wu_ref, wd_ref, o_ref,
                h_ref, acc_ref, gates_ref, *, moe):
    e = pl.program_id(1)
    f = pl.program_id(2)

    @pl.when((e == 0) & (f == 0))
    def _():
        h = _rms(x_ref[...], gpre_ref[...])
        h_ref[...] = h.astype(h_ref.dtype)
        acc_ref[...] = jnp.zeros_like(acc_ref)
        if moe:
            gates_ref[...] = _top2_gates(h, router_ref[...])
        else:
            gates_ref[...] = jnp.zeros_like(gates_ref)

    hb = h_ref[...]
    act = jax.nn.silu(_mm(hb, wg_ref[0].astype(hb.dtype))) * _mm(hb, wu_ref[0].astype(hb.dtype))
    if moe:
        gates = gates_ref[...]
        lane = lax.broadcasted_iota(jnp.int32, gates.shape, 1)
        act = act * jnp.sum(jnp.where(lane == e, gates, 0.0), axis=-1, keepdims=True)
    acc_ref[...] += _mm(act, wd_ref[0].astype(hb.dtype))

    @pl.when((e == pl.num_programs(1) - 1) & (f == pl.num_programs(2) - 1))
    def _():
        o_ref[...] = x_ref[...] + _rms(acc_ref[...], gpost_ref[...])


def ffn(x, gpre, gpost, router, wg, wu, wd, tm, tf, moe, precise=False):
    M, K = x.shape
    E, _, F = wg.shape
    return pl.pallas_call(
        functools.partial(_ffn_kernel, moe=moe),
        grid=(M // tm, E, F // tf),
        in_specs=[pl.BlockSpec((tm, K), lambda i, e, f: (i, 0)),
                  pl.BlockSpec((1, K), lambda i, e, f: (0, 0)),
                  pl.BlockSpec((1, K), lambda i, e, f: (0, 0)),
                  pl.BlockSpec((K, LANES), lambda i, e, f: (0, 0)),
                  pl.BlockSpec((1, K, tf), lambda i, e, f: (e, 0, f)),
                  pl.BlockSpec((1, K, tf), lambda i, e, f: (e, 0, f)),
                  pl.BlockSpec((1, tf, K), lambda i, e, f: (e, f, 0))],
        out_specs=pl.BlockSpec((tm, K), lambda i, e, f: (i, 0)),
        out_shape=jax.ShapeDtypeStruct((M, K), F32),
        scratch_shapes=[pltpu.VMEM((tm, K), F32 if precise else BF16), pltpu.VMEM((tm, K), F32), pltpu.VMEM((tm, LANES), F32)],
        compiler_params=_cparams(("parallel", "arbitrary", "arbitrary")),
        name="moe_ffn" if moe else "dense_ffn",
    )(x, gpre.reshape(1, K), gpost.reshape(1, K), router, wg, wu, wd)


INFO_E1, INFO_E2, INFO_R1, INFO_R2, INFO_W1, INFO_W2 = range(6)


def _route_kernel(x_ref, gpre_ref, router_ref, tril_ref, info_ref, cnt_ref, carry_ref):
    i = pl.program_id(0)

    @pl.when(i == 0)
    def _():
        carry_ref[...] = jnp.zeros_like(carry_ref)

    rb = tril_ref.shape[0]
    blocks = [slice(r, r + rb) for r in range(0, x_ref.shape[0], rb)]
    tops = [_top2(_rms(x_ref[rows, :], gpre_ref[...]), router_ref[...], three_pass=True) for rows in blocks]
    sels = [jnp.where((lane == i1) | (lane == i2), 1.0, 0.0) for lane, i1, i2, _, _ in tops]
    incls = [_dot(tril_ref[...], sel.astype(BF16)) for sel in sels]
    base = carry_ref[...]
    for rows, (lane, i1, i2, w1, w2), sel, incl in zip(blocks, tops, sels, incls):
        rank = incl - sel + base
        r1 = jnp.sum(jnp.where(lane == i1, rank, 0.0), axis=-1, keepdims=True)
        r2 = jnp.sum(jnp.where(lane == i2, rank, 0.0), axis=-1, keepdims=True)
        info = jnp.zeros(sel.shape, F32)
        for idx, val in ((INFO_E1, i1.astype(F32)), (INFO_E2, i2.astype(F32)), (INFO_R1, r1), (INFO_R2, r2),
                         (INFO_W1, w1), (INFO_W2, w2)):
            info = jnp.where(lane == idx, val, info)
        info_ref[rows, :] = info
        base = base + incl[rb - 1:, :]
    carry_ref[...] = base
    cnt_ref[...] = base


def moe_route(x, gpre, router, tm):
    M, K = x.shape
    rb = ROUTE_ROWS
    tril = (jnp.arange(rb)[None, :] <= jnp.arange(rb)[:, None]).astype(BF16)
    return pl.pallas_call(
        _route_kernel,
        grid=(M // tm,),
        in_specs=[pl.BlockSpec((tm, K), lambda i: (i, 0)),
                  pl.BlockSpec((1, K), lambda i: (0, 0)),
                  pl.BlockSpec((K, LANES), lambda i: (0, 0)),
                  pl.BlockSpec((rb, rb), lambda i: (0, 0))],
        out_specs=[pl.BlockSpec((tm, LANES), lambda i: (i, 0)),
                   pl.BlockSpec((1, LANES), lambda i: (0, 0))],
        out_shape=[jax.ShapeDtypeStruct((M, LANES), F32), jax.ShapeDtypeStruct((1, LANES), F32)],
        scratch_shapes=[pltpu.VMEM((1, LANES), F32)],
        compiler_params=_cparams(("arbitrary",)),
        name="moe_route",
    )(x, gpre.reshape(1, K), router, tril)


def _dispatch_kernel(dest_ref, x_ref, zeros_ref, xs_ref, sem):
    del zeros_ref
    tm = x_ref.shape[0]
    base = pl.program_id(0) * tm

    def issue(r, carry):
        for k in range(2):
            d = dest_ref[(base + r) * 2 + k]
            pltpu.make_async_copy(x_ref.at[pl.ds(r, 1), :], xs_ref.at[pl.ds(d, 1), :], sem).start()
        return carry

    lax.fori_loop(0, tm, issue, 0, unroll=DMA_UNROLL)
    for k in range(2):
        pltpu.make_async_copy(x_ref, xs_ref.at[pl.ds(0, tm), :], sem).wait()


def moe_dispatch(x, dest, n_slots, tm):
    M, K = x.shape
    return pl.pallas_call(
        _dispatch_kernel,
        grid_spec=pltpu.PrefetchScalarGridSpec(
            num_scalar_prefetch=1,
            grid=(M // tm,),
            in_specs=[pl.BlockSpec((tm, K), lambda i, dest: (i, 0)),
                      pl.BlockSpec(memory_space=pl.ANY)],
            out_specs=pl.BlockSpec(memory_space=pl.ANY),
            scratch_shapes=[pltpu.SemaphoreType.DMA(())]),
        out_shape=jax.ShapeDtypeStruct((n_slots, K), F32),
        input_output_aliases={2: 0},
        compiler_params=_cparams(("arbitrary",)),
        name="moe_dispatch",
    )(dest, x, jnp.zeros((n_slots, K), F32))


def _experts_kernel(te_ref, tv_ref, xs_ref, gpre_ref, wg_ref, wu_ref, wd_ref, o_ref, h_ref, acc_ref):
    del te_ref
    i = pl.program_id(0)
    f = pl.program_id(1)
    last = pl.num_programs(1) - 1

    @pl.when(tv_ref[i] == 1)
    def _():
        @pl.when(f == 0)
        def _():
            h_ref[...] = _rms(xs_ref[...], gpre_ref[...]).astype(BF16)
            acc_ref[...] = jnp.zeros_like(acc_ref)

        hb = h_ref[...]
        act = jax.nn.silu(_dot(hb, wg_ref[0].astype(BF16))) * _dot(hb, wu_ref[0].astype(BF16))
        acc_ref[...] += _dot(act.astype(BF16), wd_ref[0].astype(BF16))

        @pl.when(f == last)
        def _():
            o_ref[...] = acc_ref[...]

    @pl.when((tv_ref[i] == 0) & (f == last))
    def _():
        o_ref[...] = jnp.zeros_like(o_ref)


def moe_experts(xs, gpre, tile_expert, tile_valid, wg, wu, wd, tg, tf):
    S, K = xs.shape
    F = wg.shape[2]
    nf = F // tf
    fidx = lambda i, f, te, tv: jnp.where(tv[i] == 1, f, nf - 1)
    return pl.pallas_call(
        _experts_kernel,
        grid_spec=pltpu.PrefetchScalarGridSpec(
            num_scalar_prefetch=2,
            grid=(S // tg, nf),
            in_specs=[pl.BlockSpec((tg, K), lambda i, f, te, tv: (i, 0)),
                      pl.BlockSpec((1, K), lambda i, f, te, tv: (0, 0)),
                      pl.BlockSpec((1, K, tf), lambda i, f, te, tv: (te[i], 0, fidx(i, f, te, tv))),
                      pl.BlockSpec((1, K, tf), lambda i, f, te, tv: (te[i], 0, fidx(i, f, te, tv))),
                      pl.BlockSpec((1, tf, K), lambda i, f, te, tv: (te[i], fidx(i, f, te, tv), 0))],
            out_specs=pl.BlockSpec((tg, K), lambda i, f, te, tv: (i, 0)),
            scratch_shapes=[pltpu.VMEM((tg, K), BF16), pltpu.VMEM((tg, K), F32)]),
        out_shape=jax.ShapeDtypeStruct((S, K), F32),
        compiler_params=_cparams(("parallel", "arbitrary")),
        name="moe_experts",
    )(tile_expert, tile_valid, xs, gpre.reshape(1, K), wg, wu, wd)


def _combine_kernel(dest_ref, x_ref, info_ref, gpost_ref, ys_ref, o_ref, buf_ref, sem):
    tm = x_ref.shape[0]
    i = pl.program_id(0)

    def gather(tile, slot):
        def issue(r, carry):
            for k in range(2):
                d = dest_ref[(tile * tm + r) * 2 + k]
                pltpu.make_async_copy(ys_ref.at[pl.ds(d, 1), :], buf_ref.at[slot, k, pl.ds(r, 1), :],
                                      sem.at[slot]).start()
            return carry

        lax.fori_loop(0, tm, issue, 0, unroll=DMA_UNROLL)

    @pl.when(i == 0)
    def _():
        gather(0, 0)

    @pl.when(i + 1 < pl.num_programs(0))
    def _():
        gather(i + 1, (i + 1) % 2)

    slot = i % 2
    for k in range(2):
        pltpu.make_async_copy(ys_ref.at[pl.ds(0, tm), :], buf_ref.at[slot, k], sem.at[slot]).wait()
    info = info_ref[...]
    y = info[:, INFO_W1:INFO_W1 + 1] * buf_ref[slot, 0] + info[:, INFO_W2:INFO_W2 + 1] * buf_ref[slot, 1]
    o_ref[...] = x_ref[...] + _rms(y, gpost_ref[...])


def moe_combine(x, info, gpost, ys, dest, tm):
    M, K = x.shape
    return pl.pallas_call(
        _combine_kernel,
        grid_spec=pltpu.PrefetchScalarGridSpec(
            num_scalar_prefetch=1,
            grid=(M // tm,),
            in_specs=[pl.BlockSpec((tm, K), lambda i, dest: (i, 0)),
                      pl.BlockSpec((tm, LANES), lambda i, dest: (i, 0)),
                      pl.BlockSpec((1, K), lambda i, dest: (0, 0)),
                      pl.BlockSpec(memory_space=pl.ANY)],
            out_specs=pl.BlockSpec((tm, K), lambda i, dest: (i, 0)),
            scratch_shapes=[pltpu.VMEM((2, 2, tm, K), F32), pltpu.SemaphoreType.DMA((2,))]),
        out_shape=jax.ShapeDtypeStruct((M, K), F32),
        compiler_params=_cparams(("arbitrary",)),
        name="moe_combine",
    )(dest, x, info, gpost.reshape(1, K), ys)


def moe_routed(x, gpre, gpost, router, wg, wu, wd):
    M, K = x.shape
    tg = TG_MOE
    n_slots = 2 * M + N_EXPERTS * tg
    info, cnt = moe_route(x, gpre, router, TM_ROUTE)
    cnt = cnt[0, :N_EXPERTS].astype(jnp.int32)
    padded = (cnt + tg - 1) // tg * tg
    ends = jnp.cumsum(padded)
    offs = ends - padded
    ids = info[:, INFO_E1:INFO_E2 + 1].astype(jnp.int32)
    ranks = info[:, INFO_R1:INFO_R2 + 1].astype(jnp.int32)
    dest = (offs[ids] + ranks).reshape(2 * M)
    tile_start = jnp.arange(n_slots // tg, dtype=jnp.int32) * tg
    tile_valid = (tile_start < ends[-1]).astype(jnp.int32)
    tile_expert = jnp.minimum(jnp.sum((tile_start[:, None] >= ends[None, :]).astype(jnp.int32), axis=1),
                              N_EXPERTS - 1)
    tile_expert = jnp.where(tile_valid == 1, tile_expert, tile_expert[jnp.maximum(ends[-1] // tg - 1, 0)])
    xs = moe_dispatch(x, dest, n_slots, TM_DISPATCH)
    ys = moe_experts(xs, gpre, tile_expert, tile_valid, wg, wu, wd, tg, TF_ROUTED)
    return moe_combine(x, info, gpost, ys, dest, TM_COMBINE)


def _xattn_kernel(x_ref, k_ref, v_ref, wq_ref, wo_ref, gpre_ref, gpost_ref, o_ref):
    x = x_ref[0]
    h = _rms(x, gpre_ref[...]).astype(BF16)
    q = _dot(h, wq_ref[...])
    k = k_ref[0, 0].astype(BF16)
    v = v_ref[0, 0].astype(BF16)
    heads = [slice(hd * X_HEAD_DIM, (hd + 1) * X_HEAD_DIM) for hd in range(X_HEADS)]
    ss = [_dot_nt(q[:, sl].astype(BF16), k[:, sl]) * (X_HEAD_DIM ** -0.5) for sl in heads]
    es = [jnp.exp(s - jnp.max(s, axis=-1, keepdims=True)) for s in ss]
    ps = [(e / jnp.sum(e, axis=-1, keepdims=True)).astype(BF16) for e in es]
    o = jnp.concatenate([_dot(p, v[:, sl]) for p, sl in zip(ps, heads)], axis=-1).astype(BF16)
    o_ref[0] = x + _rms(_dot(o, wo_ref[...]), gpost_ref[...])


def xattn_prompt(x, k, v, l, wq, wo, gpre, gpost, tm):
    B, T, K = x.shape
    return pl.pallas_call(
        _xattn_kernel,
        grid=(B, T // tm),
        in_specs=[pl.BlockSpec((1, tm, K), lambda b, t: (b, t, 0)),
                  pl.BlockSpec((1, 1, MEM_LEN, K), lambda b, t: (l, b, 0, 0)),
                  pl.BlockSpec((1, 1, MEM_LEN, K), lambda b, t: (l, b, 0, 0)),
                  pl.BlockSpec((K, K), lambda b, t: (0, 0)),
                  pl.BlockSpec((K, K), lambda b, t: (0, 0)),
                  pl.BlockSpec((1, K), lambda b, t: (0, 0)),
                  pl.BlockSpec((1, K), lambda b, t: (0, 0))],
        out_specs=pl.BlockSpec((1, tm, K), lambda b, t: (b, t, 0)),
        out_shape=jax.ShapeDtypeStruct((B, T, K), F32),
        compiler_params=_cparams(("parallel", "parallel")),
        name="xattn_prompt",
    )(x, k, v, wq, wo, gpre.reshape(1, K), gpost.reshape(1, K))


def _xattn_sample_kernel(q_ref, k_ref, v_ref, o_ref):
    mc = XATTN_MEM_CHUNK
    chunks = [slice(c * mc, (c + 1) * mc) for c in range(MEM_LEN // mc)]
    for j in range(q_ref.shape[0]):
        q = q_ref[j]
        s = jnp.concatenate([jnp.sum(k_ref[0, j, rows] * q[None], axis=-1, keepdims=True) for rows in chunks],
                            axis=0) * (X_HEAD_DIM ** -0.5)
        e = jnp.exp(s - jnp.max(s, axis=0, keepdims=True))
        p = e / jnp.sum(e, axis=0, keepdims=True)
        o = None
        for rows in chunks:
            part = jnp.sum(p[rows] * v_ref[0, j, rows], axis=0)
            o = part if o is None else o + part
        o_ref[j] = o


def xattn_sample_core(q, k, v, l, nb):
    B = q.shape[0]
    kv_spec = pl.BlockSpec((1, nb, MEM_LEN, X_HEADS, X_HEAD_DIM), lambda b: (l, b, 0, 0, 0))
    return pl.pallas_call(
        _xattn_sample_kernel,
        grid=(B // nb,),
        in_specs=[pl.BlockSpec((nb, X_HEADS, X_HEAD_DIM), lambda b: (b, 0, 0)), kv_spec, kv_spec],
        out_specs=pl.BlockSpec((nb, X_HEADS, X_HEAD_DIM), lambda b: (b, 0, 0)),
        out_shape=jax.ShapeDtypeStruct((B, X_HEADS, X_HEAD_DIM), F32),
        compiler_params=_cparams(("parallel",)),
        name="xattn_sample",
    )(q, k, v)


def _ev_proj_kernel(x_ref, g_ref, w_ref, lng_ref, lnb_ref, cw_ref, cb_ref,
                    gu_ref, vln_ref, sz_ref, xa_ref, dt_ref, tail_ref, xp_ref):
    t = pl.program_id(1)
    tm = x_ref.shape[1]

    @pl.when(t == 0)
    def _():
        xp_ref[0:SUBLANES, :] = jnp.zeros((SUBLANES, B_CONV_DIM), F32)

    h = _rms(x_ref[0], g_ref[...]).astype(BF16)
    u = _dot(h, w_ref[:, 0:D])
    v = _dot(h, w_ref[:, D:2 * D])
    gu_ref[0] = jax.nn.gelu(u)
    z = _dot(h, w_ref[:, 2 * D:3 * D])
    vln_ref[0] = _layernorm(jax.nn.gelu(v), lng_ref[...], lnb_ref[...]).astype(BF16)
    x = _dot(h, w_ref[:, 3 * D:3 * D + B_CONV_DIM])
    sz_ref[0] = jax.nn.silu(z)
    dt_ref[0] = _dot(h, w_ref[:, DT_COL_BLOCK * LANES:(DT_COL_BLOCK + 1) * LANES])
    xp_ref[SUBLANES:SUBLANES + tm, :] = x
    conv = cb_ref[...] + cw_ref[B_CONV - 1:B_CONV, :] * x
    for k in range(B_CONV - 1):
        conv = conv + cw_ref[k:k + 1, :] * xp_ref[pl.ds(SUBLANES - (B_CONV - 1) + k, tm), :]
    xp_ref[0:SUBLANES, :] = x[tm - SUBLANES:tm, :]
    tail_ref[0] = x[tm - SUBLANES:tm, :]
    xa_ref[0] = jax.nn.silu(conv)


def ev_proj(x, g, w, P, tm):
    B, T, K = x.shape
    row = lambda a: a.reshape(1, -1)
    full = lambda shape: pl.BlockSpec(shape, lambda b, t: (0,) * len(shape))
    tile = lambda n: pl.BlockSpec((1, tm, n), lambda b, t: (b, t, 0))
    return pl.pallas_call(
        _ev_proj_kernel,
        grid=(B, T // tm),
        in_specs=[tile(K), full((1, K)), full((K, IN0_PAD)), full((1, D)), full((1, D)),
                  full((B_CONV, B_CONV_DIM)), full((1, B_CONV_DIM))],
        out_specs=[tile(D), tile(D), tile(D), tile(B_CONV_DIM), tile(LANES),
                   pl.BlockSpec((1, SUBLANES, B_CONV_DIM), lambda b, t: (b, 0, 0))],
        out_shape=[jax.ShapeDtypeStruct((B, T, D), F32), jax.ShapeDtypeStruct((B, T, D), BF16),
                   jax.ShapeDtypeStruct((B, T, D), F32), jax.ShapeDtypeStruct((B, T, B_CONV_DIM), F32),
                   jax.ShapeDtypeStruct((B, T, LANES), F32), jax.ShapeDtypeStruct((B, SUBLANES, B_CONV_DIM), F32)],
        scratch_shapes=[pltpu.VMEM((SUBLANES + tm, B_CONV_DIM), F32)],
        compiler_params=_cparams(("parallel", "arbitrary")),
        name="ev_proj",
    )(x, row(g), w, row(P["a_ln_g"][0]), row(P["a_ln_b"][0]), P["b_conv_w"][0], row(P["b_conv_b"][0]))


def _mixer_even_kernel(gu_ref, vln_ref, sz_ref, xa_ref, dt_ref, ws_ref, bst_ref,
                       dtb_ref, anar_ref, dexp_ref, bnorm_ref, tril_ref, expand_ref,
                       yab_ref, ssm_ref, st_ref):
    c = pl.program_id(1)
    L = A_CHUNK

    @pl.when(c == 0)
    def _():
        st_ref[...] = jnp.zeros_like(st_ref)

    row = lax.broadcasted_iota(jnp.int32, (L, L), 0)
    col = lax.broadcasted_iota(jnp.int32, (L, L), 1)
    causal = col <= row

    tril = tril_ref[...]
    lane = lax.broadcasted_iota(jnp.int32, (L, LANES), 1)

    def chunk(rows):
        gu = gu_ref[0, rows, :]
        vb = vln_ref[0, rows, :]
        a_heads = [slice(hh * LANES, (hh + 1) * LANES) for hh in range(A_HEADS)]
        mixes = [_dot(jnp.where(causal, ws_ref[hh], 0.0).astype(BF16), vb[:, sl]) for hh, sl in enumerate(a_heads)]
        for hh, sl in enumerate(a_heads):
            yab_ref[0, rows, sl] = (gu[:, sl] * (mixes[hh] + bst_ref[:, hh:hh + 1])).astype(BF16)

        xa = xa_ref[0, rows, :]
        xs = xa[:, :D]
        bm = xa[:, D:D + B_GROUPS * B_STATE]
        cm = xa[:, D + B_GROUPS * B_STATE:]

        dtf = _softplus(dt_ref[0, rows, :] + dtb_ref[...])
        cs_n = _sel_left(tril, dtf * anar_ref[...], _split3)
        dt_x = _sel_right(dtf, expand_ref[...], _split2)
        cs_x = _sel_right(cs_n, expand_ref[...], _split3)
        ecs_x = jnp.exp(cs_x)
        last_x = cs_x[L - 1:L, :]
        xdt = xs * dt_x
        xdt_b = xdt.astype(BF16)
        xdec_b = (xdt * jnp.exp(last_x - cs_x)).astype(BF16)
        groups = range(B_GROUPS)
        gsl = [slice(g * B_GROUP_W, (g + 1) * B_GROUP_W) for g in groups]
        bgs = [bm[:, g * B_STATE:(g + 1) * B_STATE] for g in groups]
        cgs = [cm[:, g * B_STATE:(g + 1) * B_STATE].astype(BF16) for g in groups]
        gmats = [_dot_nt(cgs[g], bgs[g].astype(BF16)) for g in groups]
        sts = [st_ref[g] for g in groups]
        y_offs = [ecs_x[:, gsl[g]] * _dot(cgs[g], sts[g].astype(BF16)) for g in groups]
        for g in groups:
            st_ref[g] = sts[g] * jnp.exp(last_x[:, gsl[g]]) + _dot(bgs[g].T.astype(BF16), xdec_b[:, gsl[g]])
        pairs = [(g, pair) for g in groups for pair in range(B_GROUP_W // LANES)]
        bases = [g * B_GROUP_W + pair * LANES for g, pair in pairs]
        cs_ts = [cs_x[:, base:base + LANES].T for base in bases]
        lhs = []
        for (g, _), base, cs_t in zip(pairs, bases, cs_ts):
            ms = []
            for half in range(2):
                ch = half * B_HEAD_DIM
                diff = cs_x[:, base + ch:base + ch + 1] - cs_t[ch:ch + 1, :]
                ms.append((gmats[g] * jnp.exp(jnp.where(causal, diff, -jnp.inf))).astype(BF16))
            lhs.append(jnp.concatenate(ms, axis=1))
        y_parts = []
        for (g, pair), base, m2 in zip(pairs, bases, lhs):
            xp2 = xdt_b[:, base:base + LANES]
            rhs = jnp.concatenate([jnp.where(lane < B_HEAD_DIM, xp2, jnp.zeros_like(xp2)),
                                   jnp.where(lane >= B_HEAD_DIM, xp2, jnp.zeros_like(xp2))], axis=0)
            y_parts.append(_dot(m2, rhs) + y_offs[g][:, pair * LANES:(pair + 1) * LANES])
        y = jnp.concatenate(y_parts, axis=-1) + dexp_ref[...] * xs
        yb = _group_rms(y * sz_ref[0, rows, :], bnorm_ref[...], B_GROUP_W)
        yab_ref[0, rows, D:] = yb.astype(BF16)

    for sub in range(gu_ref.shape[1] // L):
        chunk(slice(sub * L, (sub + 1) * L))

    @pl.when(c == pl.num_programs(1) - 1)
    def _():
        for g in range(B_GROUPS):
            ssm_ref[0, g] = st_ref[g].T


def _even_consts(P):
    head_of_ch = jnp.arange(D) // B_HEAD_DIM
    expand = (jnp.arange(LANES)[:, None] == head_of_ch[None, :]).astype(BF16)
    tril = (jnp.arange(A_CHUNK)[None, :] <= jnp.arange(A_CHUNK)[:, None]).astype(BF16)
    aexp = jnp.repeat(-jnp.exp(P["b_a_log"][0].astype(F32)), B_HEAD_DIM).reshape(1, D)
    dexp = jnp.repeat(P["b_d"][0].astype(F32), B_HEAD_DIM).reshape(1, D)
    dtb = jnp.pad(P["b_dt_bias"][0].astype(F32), (0, LANES - B_HEADS)).reshape(1, LANES)
    return expand, tril, aexp, dexp, dtb


def mixer_even_prompt(gu, vln, sz, xa, dt, P):
    B, T, _ = gu.shape
    L = A_CHUNK
    expand, tril, _, dexp, dtb = _even_consts(P)
    anar = jnp.pad(-jnp.exp(P["b_a_log"][0].astype(F32)), (0, LANES - B_HEADS)).reshape(1, LANES)
    row = lambda a: a.reshape(1, -1)
    full = lambda shape: pl.BlockSpec(shape, lambda b, c: (0,) * len(shape))
    R = A_ROWS
    chunk = lambda n: pl.BlockSpec((1, R, n), lambda b, c: (b, c, 0))
    return pl.pallas_call(
        _mixer_even_kernel,
        grid=(B, T // R),
        in_specs=[chunk(D), chunk(D), chunk(D), chunk(B_CONV_DIM), chunk(LANES),
                  full((A_HEADS, L, L)), full((L, A_HEADS)), full((1, LANES)), full((1, LANES)), full((1, D)),
                  full((1, D)), full((L, L)), full((LANES, D))],
        out_specs=[pl.BlockSpec((1, R, 2 * D), lambda b, c: (b, c, 0)),
                   pl.BlockSpec((1, B_GROUPS, B_GROUP_W, B_STATE), lambda b, c: (b, 0, 0, 0))],
        out_shape=[jax.ShapeDtypeStruct((B, T, 2 * D), BF16),
                   jax.ShapeDtypeStruct((B, B_GROUPS, B_GROUP_W, B_STATE), F32)],
        scratch_shapes=[pltpu.VMEM((B_GROUPS, B_STATE, B_GROUP_W), F32)],
        compiler_params=_cparams(("parallel", "arbitrary")),
        name="mixer_even_prompt",
    )(gu, vln, sz, xa, dt, P["a_ws"][0], P["a_bs"][0].T, dtb, anar, dexp, row(P["b_norm"][0]), tril, expand)


def _mixer_even_step_kernel(proj_ref, conv_ref, ssm_ref, ws0_ref, bs0_ref, lng_ref, lnb_ref, cw_ref, cb_ref,
                            dtb_ref, aexp_ref, dexp_ref, bnorm_ref, expand_ref,
                            yab_ref, av_ref, convo_ref, ssmo_ref):
    nb = proj_ref.shape[0]
    u = proj_ref[:, 0:D]
    v = proj_ref[:, D:2 * D]
    z = proj_ref[:, 2 * D:3 * D]
    x = proj_ref[:, 3 * D:3 * D + B_CONV_DIM]
    dt = proj_ref[:, DT_COL_BLOCK * LANES:(DT_COL_BLOCK + 1) * LANES]

    vln = _layernorm(jax.nn.gelu(v), lng_ref[...], lnb_ref[...])
    av_ref[...] = vln
    yab_ref[:, 0:D] = jax.nn.gelu(u) * (ws0_ref[...] * vln + bs0_ref[...])

    conv = cb_ref[...] + cw_ref[B_CONV - 1:B_CONV, :] * x
    for k in range(B_CONV - 1):
        conv = conv + cw_ref[k:k + 1, :] * conv_ref[k]
        if k > 0:
            convo_ref[k - 1] = conv_ref[k]
    convo_ref[B_CONV - 2] = x
    xa = jax.nn.silu(conv)
    xs = xa[:, :D]
    bm = xa[:, D:D + B_GROUPS * B_STATE]
    cm = xa[:, D + B_GROUPS * B_STATE:]
    dtf = _softplus(dt + dtb_ref[...])
    dt_x = _sel_right(dtf, expand_ref[...], _split3)
    dec_x = jnp.exp(dt_x * aexp_ref[...])
    xdt = xs * dt_x
    y_rows = []
    for g in range(B_GROUPS):
        gs = slice(g * B_GROUP_W, (g + 1) * B_GROUP_W)
        dec_t = _rows_to_cols(dec_x[:, gs])
        xdt_t = _rows_to_cols(xdt[:, gs])
        s_news = [ssm_ref[j, g] * dec_t[:, j:j + 1] + xdt_t[:, j:j + 1] * bm[j:j + 1, g * B_STATE:(g + 1) * B_STATE]
                  for j in range(nb)]
        for j in range(nb):
            ssmo_ref[j, g] = s_news[j]
        cjs = [jnp.broadcast_to(cm[j:j + 1, g * B_STATE:(g + 1) * B_STATE], (SUBLANES, B_STATE)) for j in range(nb)]
        y_rows.append(jnp.concatenate([_dot_nt_f32(cjs[j], s_news[j])[0:1, :] for j in range(nb)], axis=0))
    y = jnp.concatenate(y_rows, axis=-1) + dexp_ref[...] * xs
    yb = _group_rms(y * jax.nn.silu(z), bnorm_ref[...], B_GROUP_W)
    yab_ref[:, D:] = yb


def mixer_even_step(proj, conv0, ssm0, P, nb):
    B = proj.shape[0]
    expand, _, aexp, dexp, dtb = _even_consts(P)
    row = lambda a: a.reshape(1, -1)
    rep = lambda a: jnp.repeat(a.astype(F32), LANES).reshape(1, D)
    full = lambda shape: pl.BlockSpec(shape, lambda i: (0,) * len(shape))
    return pl.pallas_call(
        _mixer_even_step_kernel,
        grid=(B // nb,),
        in_specs=[pl.BlockSpec((nb, proj.shape[1]), lambda i: (i, 0)),
                  pl.BlockSpec((B_CONV - 1, nb, B_CONV_DIM), lambda i: (0, i, 0)),
                  pl.BlockSpec((nb, B_GROUPS, B_GROUP_W, B_STATE), lambda i: (i, 0, 0, 0)),
                  full((1, D)), full((1, D)), full((1, D)), full((1, D)),
                  full((B_CONV, B_CONV_DIM)), full((1, B_CONV_DIM)), full((1, LANES)), full((1, D)), full((1, D)),
                  full((1, D)), full((LANES, D))],
        out_specs=[pl.BlockSpec((nb, 2 * D), lambda i: (i, 0)),
                   pl.BlockSpec((nb, D), lambda i: (i, 0)),
                   pl.BlockSpec((B_CONV - 1, nb, B_CONV_DIM), lambda i: (0, i, 0)),
                   pl.BlockSpec((nb, B_GROUPS, B_GROUP_W, B_STATE), lambda i: (i, 0, 0, 0))],
        out_shape=[jax.ShapeDtypeStruct((B, 2 * D), F32),
                   jax.ShapeDtypeStruct((B, D), F32),
                   jax.ShapeDtypeStruct((B_CONV - 1, B, B_CONV_DIM), F32),
                   jax.ShapeDtypeStruct((B, B_GROUPS, B_GROUP_W, B_STATE), F32)],
        compiler_params=_cparams(("parallel",)),
        name="mixer_even_step",
    )(proj, conv0, ssm0, rep(P["a_ws"][0][:, 0, 0]), rep(P["a_bs"][0][:, 0]), row(P["a_ln_g"][0]), row(P["a_ln_b"][0]),
      P["b_conv_w"][0], row(P["b_conv_b"][0]), dtb, aexp, dexp, row(P["b_norm"][0]), expand)


def _gla_gates(q_raw, f_raw, lb):
    fg = lb + (1.0 - lb) * jax.nn.sigmoid(f_raw)
    return jax.nn.silu(q_raw), fg, 1.0 - fg


def _hgrn_kernel(q_ref, f_ref, i_ref, g_ref, lb_ref, cnorm_ref, tril_ref, o_ref, s_ref, st_ref):
    c = pl.program_id(1)
    L = C_CHUNK
    R = C_ROWS

    @pl.when(c == 0)
    def _():
        st_ref[...] = jnp.zeros_like(st_ref)

    row = lax.broadcasted_iota(jnp.int32, (R, R), 0)
    col = lax.broadcasted_iota(jnp.int32, (R, R), 1)
    causal = (col <= row) & (row // L == col // L)
    chunk_of_row = lax.broadcasted_iota(jnp.int32, (R, 1), 0) // L
    tril = tril_ref[...]
    heads = [slice(hh * C_KDIM, (hh + 1) * C_KDIM) for hh in range(C_HEADS)]

    def block(rows):
        q, fg, k = _gla_gates(q_ref[0, rows, :], f_ref[0, rows, :], lb_ref[...])
        v = i_ref[0, rows, :]
        bc = _sel_left(tril, jnp.log(fg), _split3)
        q_in = (q * jnp.exp(bc)).astype(BF16)
        k_in = (k * jnp.exp(-bc)).astype(BF16)
        vb = v.astype(BF16)
        btots, k_decs = [], []
        for s in range(R // L):
            btot = bc[(s + 1) * L - 1:(s + 1) * L, :]
            btots.append(btot)
            k_decs.append(jnp.where(chunk_of_row == s, k * jnp.exp(btot - bc), 0.0).astype(BF16))
        atts = [jnp.where(causal, _dot_nt(q_in[:, sl], k_in[:, sl]), 0.0).astype(BF16) for sl in heads]
        v_ts = [v[:, sl].T.astype(BF16) for sl in heads]
        sts = [st_ref[hh] for hh in range(C_HEADS)]
        inters = [[] for _ in heads]
        for s in range(R // L):
            for hh, sl in enumerate(heads):
                inters[hh].append(_dot_nt(q_in[s * L:(s + 1) * L, sl], sts[hh].astype(BF16)))
            sts = [sts[hh] * jnp.exp(btots[s][:, sl]) + _dot(v_ts[hh], k_decs[s][:, sl])
                   for hh, sl in enumerate(heads)]
        for hh in range(C_HEADS):
            st_ref[hh] = sts[hh]
        outs = [_dot(atts[hh], vb[:, sl]) + jnp.concatenate(inters[hh], axis=0) for hh, sl in enumerate(heads)]
        o = _group_rms(jnp.concatenate(outs, axis=-1), cnorm_ref[...], C_KDIM)
        o_ref[0, rows, :] = (o * jax.nn.silu(g_ref[0, rows, :])).astype(BF16)

    for blk in range(q_ref.shape[1] // R):
        block(slice(blk * R, (blk + 1) * R))

    @pl.when(c == pl.num_programs(1) - 1)
    def _():
        for hh in range(C_HEADS):
            s_ref[0, hh] = st_ref[hh].T


def hgrn_prompt(proj, lb, cnorm):
    B, T, _ = proj.shape
    L = C_ROWS
    r = jnp.arange(L)
    tril = ((r[None, :] <= r[:, None]) & (r[None, :] // C_CHUNK == r[:, None] // C_CHUNK)).astype(BF16)
    full = lambda shape: pl.BlockSpec(shape, lambda b, c: (0,) * len(shape))
    S = C_STEP_ROWS
    return pl.pallas_call(
        _hgrn_kernel,
        grid=(B, T // S),
        in_specs=[pl.BlockSpec((1, S, D), lambda b, c: (b, c, 0)),
                  pl.BlockSpec((1, S, D), lambda b, c: (b, c, 1)),
                  pl.BlockSpec((1, S, D), lambda b, c: (b, c, 2)),
                  pl.BlockSpec((1, S, D), lambda b, c: (b, c, 3)),
                  full((1, D)), full((1, D)), full((L, L))],
        out_specs=[pl.BlockSpec((1, S, D), lambda b, c: (b, c, 0)),
                   pl.BlockSpec((1, C_HEADS, C_KDIM, C_KDIM), lambda b, c: (b, 0, 0, 0))],
        out_shape=[jax.ShapeDtypeStruct((B, T, D), BF16),
                   jax.ShapeDtypeStruct((B, C_HEADS, C_KDIM, C_KDIM), F32)],
        scratch_shapes=[pltpu.VMEM((C_HEADS, C_KDIM, C_KDIM), F32)],
        compiler_params=_cparams(("parallel", "arbitrary")),
        name="hgrn_prompt",
    )(proj, proj, proj, proj, lb.reshape(1, D), cnorm.reshape(1, D), tril)


def _hgrn_step_kernel(proj_ref, s_ref, lb_ref, cnorm_ref, o_ref, so_ref):
    nb = proj_ref.shape[0]
    q, fg, k = _gla_gates(proj_ref[:, 0:D], proj_ref[:, D:2 * D], lb_ref[...])
    v = proj_ref[:, 2 * D:3 * D]
    g = proj_ref[:, 3 * D:4 * D]
    heads = [slice(hh * C_KDIM, (hh + 1) * C_KDIM) for hh in range(C_HEADS)]
    fg_ts = [_rows_to_cols(fg[:, sl]) for sl in heads]
    k_ts = [_rows_to_cols(k[:, sl]) for sl in heads]
    outs = []
    for hh, sl in enumerate(heads):
        s_news = [s_ref[j, hh] * fg_ts[hh][:, j:j + 1] + k_ts[hh][:, j:j + 1] * v[j:j + 1, sl] for j in range(nb)]
        for j in range(nb):
            so_ref[j, hh] = s_news[j]
        qjs = [jnp.broadcast_to(q[j:j + 1, sl], (SUBLANES, C_KDIM)) for j in range(nb)]
        outs.append(jnp.concatenate([_dot_f32(qjs[j], s_news[j])[0:1, :] for j in range(nb)], axis=0))
    o = _group_rms(jnp.concatenate(outs, axis=-1), cnorm_ref[...], C_KDIM)
    o_ref[...] = o * jax.nn.silu(g)


def hgrn_step(proj, s0, lb, cnorm, nb):
    B = proj.shape[0]
    full = lambda shape: pl.BlockSpec(shape, lambda i: (0,) * len(shape))
    return pl.pallas_call(
        _hgrn_step_kernel,
        grid=(B // nb,),
        in_specs=[pl.BlockSpec((nb, 4 * D), lambda i: (i, 0)),
                  pl.BlockSpec((nb, C_HEADS, C_KDIM, C_KDIM), lambda i: (i, 0, 0, 0)),
                  full((1, D)), full((1, D))],
        out_specs=[pl.BlockSpec((nb, D), lambda i: (i, 0)),
                   pl.BlockSpec((nb, C_HEADS, C_KDIM, C_KDIM), lambda i: (i, 0, 0, 0))],
        out_shape=[jax.ShapeDtypeStruct((B, D), F32),
                   jax.ShapeDtypeStruct((B, C_HEADS, C_KDIM, C_KDIM), F32)],
        compiler_params=_cparams(("parallel",)),
        name="hgrn_step",
    )(proj, s0, lb.reshape(1, D), cnorm.reshape(1, D))


TM_PROJ = 512
TM_EV_PROJ = 256
TM_OUT = 1024
TM_ATTN = 1024
TM_FFN = 1024
TF_DENSE = 256
TF_MOE = 896
TF_ROUTED = 512
TG_MOE = 1024
TM_ROUTE = 1024
ROUTE_ROWS = 256
TM_DISPATCH = 2048
TM_COMBINE = 1024
DMA_UNROLL = 8
ROUTED_MIN_TOKENS = 8 * TG_MOE
STEP_NB = 8
XATTN_STEP_NB = 8
XATTN_MEM_CHUNK = 64
TN_STEP = 512
TK_STEP = 512
IN0_STEP_PAD = 5120


def _prep_weights(P):
    W = {}
    W["ev_w_in"] = jnp.pad(P["ev_w_in"][0], ((0, 0), (0, IN0_PAD - IN0))).astype(BF16)
    W["ev_w_out"] = P["ev_w_out"][0].astype(BF16)
    W["od_w_in"] = P["od_w_in"][0].astype(BF16)
    W["od_w_out"] = P["od_w_out"][0].astype(BF16)
    W["xa_wq"] = P["xa_wq"].astype(BF16)
    W["xa_wo"] = P["xa_wo"].astype(BF16)
    W["xa_wk"] = P["xa_wk"].astype(BF16)
    W["xa_wv"] = P["xa_wv"].astype(BF16)
    W["ffn"] = tuple(P[n].astype(BF16) for n in ("ffn_w_gate", "ffn_w_up", "ffn_w_down"))
    W["moe"] = tuple(P[n][0] for n in ("moe_w_gate", "moe_w_up", "moe_w_down"))
    W["router"] = jnp.pad(P["moe_router"][0].astype(F32), ((0, 0), (0, LANES - N_EXPERTS)))
    lbp = jax.nn.softmax(P["hgrn_lb_logits"].astype(F32), axis=0)
    W["lower_bounds"] = jnp.cumsum(lbp, axis=0) - lbp[0]
    return W


def _channel_mix(x2, l, P, W, tm):
    if l == 0:
        wg, wu, wd = W["ffn"]
        return ffn(x2, P["norm_ffn_pre"][l], P["norm_ffn_post"][l], W["router"], wg, wu, wd, tm, TF_DENSE, False)
    wg, wu, wd = W["moe"]
    if x2.shape[0] >= ROUTED_MIN_TOKENS:
        return moe_routed(x2, P["norm_ffn_pre"][l], P["norm_ffn_post"][l], W["router"], wg, wu, wd)
    return ffn(x2, P["norm_ffn_pre"][l], P["norm_ffn_post"][l], W["router"], wg, wu, wd, tm, TF_MOE, True)


def _trunk_prompt(x, mem_k, mem_v, P, W):
    B, T, _ = x.shape
    M = B * T
    x2 = x.reshape(M, D)
    gu, vln, sz, xa, dt, tail = ev_proj(x, P["norm_mix_pre"][0], W["ev_w_in"], P, TM_EV_PROJ)
    yab, ssm = mixer_even_prompt(gu, vln, sz, xa, dt, P)
    conv = tail[:, SUBLANES - (B_CONV - 1):, :]
    x2 = matmul_norm_res(yab.reshape(M, 2 * D), W["ev_w_out"], P["norm_mix_post"][0], x2, TM_OUT)
    x2 = xattn_prompt(x2.reshape(B, T, D), mem_k, mem_v, 0, W["xa_wq"][0], W["xa_wo"][0],
                      P["norm_x_pre"][0], P["norm_x_post"][0], TM_ATTN).reshape(M, D)
    x2 = _channel_mix(x2, 0, P, W, TM_FFN)
    proj = norm_matmul(x2, P["norm_mix_pre"][1], W["od_w_in"], TM_PROJ).reshape(B, T, 4 * D)
    o, hgrn = hgrn_prompt(proj, W["lower_bounds"][1], P["c_norm"][0])
    x2 = matmul_norm_res(o.reshape(M, D), W["od_w_out"], P["norm_mix_post"][1], x2, TM_OUT)
    x2 = xattn_prompt(x2.reshape(B, T, D), mem_k, mem_v, 1, W["xa_wq"][1], W["xa_wo"][1],
                      P["norm_x_pre"][1], P["norm_x_post"][1], TM_ATTN).reshape(M, D)
    x2 = _channel_mix(x2, 1, P, W, TM_FFN)
    return x2.reshape(B, T, D), conv, ssm, hgrn


def _xattn_step(x2, l, mem_k, mem_v, P):
    B = x2.shape[0]
    q = norm_matmul(x2, P["norm_x_pre"][l], P["xa_wq"][l], B, TN_STEP)
    o = xattn_sample_core(q.reshape(B, X_HEADS, X_HEAD_DIM), mem_k, mem_v, l, XATTN_STEP_NB).reshape(B, D)
    return matmul_norm_res(o, P["xa_wo"][l], P["norm_x_post"][l], x2, B)


def _trunk_step(x, mem_k, mem_v, conv0, ssm0, hgrn0, P, W):
    B = x.shape[0]
    x2 = x.reshape(B, D)
    w_in = jnp.pad(P["ev_w_in"][0], ((0, 0), (0, IN0_STEP_PAD - IN0)))
    proj = norm_matmul(x2, P["norm_mix_pre"][0], w_in, B, TN_STEP)
    yab, av, conv, ssm = mixer_even_step(proj, jnp.swapaxes(conv0, 0, 1), ssm0, P, STEP_NB)
    x2 = matmul_norm_res(yab, P["ev_w_out"][0], P["norm_mix_post"][0], x2, B, TK_STEP)
    x2 = _xattn_step(x2, 0, mem_k, mem_v, P)
    x2 = ffn(x2, P["norm_ffn_pre"][0], P["norm_ffn_post"][0], W["router"],
             P["ffn_w_gate"], P["ffn_w_up"], P["ffn_w_down"], B, TF_DENSE, False, precise=True)
    proj = norm_matmul(x2, P["norm_mix_pre"][1], P["od_w_in"][0], B, TN_STEP)
    o, hgrn = hgrn_step(proj, hgrn0, W["lower_bounds"][1], P["c_norm"][0], STEP_NB)
    x2 = matmul_norm_res(o, P["od_w_out"][0], P["norm_mix_post"][1], x2, B)
    x2 = _xattn_step(x2, 1, mem_k, mem_v, P)
    x2 = _channel_mix(x2, 1, P, W, B)
    return x2.reshape(B, 1, D), jnp.swapaxes(conv, 0, 1), ssm, hgrn, av


def kernel(x_prompt, x_sample, mem_prompt, cache_mem_k, cache_mem_v, state_conv, state_ssm, state_hgrn,
           norm_mix_pre, norm_mix_post, norm_x_pre, norm_x_post, norm_ffn_pre, norm_ffn_post, norm_mem,
           xa_wq, xa_wk, xa_wv, xa_wo,
           ev_w_in, a_ws, a_bs, a_ln_g, a_ln_b, b_conv_w, b_conv_b, b_dt_bias, b_a_log, b_d, b_norm, ev_w_out,
           ffn_w_gate, ffn_w_up, ffn_w_down,
           od_w_in, hgrn_lb_logits, c_norm, od_w_out,
           moe_router, moe_w_gate, moe_w_up, moe_w_down):
    P = dict(norm_mix_pre=norm_mix_pre, norm_mix_post=norm_mix_post, norm_x_pre=norm_x_pre, norm_x_post=norm_x_post,
             norm_ffn_pre=norm_ffn_pre, norm_ffn_post=norm_ffn_post, xa_wq=xa_wq, xa_wk=xa_wk, xa_wv=xa_wv,
             xa_wo=xa_wo, ev_w_in=ev_w_in, a_ws=a_ws, a_bs=a_bs, a_ln_g=a_ln_g, a_ln_b=a_ln_b, b_conv_w=b_conv_w,
             b_conv_b=b_conv_b, b_dt_bias=b_dt_bias, b_a_log=b_a_log, b_d=b_d, b_norm=b_norm, ev_w_out=ev_w_out,
             ffn_w_gate=ffn_w_gate, ffn_w_up=ffn_w_up, ffn_w_down=ffn_w_down, od_w_in=od_w_in,
             hgrn_lb_logits=hgrn_lb_logits, c_norm=c_norm, od_w_out=od_w_out, moe_router=moe_router,
             moe_w_gate=moe_w_gate, moe_w_up=moe_w_up, moe_w_down=moe_w_down)
    W = _prep_weights(P)
    depth = norm_mem.shape[0]
    bp, T, _ = x_prompt.shape
    bs = x_sample.shape[0]

    mem_k_p, mem_v_p, mem_k_out, mem_v_out = mem_kv(mem_prompt.reshape(bp * MEM_LEN, D), norm_mem,
                                                    W["xa_wk"], W["xa_wv"], TM_PROJ)
    mem_k_p = mem_k_p.reshape(depth, bp, MEM_LEN, D)
    mem_v_p = mem_v_p.reshape(depth, bp, MEM_LEN, D)
    y_p, conv_p, ssm_p, hgrn_p = _trunk_prompt(x_prompt, mem_k_p, mem_v_p, P, W)

    y_s, conv_s, ssm_s, hgrn_s, av_s = _trunk_step(
        x_sample, cache_mem_k, cache_mem_v,
        state_conv[0], state_ssm[0].reshape(bs, B_GROUPS, B_GROUP_W, B_STATE), state_hgrn[0], P, W)

    kv_shape = (depth, bp, MEM_LEN, X_HEADS, X_HEAD_DIM)
    ssm_shape = (B_GROUPS, B_GROUP_W // B_HEAD_DIM, B_HEAD_DIM, B_STATE)
    return (y_p, y_s, mem_k_out.reshape(kv_shape), mem_v_out.reshape(kv_shape),
            conv_p[None], ssm_p.reshape((1, bp) + ssm_shape), hgrn_p[None],
            conv_s[None], ssm_s.reshape((1, bs) + ssm_shape), hgrn_s[None], av_s.reshape(1, bs, 1, D))
```

```python
import functools

import jax
import jax.numpy as jnp
from jax import lax
from jax.experimental import pallas as pl
from jax.experimental.pallas import tpu as pltpu

F32 = jnp.float32
BF16 = jnp.bfloat16
EPS = 1e-6

D = 1024
LANES = 128
SUBLANES = 8
A_HEADS = 8
A_CHUNK = 128
A_ROWS = 1024
B_HEADS = 16
B_HEAD_DIM = 64
B_GROUPS = 2
B_GROUP_W = 512
B_STATE = 128
B_CONV = 4
B_CONV_DIM = 1536
IN0 = 4624
IN0_PAD = 4736
DT_COL_BLOCK = 36
C_HEADS = 8
C_KDIM = 128
C_CHUNK = 64
C_ROWS = 128
C_STEP_ROWS = 1024
X_HEADS = 4
X_HEAD_DIM = 256
MEM_LEN = 256
N_EXPERTS = 8

VMEM_LIMIT = 56 * 1024 * 1024


def _cparams(sem):
    return pltpu.CompilerParams(dimension_semantics=sem, vmem_limit_bytes=VMEM_LIMIT)


def _dot(a, b):
    return jnp.dot(a, b, preferred_element_type=F32)


def _dot_nt(a, b):
    return lax.dot_general(a, b, (((1,), (1,)), ((), ())), preferred_element_type=F32)


def _dot_f32(a, b):
    return jnp.dot(a, b, precision=lax.Precision.HIGHEST, preferred_element_type=F32)


def _dot_nt_f32(a, b):
    return lax.dot_general(a, b, (((1,), (1,)), ((), ())), precision=lax.Precision.HIGHEST,
                           preferred_element_type=F32)


def _mm(a, w):
    if w.dtype == F32:
        return _dot_f32(a.astype(F32), w)
    return _dot(a.astype(BF16), w)


def _rms(x, g):
    return x * lax.rsqrt(jnp.mean(x * x, axis=-1, keepdims=True) + EPS) * g


def _split2(x):
    hi = x.astype(BF16)
    lo = (x - hi.astype(F32)).astype(BF16)
    return hi, lo


def _split3(x):
    hi = x.astype(BF16)
    r = x - hi.astype(F32)
    mid = r.astype(BF16)
    lo = (r - mid.astype(F32)).astype(BF16)
    return hi, mid, lo


def _sel_left(m, x, parts):
    out = None
    for p in parts(x):
        t = _dot(m, p)
        out = t if out is None else out + t
    return out


def _sel_right(x, m, parts):
    out = None
    for p in parts(x):
        t = _dot(p, m)
        out = t if out is None else out + t
    return out


def _rows_to_cols(x):
    n, w = x.shape
    if n < LANES:
        x = jnp.concatenate([x, jnp.zeros((LANES - n, w), x.dtype)], axis=0)
    return x.T


def _softplus(x):
    return jnp.maximum(x, 0.0) + jnp.log1p(jnp.exp(-jnp.abs(x)))


def _layernorm(x, g, b):
    xc = x - jnp.mean(x, axis=-1, keepdims=True)
    return xc * lax.rsqrt(jnp.mean(xc * xc, axis=-1, keepdims=True) + EPS) * g + b


def _group_rms(x, g, width):
    parts = []
    for s in range(0, x.shape[-1], width):
        t = x[:, s:s + width]
        parts.append(t * lax.rsqrt(jnp.mean(t * t, axis=-1, keepdims=True) + EPS))
    return jnp.concatenate(parts, axis=-1) * g


def _norm_matmul_kernel(x_ref, g_ref, w_ref, o_ref):
    o_ref[...] = _mm(_rms(x_ref[...], g_ref[...]), w_ref[...])


def norm_matmul(x, g, w, tm, tn=None):
    M, K = x.shape
    N = w.shape[1]
    tn = N if tn is None else tn
    return pl.pallas_call(
        _norm_matmul_kernel,
        grid=(M // tm, N // tn),
        in_specs=[pl.BlockSpec((tm, K), lambda i, j: (i, 0)),
                  pl.BlockSpec((1, K), lambda i, j: (0, 0)),
                  pl.BlockSpec((K, tn), lambda i, j: (0, j))],
        out_specs=pl.BlockSpec((tm, tn), lambda i, j: (i, j)),
        out_shape=jax.ShapeDtypeStruct((M, N), F32),
        compiler_params=_cparams(("parallel", "parallel")),
        name="norm_matmul",
    )(x, g.reshape(1, K), w)


def _mem_kv_kernel(x_ref, g_ref, wk_ref, wv_ref, k_ref, v_ref, k5_ref, v5_ref):
    h = _rms(x_ref[...], g_ref[0]).astype(BF16)
    for w_ref, o_ref, o5_ref in ((wk_ref, k_ref, k5_ref), (wv_ref, v_ref, v5_ref)):
        r = _dot(h, w_ref[0])
        o_ref[0] = r
        for hd in range(X_HEADS):
            o5_ref[0, :, hd, :] = r[:, hd * X_HEAD_DIM:(hd + 1) * X_HEAD_DIM]


def mem_kv(mem, g, wk, wv, tm):
    M, K = mem.shape
    depth = g.shape[0]
    wspec = pl.BlockSpec((1, K, K), lambda l, i: (l, 0, 0))
    ospec = pl.BlockSpec((1, tm, K), lambda l, i: (l, i, 0))
    o5spec = pl.BlockSpec((1, tm, X_HEADS, X_HEAD_DIM), lambda l, i: (l, i, 0, 0))
    return pl.pallas_call(
        _mem_kv_kernel,
        grid=(depth, M // tm),
        in_specs=[pl.BlockSpec((tm, K), lambda l, i: (i, 0)),
                  pl.BlockSpec((1, 1, K), lambda l, i: (l, 0, 0)), wspec, wspec],
        out_specs=[ospec, ospec, o5spec, o5spec],
        out_shape=[jax.ShapeDtypeStruct((depth, M, K), F32)] * 2
        + [jax.ShapeDtypeStruct((depth, M, X_HEADS, X_HEAD_DIM), F32)] * 2,
        compiler_params=_cparams(("parallel", "parallel")),
        name="mem_kv",
    )(mem, g.reshape(depth, 1, K), wk, wv)


def _matmul_norm_res_kernel(a_ref, w_ref, g_ref, r_ref, o_ref, acc_ref):
    k = pl.program_id(1)

    @pl.when(k == 0)
    def _():
        acc_ref[...] = jnp.zeros_like(acc_ref)

    acc_ref[...] += _mm(a_ref[...], w_ref[...])

    @pl.when(k == pl.num_programs(1) - 1)
    def _():
        o_ref[...] = r_ref[...] + _rms(acc_ref[...], g_ref[...])


def _matmul_norm_res_whole_k_kernel(a_ref, w_ref, g_ref, r_ref, o_ref):
    o_ref[...] = r_ref[...] + _rms(_mm(a_ref[...], w_ref[...]), g_ref[...])


def matmul_norm_res(a, w, g, res, tm, tk=None):
    M, K = a.shape
    N = w.shape[1]
    tk = K if tk is None else tk
    if tk == K:
        return pl.pallas_call(
            _matmul_norm_res_whole_k_kernel,
            grid=(M // tm,),
            in_specs=[pl.BlockSpec((tm, K), lambda i: (i, 0)),
                      pl.BlockSpec((K, N), lambda i: (0, 0)),
                      pl.BlockSpec((1, N), lambda i: (0, 0)),
                      pl.BlockSpec((tm, N), lambda i: (i, 0))],
            out_specs=pl.BlockSpec((tm, N), lambda i: (i, 0)),
            out_shape=jax.ShapeDtypeStruct((M, N), F32),
            compiler_params=_cparams(("parallel",)),
            name="matmul_norm_res",
        )(a, w, g.reshape(1, N), res)
    return pl.pallas_call(
        _matmul_norm_res_kernel,
        grid=(M // tm, K // tk),
        in_specs=[pl.BlockSpec((tm, tk), lambda i, k: (i, k)),
                  pl.BlockSpec((tk, N), lambda i, k: (k, 0)),
                  pl.BlockSpec((1, N), lambda i, k: (0, 0)),
                  pl.BlockSpec((tm, N), lambda i, k: (i, 0))],
        out_specs=pl.BlockSpec((tm, N), lambda i, k: (i, 0)),
        out_shape=jax.ShapeDtypeStruct((M, N), F32),
        scratch_shapes=[pltpu.VMEM((tm, N), F32)],
        compiler_params=_cparams(("parallel", "arbitrary")),
        name="matmul_norm_res",
    )(a, w, g.reshape(1, N), res)


def _top2(h, router, three_pass=False):
    if three_pass:
        hh, hl = _split2(h)
        rh, rl = _split2(router)
        lg = _dot(hh, rh) + _dot(hh, rl) + _dot(hl, rh)
    else:
        lg = _dot_f32(h, router)
    lane = lax.broadcasted_iota(jnp.int32, lg.shape, 1)
    lg = jnp.where(lane < N_EXPERTS, lg, -jnp.inf)
    m1 = jnp.max(lg, axis=-1, keepdims=True)
    i1 = jnp.min(jnp.where(lg == m1, lane, LANES), axis=-1, keepdims=True)
    lg2 = jnp.where(lane == i1, -jnp.inf, lg)
    m2 = jnp.max(lg2, axis=-1, keepdims=True)
    i2 = jnp.min(jnp.where(lg2 == m2, lane, LANES), axis=-1, keepdims=True)
    e2 = jnp.exp(m2 - m1)
    den = 1.0 + e2
    return lane, i1, i2, 1.0 / den, e2 / den


def _top2_gates(h, router):
    lane, i1, i2, w1, w2 = _top2(h, router)
    return jnp.where(lane == i1, w1, 0.0) + jnp.where(lane == i2, w2, 0.0)


def _ffn_kernel(x_ref, gpre_ref, gpost_ref, router_ref, wg_ref, wu_ref, wd_ref, o_ref,
                h_ref, acc_ref, gates_ref, *, moe):
    e = pl.program_id(1)
    f = pl.program_id(2)

    @pl.when((e == 0) & (f == 0))
    def _():
        h = _rms(x_ref[...], gpre_ref[...])
        h_ref[...] = h.astype(h_ref.dtype)
        acc_ref[...] = jnp.zeros_like(acc_ref)
        if moe:
            gates_ref[...] = _top2_gates(h, router_ref[...])
        else:
            gates_ref[...] = jnp.zeros_like(gates_ref)

    hb = h_ref[...]
    act = jax.nn.silu(_mm(hb, wg_ref[0].astype(hb.dtype))) * _mm(hb, wu_ref[0].astype(hb.dtype))
    if moe:
        gates = gates_ref[...]
        lane = lax.broadcasted_iota(jnp.int32, gates.shape, 1)
        act = act * jnp.sum(jnp.where(lane == e, gates, 0.0), axis=-1, keepdims=True)
    acc_ref[...] += _mm(act, wd_ref[0].astype(hb.dtype))

    @pl.when((e == pl.num_programs(1) - 1) & (f == pl.num_programs(2) - 1))
    def _():
        o_ref[...] = x_ref[...] + _rms(acc_ref[...], gpost_ref[...])


def ffn(x, gpre, gpost, router, wg, wu, wd, tm, tf, moe, precise=False):
    M, K = x.shape
    E, _, F = wg.shape
    return pl.pallas_call(
        functools.partial(_ffn_kernel, moe=moe),
        grid=(M // tm, E, F // tf),
        in_specs=[pl.BlockSpec((tm, K), lambda i, e, f: (i, 0)),
                  pl.BlockSpec((1, K), lambda i, e, f: (0, 0)),
                  pl.BlockSpec((1, K), lambda i, e, f: (0, 0)),
                  pl.BlockSpec((K, LANES), lambda i, e, f: (0, 0)),
                  pl.BlockSpec((1, K, tf), lambda i, e, f: (e, 0, f)),
                  pl.BlockSpec((1, K, tf), lambda i, e, f: (e, 0, f)),
                  pl.BlockSpec((1, tf, K), lambda i, e, f: (e, f, 0))],
        out_specs=pl.BlockSpec((tm, K), lambda i, e, f: (i, 0)),
        out_shape=jax.ShapeDtypeStruct((M, K), F32),
        scratch_shapes=[pltpu.VMEM((tm, K), F32 if precise else BF16), pltpu.VMEM((tm, K), F32), pltpu.VMEM((tm, LANES), F32)],
        compiler_params=_cparams(("parallel", "arbitrary", "arbitrary")),
        name="moe_ffn" if moe else "dense_ffn",
    )(x, gpre.reshape(1, K), gpost.reshape(1, K), router, wg, wu, wd)


INFO_E1, INFO_E2, INFO_R1, INFO_R2, INFO_W1, INFO_W2 = range(6)


def _route_kernel(x_ref, gpre_ref, router_ref, tril_ref, info_ref, cnt_ref, carry_ref):
    i = pl.program_id(0)

    @pl.when(i == 0)
    def _():
        carry_ref[...] = jnp.zeros_like(carry_ref)

    rb = tril_ref.shape[0]
    blocks = [slice(r, r + rb) for r in range(0, x_ref.shape[0], rb)]
    tops = [_top2(_rms(x_ref[rows, :], gpre_ref[...]), router_ref[...], three_pass=True) for rows in blocks]
    sels = [jnp.where((lane == i1) | (lane == i2), 1.0, 0.0) for lane, i1, i2, _, _ in tops]
    incls = [_dot(tril_ref[...], sel.astype(BF16)) for sel in sels]
    base = carry_ref[...]
    for rows, (lane, i1, i2, w1, w2), sel, incl in zip(blocks, tops, sels, incls):
        rank = incl - sel + base
        r1 = jnp.sum(jnp.where(lane == i1, rank, 0.0), axis=-1, keepdims=True)
        r2 = jnp.sum(jnp.where(lane == i2, rank, 0.0), axis=-1, keepdims=True)
        info = jnp.zeros(sel.shape, F32)
        for idx, val in ((INFO_E1, i1.astype(F32)), (INFO_E2, i2.astype(F32)), (INFO_R1, r1), (INFO_R2, r2),
                         (INFO_W1, w1), (INFO_W2, w2)):
            info = jnp.where(lane == idx, val, info)
        info_ref[rows, :] = info
        base = base + incl[rb - 1:, :]
    carry_ref[...] = base
    cnt_ref[...] = base


def moe_route(x, gpre, router, tm):
    M, K = x.shape
    rb = ROUTE_ROWS
    tril = (jnp.arange(rb)[None, :] <= jnp.arange(rb)[:, None]).astype(BF16)
    return pl.pallas_call(
        _route_kernel,
        grid=(M // tm,),
        in_specs=[pl.BlockSpec((tm, K), lambda i: (i, 0)),
                  pl.BlockSpec((1, K), lambda i: (0, 0)),
                  pl.BlockSpec((K, LANES), lambda i: (0, 0)),
                  pl.BlockSpec((rb, rb), lambda i: (0, 0))],
        out_specs=[pl.BlockSpec((tm, LANES), lambda i: (i, 0)),
                   pl.BlockSpec((1, LANES), lambda i: (0, 0))],
        out_shape=[jax.ShapeDtypeStruct((M, LANES), F32), jax.ShapeDtypeStruct((1, LANES), F32)],
        scratch_shapes=[pltpu.VMEM((1, LANES), F32)],
        compiler_params=_cparams(("arbitrary",)),
        name="moe_route",
    )(x, gpre.reshape(1, K), router, tril)


def _dispatch_kernel(dest_ref, x_ref, zeros_ref, xs_ref, sem):
    del zeros_ref
    tm = x_ref.shape[0]
    base = pl.program_id(0) * tm

    def issue(r, carry):
        for k in range(2):
            d = dest_ref[(base + r) * 2 + k]
            pltpu.make_async_copy(x_ref.at[pl.ds(r, 1), :], xs_ref.at[pl.ds(d, 1), :], sem).start()
        return carry

    lax.fori_loop(0, tm, issue, 0, unroll=DMA_UNROLL)
    for k in range(2):
        pltpu.make_async_copy(x_ref, xs_ref.at[pl.ds(0, tm), :], sem).wait()


def moe_dispatch(x, dest, n_slots, tm):
    M, K = x.shape
    return pl.pallas_call(
        _dispatch_kernel,
        grid_spec=pltpu.PrefetchScalarGridSpec(
            num_scalar_prefetch=1,
            grid=(M // tm,),
            in_specs=[pl.BlockSpec((tm, K), lambda i, dest: (i, 0)),
                      pl.BlockSpec(memory_space=pl.ANY)],
            out_specs=pl.BlockSpec(memory_space=pl.ANY),
            scratch_shapes=[pltpu.SemaphoreType.DMA(())]),
        out_shape=jax.ShapeDtypeStruct((n_slots, K), F32),
        input_output_aliases={2: 0},
        compiler_params=_cparams(("arbitrary",)),
        name="moe_dispatch",
    )(dest, x, jnp.zeros((n_slots, K), F32))


def _experts_kernel(te_ref, tv_ref, xs_ref, gpre_ref, wg_ref, wu_ref, wd_ref, o_ref, h_ref, acc_ref):
    del te_ref
    i = pl.program_id(0)
    f = pl.program_id(1)
    last = pl.num_programs(1) - 1

    @pl.when(tv_ref[i] == 1)
    def _():
        @pl.when(f == 0)
        def _():
            h_ref[...] = _rms(xs_ref[...], gpre_ref[...]).astype(BF16)
            acc_ref[...] = jnp.zeros_like(acc_ref)

        hb = h_ref[...]
        act = jax.nn.silu(_dot(hb, wg_ref[0].astype(BF16))) * _dot(hb, wu_ref[0].astype(BF16))
        acc_ref[...] += _dot(act.astype(BF16), wd_ref[0].astype(BF16))

        @pl.when(f == last)
        def _():
            o_ref[...] = acc_ref[...]

    @pl.when((tv_ref[i] == 0) & (f == last))
    def _():
        o_ref[...] = jnp.zeros_like(o_ref)


def moe_experts(xs, gpre, tile_expert, tile_valid, wg, wu, wd, tg, tf):
    S, K = xs.shape
    F = wg.shape[2]
    nf = F // tf
    fidx = lambda i, f, te, tv: jnp.where(tv[i] == 1, f, nf - 1)
    return pl.pallas_call(
        _experts_kernel,
        grid_spec=pltpu.PrefetchScalarGridSpec(
            num_scalar_prefetch=2,
            grid=(S // tg, nf),
            in_specs=[pl.BlockSpec((tg, K), lambda i, f, te, tv: (i, 0)),
                      pl.BlockSpec((1, K), lambda i, f, te, tv: (0, 0)),
                      pl.BlockSpec((1, K, tf), lambda i, f, te, tv: (te[i], 0, fidx(i, f, te, tv))),
                      pl.BlockSpec((1, K, tf), lambda i, f, te, tv: (te[i], 0, fidx(i, f, te, tv))),
                      pl.BlockSpec((1, tf, K), lambda i, f, te, tv: (te[i], fidx(i, f, te, tv), 0))],
            out_specs=pl.BlockSpec((tg, K), lambda i, f, te, tv: (i, 0)),
            scratch_shapes=[pltpu.VMEM((tg, K), BF16), pltpu.VMEM((tg, K), F32)]),
        out_shape=jax.ShapeDtypeStruct((S, K), F32),
        compiler_params=_cparams(("parallel", "arbitrary")),
        name="moe_experts",
    )(tile_expert, tile_valid, xs, gpre.reshape(1, K), wg, wu, wd)


def _combine_kernel(dest_ref, x_ref, info_ref, gpost_ref, ys_ref, o_ref, buf_ref, sem):
    tm = x_ref.shape[0]
    i = pl.program_id(0)

    def gather(tile, slot):
        def issue(r, carry):
            for k in range(2):
                d = dest_ref[(tile * tm + r) * 2 + k]
                pltpu.make_async_copy(ys_ref.at[pl.ds(d, 1), :], buf_ref.at[slot, k, pl.ds(r, 1), :],
                                      sem.at[slot]).start()
            return carry

        lax.fori_loop(0, tm, issue, 0, unroll=DMA_UNROLL)

    @pl.when(i == 0)
    def _():
        gather(0, 0)

    @pl.when(i + 1 < pl.num_programs(0))
    def _():
        gather(i + 1, (i + 1) % 2)

    slot = i % 2
    for k in range(2):
        pltpu.make_async_copy(ys_ref.at[pl.ds(0, tm), :], buf_ref.at[slot, k], sem.at[slot]).wait()
    info = info_ref[...]
    y = info[:, INFO_W1:INFO_W1 + 1] * buf_ref[slot, 0] + info[:, INFO_W2:INFO_W2 + 1] * buf_ref[slot, 1]
    o_ref[...] = x_ref[...] + _rms(y, gpost_ref[...])


def moe_combine(x, info, gpost, ys, dest, tm):
    M, K = x.shape
    return pl.pallas_call(
        _combine_kernel,
        grid_spec=pltpu.PrefetchScalarGridSpec(
            num_scalar_prefetch=1,
            grid=(M // tm,),
            in_specs=[pl.BlockSpec((tm, K), lambda i, dest: (i, 0)),
                      pl.BlockSpec((tm, LANES), lambda i, dest: (i, 0)),
                      pl.BlockSpec((1, K), lambda i, dest: (0, 0)),
                      pl.BlockSpec(memory_space=pl.ANY)],
            out_specs=pl.BlockSpec((tm, K), lambda i, dest: (i, 0)),
            scratch_shapes=[pltpu.VMEM((2, 2, tm, K), F32), pltpu.SemaphoreType.DMA((2,))]),
        out_shape=jax.ShapeDtypeStruct((M, K), F32),
        compiler_params=_cparams(("arbitrary",)),
        name="moe_combine",
    )(dest, x, info, gpost.reshape(1, K), ys)


def moe_routed(x, gpre, gpost, router, wg, wu, wd):
    M, K = x.shape
    tg = TG_MOE
    n_slots = 2 * M + N_EXPERTS * tg
    info, cnt = moe_route(x, gpre, router, TM_ROUTE)
    cnt = cnt[0, :N_EXPERTS].astype(jnp.int32)
    padded = (cnt + tg - 1) // tg * tg
    ends = jnp.cumsum(padded)
    offs = ends - padded
    ids = info[:, INFO_E1:INFO_E2 + 1].astype(jnp.int32)
    ranks = info[:, INFO_R1:INFO_R2 + 1].astype(jnp.int32)
    dest = (offs[ids] + ranks).reshape(2 * M)
    tile_start = jnp.arange(n_slots // tg, dtype=jnp.int32) * tg
    tile_valid = (tile_start < ends[-1]).astype(jnp.int32)
    tile_expert = jnp.minimum(jnp.sum((tile_start[:, None] >= ends[None, :]).astype(jnp.int32), axis=1),
                              N_EXPERTS - 1)
    tile_expert = jnp.where(tile_valid == 1, tile_expert, tile_expert[jnp.maximum(ends[-1] // tg - 1, 0)])
    xs = moe_dispatch(x, dest, n_slots, TM_DISPATCH)
    ys = moe_experts(xs, gpre, tile_expert, tile_valid, wg, wu, wd, tg, TF_ROUTED)
    return moe_combine(x, info, gpost, ys, dest, TM_COMBINE)


def _xattn_kernel(x_ref, k_ref, v_ref, wq_ref, wo_ref, gpre_ref, gpost_ref, o_ref):
    x = x_ref[0]
    h = _rms(x, gpre_ref[...]).astype(BF16)
    q = _dot(h, wq_ref[...])
    k = k_ref[0, 0].astype(BF16)
    v = v_ref[0, 0].astype(BF16)
    heads = [slice(hd * X_HEAD_DIM, (hd + 1) * X_HEAD_DIM) for hd in range(X_HEADS)]
    ss = [_dot_nt(q[:, sl].astype(BF16), k[:, sl]) * (X_HEAD_DIM ** -0.5) for sl in heads]
    es = [jnp.exp(s - jnp.max(s, axis=-1, keepdims=True)) for s in ss]
    ps = [(e / jnp.sum(e, axis=-1, keepdims=True)).astype(BF16) for e in es]
    o = jnp.concatenate([_dot(p, v[:, sl]) for p, sl in zip(ps, heads)], axis=-1).astype(BF16)
    o_ref[0] = x + _rms(_dot(o, wo_ref[...]), gpost_ref[...])


def xattn_prompt(x, k, v, l, wq, wo, gpre, gpost, tm):
    B, T, K = x.shape
    return pl.pallas_call(
        _xattn_kernel,
        grid=(B, T // tm),
        in_specs=[pl.BlockSpec((1, tm, K), lambda b, t: (b, t, 0)),
                  pl.BlockSpec((1, 1, MEM_LEN, K), lambda b, t: (l, b, 0, 0)),
                  pl.BlockSpec((1, 1, MEM_LEN, K), lambda b, t: (l, b, 0, 0)),
                  pl.BlockSpec((K, K), lambda b, t: (0, 0)),
                  pl.BlockSpec((K, K), lambda b, t: (0, 0)),
                  pl.BlockSpec((1, K), lambda b, t: (0, 0)),
                  pl.BlockSpec((1, K), lambda b, t: (0, 0))],
        out_specs=pl.BlockSpec((1, tm, K), lambda b, t: (b, t, 0)),
        out_shape=jax.ShapeDtypeStruct((B, T, K), F32),
        compiler_params=_cparams(("parallel", "parallel")),
        name="xattn_prompt",
    )(x, k, v, wq, wo, gpre.reshape(1, K), gpost.reshape(1, K))


def _xattn_sample_kernel(q_ref, k_ref, v_ref, o_ref):
    mc = XATTN_MEM_CHUNK
    chunks = [slice(c * mc, (c + 1) * mc) for c in range(MEM_LEN // mc)]
    for j in range(q_ref.shape[0]):
        q = q_ref[j]
        s = jnp.concatenate([jnp.sum(k_ref[0, j, rows] * q[None], axis=-1, keepdims=True) for rows in chunks],
                            axis=0) * (X_HEAD_DIM ** -0.5)
        e = jnp.exp(s - jnp.max(s, axis=0, keepdims=True))
        p = e / jnp.sum(e, axis=0, keepdims=True)
        o = None
        for rows in chunks:
            part = jnp.sum(p[rows] * v_ref[0, j, rows], axis=0)
            o = part if o is None else o + part
        o_ref[j] = o


def xattn_sample_core(q, k, v, l, nb):
    B = q.shape[0]
    kv_spec = pl.BlockSpec((1, nb, MEM_LEN, X_HEADS, X_HEAD_DIM), lambda b: (l, b, 0, 0, 0))
    return pl.pallas_call(
        _xattn_sample_kernel,
        grid=(B // nb,),
        in_specs=[pl.BlockSpec((nb, X_HEADS, X_HEAD_DIM), lambda b: (b, 0, 0)), kv_spec, kv_spec],
        out_specs=pl.BlockSpec((nb, X_HEADS, X_HEAD_DIM), lambda b: (b, 0, 0)),
        out_shape=jax.ShapeDtypeStruct((B, X_HEADS, X_HEAD_DIM), F32),
        compiler_params=_cparams(("parallel",)),
        name="xattn_sample",
    )(q, k, v)


def _ev_proj_kernel(x_ref, g_ref, w_ref, lng_ref, lnb_ref, cw_ref, cb_ref,
                    gu_ref, vln_ref, sz_ref, xa_ref, dt_ref, tail_ref, xp_ref):
    t = pl.program_id(1)
    tm = x_ref.shape[1]

    @pl.when(t == 0)
    def _():
        xp_ref[0:SUBLANES, :] = jnp.zeros((SUBLANES, B_CONV_DIM), F32)

    h = _rms(x_ref[0], g_ref[...]).astype(BF16)
    u = _dot(h, w_ref[:, 0:D])
    v = _dot(h, w_ref[:, D:2 * D])
    gu_ref[0] = jax.nn.gelu(u)
    z = _dot(h, w_ref[:, 2 * D:3 * D])
    vln_ref[0] = _layernorm(jax.nn.gelu(v), lng_ref[...], lnb_ref[...]).astype(BF16)
    x = _dot(h, w_ref[:, 3 * D:3 * D + B_CONV_DIM])
    sz_ref[0] = jax.nn.silu(z)
    dt_ref[0] = _dot(h, w_ref[:, DT_COL_BLOCK * LANES:(DT_COL_BLOCK + 1) * LANES])
    xp_ref[SUBLANES:SUBLANES + tm, :] = x
    conv = cb_ref[...] + cw_ref[B_CONV - 1:B_CONV, :] * x
    for k in range(B_CONV - 1):
        conv = conv + cw_ref[k:k + 1, :] * xp_ref[pl.ds(SUBLANES - (B_CONV - 1) + k, tm), :]
    xp_ref[0:SUBLANES, :] = x[tm - SUBLANES:tm, :]
    tail_ref[0] = x[tm - SUBLANES:tm, :]
    xa_ref[0] = jax.nn.silu(conv)


def ev_proj(x, g, w, P, tm):
    B, T, K = x.shape
    row = lambda a: a.reshape(1, -1)
    full = lambda shape: pl.BlockSpec(shape, lambda b, t: (0,) * len(shape))
    tile = lambda n: pl.BlockSpec((1, tm, n), lambda b, t: (b, t, 0))
    return pl.pallas_call(
        _ev_proj_kernel,
        grid=(B, T // tm),
        in_specs=[tile(K), full((1, K)), full((K, IN0_PAD)), full((1, D)), full((1, D)),
                  full((B_CONV, B_CONV_DIM)), full((1, B_CONV_DIM))],
        out_specs=[tile(D), tile(D), tile(D), tile(B_CONV_DIM), tile(LANES),
                   pl.BlockSpec((1, SUBLANES, B_CONV_DIM), lambda b, t: (b, 0, 0))],
        out_shape=[jax.ShapeDtypeStruct((B, T, D), F32), jax.ShapeDtypeStruct((B, T, D), BF16),
                   jax.ShapeDtypeStruct((B, T, D), F32), jax.ShapeDtypeStruct((B, T, B_CONV_DIM), F32),
                   jax.ShapeDtypeStruct((B, T, LANES), F32), jax.ShapeDtypeStruct((B, SUBLANES, B_CONV_DIM), F32)],
        scratch_shapes=[pltpu.VMEM((SUBLANES + tm, B_CONV_DIM), F32)],
        compiler_params=_cparams(("parallel", "arbitrary")),
        name="ev_proj",
    )(x, row(g), w, row(P["a_ln_g"][0]), row(P["a_ln_b"][0]), P["b_conv_w"][0], row(P["b_conv_b"][0]))


def _mixer_even_kernel(gu_ref, vln_ref, sz_ref, xa_ref, dt_ref, ws_ref, bst_ref,
                       dtb_ref, anar_ref, dexp_ref, bnorm_ref, tril_ref, expand_ref,
                       yab_ref, ssm_ref, st_ref):
    c = pl.program_id(1)
    L = A_CHUNK

    @pl.when(c == 0)
    def _():
        st_ref[...] = jnp.zeros_like(st_ref)

    row = lax.broadcasted_iota(jnp.int32, (L, L), 0)
    col = lax.broadcasted_iota(jnp.int32, (L, L), 1)
    causal = col <= row

    tril = tril_ref[...]
    lane = lax.broadcasted_iota(jnp.int32, (L, LANES), 1)

    def chunk(rows):
        gu = gu_ref[0, rows, :]
        vb = vln_ref[0, rows, :]
        a_heads = [slice(hh * LANES, (hh + 1) * LANES) for hh in range(A_HEADS)]
        mixes = [_dot(jnp.where(causal, ws_ref[hh], 0.0).astype(BF16), vb[:, sl]) for hh, sl in enumerate(a_heads)]
        for hh, sl in enumerate(a_heads):
            yab_ref[0, rows, sl] = (gu[:, sl] * (mixes[hh] + bst_ref[:, hh:hh + 1])).astype(BF16)

        xa = xa_ref[0, rows, :]
        xs = xa[:, :D]
        bm = xa[:, D:D + B_GROUPS * B_STATE]
        cm = xa[:, D + B_GROUPS * B_STATE:]

        dtf = _softplus(dt_ref[0, rows, :] + dtb_ref[...])
        cs_n = _sel_left(tril, dtf * anar_ref[...], _split3)
        dt_x = _sel_right(dtf, expand_ref[...], _split2)
        cs_x = _sel_right(cs_n, expand_ref[...], _split3)
        ecs_x = jnp.exp(cs_x)
        last_x = cs_x[L - 1:L, :]
        xdt = xs * dt_x
        xdt_b = xdt.astype(BF16)
        xdec_b = (xdt * jnp.exp(last_x - cs_x)).astype(BF16)
        groups = range(B_GROUPS)
        gsl = [slice(g * B_GROUP_W, (g + 1) * B_GROUP_W) for g in groups]
        bgs = [bm[:, g * B_STATE:(g + 1) * B_STATE] for g in groups]
        cgs = [cm[:, g * B_STATE:(g + 1) * B_STATE].astype(BF16) for g in groups]
        gmats = [_dot_nt(cgs[g], bgs[g].astype(BF16)) for g in groups]
        sts = [st_ref[g] for g in groups]
        y_offs = [ecs_x[:, gsl[g]] * _dot(cgs[g], sts[g].astype(BF16)) for g in groups]
        for g in groups:
            st_ref[g] = sts[g] * jnp.exp(last_x[:, gsl[g]]) + _dot(bgs[g].T.astype(BF16), xdec_b[:, gsl[g]])
        pairs = [(g, pair) for g in groups for pair in range(B_GROUP_W // LANES)]
        bases = [g * B_GROUP_W + pair * LANES for g, pair in pairs]
        cs_ts = [cs_x[:, base:base + LANES].T for base in bases]
        lhs = []
        for (g, _), base, cs_t in zip(pairs, bases, cs_ts):
            ms = []
            for half in range(2):
                ch = half * B_HEAD_DIM
                diff = cs_x[:, base + ch:base + ch + 1] - cs_t[ch:ch + 1, :]
                ms.append((gmats[g] * jnp.exp(jnp.where(causal, diff, -jnp.inf))).astype(BF16))
            lhs.append(jnp.concatenate(ms, axis=1))
        y_parts = []
        for (g, pair), base, m2 in zip(pairs, bases, lhs):
            xp2 = xdt_b[:, base:base + LANES]
            rhs = jnp.concatenate([jnp.where(lane < B_HEAD_DIM, xp2, jnp.zeros_like(xp2)),
                                   jnp.where(lane >= B_HEAD_DIM, xp2, jnp.zeros_like(xp2))], axis=0)
            y_parts.append(_dot(m2, rhs) + y_offs[g][:, pair * LANES:(pair + 1) * LANES])
        y = jnp.concatenate(y_parts, axis=-1) + dexp_ref[...] * xs
        yb = _group_rms(y * sz_ref[0, rows, :], bnorm_ref[...], B_GROUP_W)
        yab_ref[0, rows, D:] = yb.astype(BF16)

    for sub in range(gu_ref.shape[1] // L):
        chunk(slice(sub * L, (sub + 1) * L))

    @pl.when(c == pl.num_programs(1) - 1)
    def _():
        for g in range(B_GROUPS):
            ssm_ref[0, g] = st_ref[g].T


def _even_consts(P):
    head_of_ch = jnp.arange(D) // B_HEAD_DIM
    expand = (jnp.arange(LANES)[:, None] == head_of_ch[None, :]).astype(BF16)
    tril = (jnp.arange(A_CHUNK)[None, :] <= jnp.arange(A_CHUNK)[:, None]).astype(BF16)
    aexp = jnp.repeat(-jnp.exp(P["b_a_log"][0].astype(F32)), B_HEAD_DIM).reshape(1, D)
    dexp = jnp.repeat(P["b_d"][0].astype(F32), B_HEAD_DIM).reshape(1, D)
    dtb = jnp.pad(P["b_dt_bias"][0].astype(F32), (0, LANES - B_HEADS)).reshape(1, LANES)
    return expand, tril, aexp, dexp, dtb


def mixer_even_prompt(gu, vln, sz, xa, dt, P):
    B, T, _ = gu.shape
    L = A_CHUNK
    expand, tril, _, dexp, dtb = _even_consts(P)
    anar = jnp.pad(-jnp.exp(P["b_a_log"][0].astype(F32)), (0, LANES - B_HEADS)).reshape(1, LANES)
    row = lambda a: a.reshape(1, -1)
    full = lambda shape: pl.BlockSpec(shape, lambda b, c: (0,) * len(shape))
    R = A_ROWS
    chunk = lambda n: pl.BlockSpec((1, R, n), lambda b, c: (b, c, 0))
    return pl.pallas_call(
        _mixer_even_kernel,
        grid=(B, T // R),
        in_specs=[chunk(D), chunk(D), chunk(D), chunk(B_CONV_DIM), chunk(LANES),
                  full((A_HEADS, L, L)), full((L, A_HEADS)), full((1, LANES)), full((1, LANES)), full((1, D)),
                  full((1, D)), full((L, L)), full((LANES, D))],
        out_specs=[pl.BlockSpec((1, R, 2 * D), lambda b, c: (b, c, 0)),
                   pl.BlockSpec((1, B_GROUPS, B_GROUP_W, B_STATE), lambda b, c: (b, 0, 0, 0))],
        out_shape=[jax.ShapeDtypeStruct((B, T, 2 * D), BF16),
                   jax.ShapeDtypeStruct((B, B_GROUPS, B_GROUP_W, B_STATE), F32)],
        scratch_shapes=[pltpu.VMEM((B_GROUPS, B_STATE, B_GROUP_W), F32)],
        compiler_params=_cparams(("parallel", "arbitrary")),
        name="mixer_even_prompt",
    )(gu, vln, sz, xa, dt, P["a_ws"][0], P["a_bs"][0].T, dtb, anar, dexp, row(P["b_norm"][0]), tril, expand)


def _mixer_even_step_kernel(proj_ref, conv_ref, ssm_ref, ws0_ref, bs0_ref, lng_ref, lnb_ref, cw_ref, cb_ref,
                            dtb_ref, aexp_ref, dexp_ref, bnorm_ref, expand_ref,
                            yab_ref, av_ref, convo_ref, ssmo_ref):
    nb = proj_ref.shape[0]
    u = proj_ref[:, 0:D]
    v = proj_ref[:, D:2 * D]
    z = proj_ref[:, 2 * D:3 * D]
    x = proj_ref[:, 3 * D:3 * D + B_CONV_DIM]
    dt = proj_ref[:, DT_COL_BLOCK * LANES:(DT_COL_BLOCK + 1) * LANES]

    vln = _layernorm(jax.nn.gelu(v), lng_ref[...], lnb_ref[...])
    av_ref[...] = vln
    yab_ref[:, 0:D] = jax.nn.gelu(u) * (ws0_ref[...] * vln + bs0_ref[...])

    conv = cb_ref[...] + cw_ref[B_CONV - 1:B_CONV, :] * x
    for k in range(B_CONV - 1):
        conv = conv + cw_ref[k:k + 1, :] * conv_ref[k]
        if k > 0:
            convo_ref[k - 1] = conv_ref[k]
    convo_ref[B_CONV - 2] = x
    xa = jax.nn.silu(conv)
    xs = xa[:, :D]
    bm = xa[:, D:D + B_GROUPS * B_STATE]
    cm = xa[:, D + B_GROUPS * B_STATE:]
    dtf = _softplus(dt + dtb_ref[...])
    dt_x = _sel_right(dtf, expand_ref[...], _split3)
    dec_x = jnp.exp(dt_x * aexp_ref[...])
    xdt = xs * dt_x
    y_rows = []
    for g in range(B_GROUPS):
        gs = slice(g * B_GROUP_W, (g + 1) * B_GROUP_W)
        dec_t = _rows_to_cols(dec_x[:, gs])
        xdt_t = _rows_to_cols(xdt[:, gs])
        s_news = [ssm_ref[j, g] * dec_t[:, j:j + 1] + xdt_t[:, j:j + 1] * bm[j:j + 1, g * B_STATE:(g + 1) * B_STATE]
                  for j in range(nb)]
        for j in range(nb):
            ssmo_ref[j, g] = s_news[j]
        cjs = [jnp.broadcast_to(cm[j:j + 1, g * B_STATE:(g + 1) * B_STATE], (SUBLANES, B_STATE)) for j in range(nb)]
        y_rows.append(jnp.concatenate([_dot_nt_f32(cjs[j], s_news[j])[0:1, :] for j in range(nb)], axis=0))
    y = jnp.concatenate(y_rows, axis=-1) + dexp_ref[...] * xs
    yb = _group_rms(y * jax.nn.silu(z), bnorm_ref[...], B_GROUP_W)
    yab_ref[:, D:] = yb


def mixer_even_step(proj, conv0, ssm0, P, nb):
    B = proj.shape[0]
    expand, _, aexp, dexp, dtb = _even_consts(P)
    row = lambda a: a.reshape(1, -1)
    rep = lambda a: jnp.repeat(a.astype(F32), LANES).reshape(1, D)
    full = lambda shape: pl.BlockSpec(shape, lambda i: (0,) * len(shape))
    return pl.pallas_call(
        _mixer_even_step_kernel,
        grid=(B // nb,),
        in_specs=[pl.BlockSpec((nb, proj.shape[1]), lambda i: (i, 0)),
                  pl.BlockSpec((B_CONV - 1, nb, B_CONV_DIM), lambda i: (0, i, 0)),
                  pl.BlockSpec((nb, B_GROUPS, B_GROUP_W, B_STATE), lambda i: (i, 0, 0, 0)),
                  full((1, D)), full((1, D)), full((1, D)), full((1, D)),
                  full((B_CONV, B_CONV_DIM)), full((1, B_CONV_DIM)), full((1, LANES)), full((1, D)), full((1, D)),
                  full((1, D)), full((LANES, D))],
        out_specs=[pl.BlockSpec((nb, 2 * D), lambda i: (i, 0)),
                   pl.BlockSpec((nb, D), lambda i: (i, 0)),
                   pl.BlockSpec((B_CONV - 1, nb, B_CONV_DIM), lambda i: (0, i, 0)),
                   pl.BlockSpec((nb, B_GROUPS, B_GROUP_W, B_STATE), lambda i: (i, 0, 0, 0))],
        out_shape=[jax.ShapeDtypeStruct((B, 2 * D), F32),
                   jax.ShapeDtypeStruct((B, D), F32),
                   jax.ShapeDtypeStruct((B_CONV - 1, B, B_CONV_DIM), F32),
                   jax.ShapeDtypeStruct((B, B_GROUPS, B_GROUP_W, B_STATE), F32)],
        compiler_params=_cparams(("parallel",)),
        name="mixer_even_step",
    )(proj, conv0, ssm0, rep(P["a_ws"][0][:, 0, 0]), rep(P["a_bs"][0][:, 0]), row(P["a_ln_g"][0]), row(P["a_ln_b"][0]),
      P["b_conv_w"][0], row(P["b_conv_b"][0]), dtb, aexp, dexp, row(P["b_norm"][0]), expand)


def _gla_gates(q_raw, f_raw, lb):
    fg = lb + (1.0 - lb) * jax.nn.sigmoid(f_raw)
    return jax.nn.silu(q_raw), fg, 1.0 - fg


def _hgrn_kernel(q_ref, f_ref, i_ref, g_ref, lb_ref, cnorm_ref, tril_ref, o_ref, s_ref, st_ref):
    c = pl.program_id(1)
    L = C_CHUNK
    R = C_ROWS

    @pl.when(c == 0)
    def _():
        st_ref[...] = jnp.zeros_like(st_ref)

    row = lax.broadcasted_iota(jnp.int32, (R, R), 0)
    col = lax.broadcasted_iota(jnp.int32, (R, R), 1)
    causal = (col <= row) & (row // L == col // L)
    chunk_of_row = lax.broadcasted_iota(jnp.int32, (R, 1), 0) // L
    tril = tril_ref[...]
    heads = [slice(hh * C_KDIM, (hh + 1) * C_KDIM) for hh in range(C_HEADS)]

    def block(rows):
        q, fg, k = _gla_gates(q_ref[0, rows, :], f_ref[0, rows, :], lb_ref[...])
        v = i_ref[0, rows, :]
        bc = _sel_left(tril, jnp.log(fg), _split3)
        q_in = (q * jnp.exp(bc)).astype(BF16)
        k_in = (k * jnp.exp(-bc)).astype(BF16)
        vb = v.astype(BF16)
        btots, k_decs = [], []
        for s in range(R // L):
            btot = bc[(s + 1) * L - 1:(s + 1) * L, :]
            btots.append(btot)
            k_decs.append(jnp.where(chunk_of_row == s, k * jnp.exp(btot - bc), 0.0).astype(BF16))
        atts = [jnp.where(causal, _dot_nt(q_in[:, sl], k_in[:, sl]), 0.0).astype(BF16) for sl in heads]
        v_ts = [v[:, sl].T.astype(BF16) for sl in heads]
        sts = [st_ref[hh] for hh in range(C_HEADS)]
        inters = [[] for _ in heads]
        for s in range(R // L):
            for hh, sl in enumerate(heads):
                inters[hh].append(_dot_nt(q_in[s * L:(s + 1) * L, sl], sts[hh].astype(BF16)))
            sts = [sts[hh] * jnp.exp(btots[s][:, sl]) + _dot(v_ts[hh], k_decs[s][:, sl])
                   for hh, sl in enumerate(heads)]
        for hh in range(C_HEADS):
            st_ref[hh] = sts[hh]
        outs = [_dot(atts[hh], vb[:, sl]) + jnp.concatenate(inters[hh], axis=0) for hh, sl in enumerate(heads)]
        o = _group_rms(jnp.concatenate(outs, axis=-1), cnorm_ref[...], C_KDIM)
        o_ref[0, rows, :] = (o * jax.nn.silu(g_ref[0, rows, :])).astype(BF16)

    for blk in range(q_ref.shape[1] // R):
        block(slice(blk * R, (blk + 1) * R))

    @pl.when(c == pl.num_programs(1) - 1)
    def _():
        for hh in range(C_HEADS):
            s_ref[0, hh] = st_ref[hh].T


def hgrn_prompt(proj, lb, cnorm):
    B, T, _ = proj.shape
    L = C_ROWS
    r = jnp.arange(L)
    tril = ((r[None, :] <= r[:, None]) & (r[None, :] // C_CHUNK == r[:, None] // C_CHUNK)).astype(BF16)
    full = lambda shape: pl.BlockSpec(shape, lambda b, c: (0,) * len(shape))
    S = C_STEP_ROWS
    return pl.pallas_call(
        _hgrn_kernel,
        grid=(B, T // S),
        in_specs=[pl.BlockSpec((1, S, D), lambda b, c: (b, c, 0)),
                  pl.BlockSpec((1, S, D), lambda b, c: (b, c, 1)),
                  pl.BlockSpec((1, S, D), lambda b, c: (b, c, 2)),
                  pl.BlockSpec((1, S, D), lambda b, c: (b, c, 3)),
                  full((1, D)), full((1, D)), full((L, L))],
        out_specs=[pl.BlockSpec((1, S, D), lambda b, c: (b, c, 0)),
                   pl.BlockSpec((1, C_HEADS, C_KDIM, C_KDIM), lambda b, c: (b, 0, 0, 0))],
        out_shape=[jax.ShapeDtypeStruct((B, T, D), BF16),
                   jax.ShapeDtypeStruct((B, C_HEADS, C_KDIM, C_KDIM), F32)],
        scratch_shapes=[pltpu.VMEM((C_HEADS, C_KDIM, C_KDIM), F32)],
        compiler_params=_cparams(("parallel", "arbitrary")),
        name="hgrn_prompt",
    )(proj, proj, proj, proj, lb.reshape(1, D), cnorm.reshape(1, D), tril)


def _hgrn_step_kernel(proj_ref, s_ref, lb_ref, cnorm_ref, o_ref, so_ref):
    nb = proj_ref.shape[0]
    q, fg, k = _gla_gates(proj_ref[:, 0:D], proj_ref[:, D:2 * D], lb_ref[...])
    v = proj_ref[:, 2 * D:3 * D]
    g = proj_ref[:, 3 * D:4 * D]
    heads = [slice(hh * C_KDIM, (hh + 1) * C_KDIM) for hh in range(C_HEADS)]
    fg_ts = [_rows_to_cols(fg[:, sl]) for sl in heads]
    k_ts = [_rows_to_cols(k[:, sl]) for sl in heads]
    outs = []
    for hh, sl in enumerate(heads):
        s_news = [s_ref[j, hh] * fg_ts[hh][:, j:j + 1] + k_ts[hh][:, j:j + 1] * v[j:j + 1, sl] for j in range(nb)]
        for j in range(nb):
            so_ref[j, hh] = s_news[j]
        qjs = [jnp.broadcast_to(q[j:j + 1, sl], (SUBLANES, C_KDIM)) for j in range(nb)]
        outs.append(jnp.concatenate([_dot_f32(qjs[j], s_news[j])[0:1, :] for j in range(nb)], axis=0))
    o = _group_rms(jnp.concatenate(outs, axis=-1), cnorm_ref[...], C_KDIM)
    o_ref[...] = o * jax.nn.silu(g)


def hgrn_step(proj, s0, lb, cnorm, nb):
    B = proj.shape[0]
    full = lambda shape: pl.BlockSpec(shape, lambda i: (0,) * len(shape))
    return pl.pallas_call(
        _hgrn_step_kernel,
        grid=(B // nb,),
        in_specs=[pl.BlockSpec((nb, 4 * D), lambda i: (i, 0)),
                  pl.BlockSpec((nb, C_HEADS, C_KDIM, C_KDIM), lambda i: (i, 0, 0, 0)),
                  full((1, D)), full((1, D))],
        out_specs=[pl.BlockSpec((nb, D), lambda i: (i, 0)),
                   pl.BlockSpec((nb, C_HEADS, C_KDIM, C_KDIM), lambda i: (i, 0, 0, 0))],
        out_shape=[jax.ShapeDtypeStruct((B, D), F32),
                   jax.ShapeDtypeStruct((B, C_HEADS, C_KDIM, C_KDIM), F32)],
        compiler_params=_cparams(("parallel",)),
        name="hgrn_step",
    )(proj, s0, lb.reshape(1, D), cnorm.reshape(1, D))


TM_PROJ = 512
TM_EV_PROJ = 256
TM_OUT = 1024
TM_ATTN = 1024
TM_FFN = 1024
TF_DENSE = 256
TF_MOE = 896
TF_ROUTED = 512
TG_MOE = 1024
TM_ROUTE = 1024
ROUTE_ROWS = 256
TM_DISPATCH = 2048
TM_COMBINE = 1024
DMA_UNROLL = 8
ROUTED_MIN_TOKENS = 8 * TG_MOE
STEP_NB = 8
XATTN_STEP_NB = 8
XATTN_MEM_CHUNK = 64
TN_STEP = 512
TK_STEP = 512
IN0_STEP_PAD = 5120


def _prep_weights(P):
    W = {}
    W["ev_w_in"] = jnp.pad(P["ev_w_in"][0], ((0, 0), (0, IN0_PAD - IN0))).astype(BF16)
    W["ev_w_out"] = P["ev_w_out"][0].astype(BF16)
    W["od_w_in"] = P["od_w_in"][0].astype(BF16)
    W["od_w_out"] = P["od_w_out"][0].astype(BF16)
    W["xa_wq"] = P["xa_wq"].astype(BF16)
    W["xa_wo"] = P["xa_wo"].astype(BF16)
    W["xa_wk"] = P["xa_wk"].astype(BF16)
    W["xa_wv"] = P["xa_wv"].astype(BF16)
    W["ffn"] = tuple(P[n].astype(BF16) for n in ("ffn_w_gate", "ffn_w_up", "ffn_w_down"))
    W["moe"] = tuple(P[n][0] for n in ("moe_w_gate", "moe_w_up", "moe_w_down"))
    W["router"] = jnp.pad(P["moe_router"][0].astype(F32), ((0, 0), (0, LANES - N_EXPERTS)))
    lbp = jax.nn.softmax(P["hgrn_lb_logits"].astype(F32), axis=0)
    W["lower_bounds"] = jnp.cumsum(lbp, axis=0) - lbp[0]
    return W


def _channel_mix(x2, l, P, W, tm):
    if l == 0:
        wg, wu, wd = W["ffn"]
        return ffn(x2, P["norm_ffn_pre"][l], P["norm_ffn_post"][l], W["router"], wg, wu, wd, tm, TF_DENSE, False)
    wg, wu, wd = W["moe"]
    if x2.shape[0] >= ROUTED_MIN_TOKENS:
        return moe_routed(x2, P["norm_ffn_pre"][l], P["norm_ffn_post"][l], W["router"], wg, wu, wd)
    return ffn(x2, P["norm_ffn_pre"][l], P["norm_ffn_post"][l], W["router"], wg, wu, wd, tm, TF_MOE, True)


def _trunk_prompt(x, mem_k, mem_v, P, W):
    B, T, _ = x.shape
    M = B * T
    x2 = x.reshape(M, D)
    gu, vln, sz, xa, dt, tail = ev_proj(x, P["norm_mix_pre"][0], W["ev_w_in"], P, TM_EV_PROJ)
    yab, ssm = mixer_even_prompt(gu, vln, sz, xa, dt, P)
    conv = tail[:, SUBLANES - (B_CONV - 1):, :]
    x2 = matmul_norm_res(yab.reshape(M, 2 * D), W["ev_w_out"], P["norm_mix_post"][0], x2, TM_OUT)
    x2 = xattn_prompt(x2.reshape(B, T, D), mem_k, mem_v, 0, W["xa_wq"][0], W["xa_wo"][0],
                      P["norm_x_pre"][0], P["norm_x_post"][0], TM_ATTN).reshape(M, D)
    x2 = _channel_mix(x2, 0, P, W, TM_FFN)
    proj = norm_matmul(x2, P["norm_mix_pre"][1], W["od_w_in"], TM_PROJ).reshape(B, T, 4 * D)
    o, hgrn = hgrn_prompt(proj, W["lower_bounds"][1], P["c_norm"][0])
    x2 = matmul_norm_res(o.reshape(M, D), W["od_w_out"], P["norm_mix_post"][1], x2, TM_OUT)
    x2 = xattn_prompt(x2.reshape(B, T, D), mem_k, mem_v, 1, W["xa_wq"][1], W["xa_wo"][1],
                      P["norm_x_pre"][1], P["norm_x_post"][1], TM_ATTN).reshape(M, D)
    x2 = _channel_mix(x2, 1, P, W, TM_FFN)
    return x2.reshape(B, T, D), conv, ssm, hgrn


def _xattn_step(x2, l, mem_k, mem_v, P):
    B = x2.shape[0]
    q = norm_matmul(x2, P["norm_x_pre"][l], P["xa_wq"][l], B, TN_STEP)
    o = xattn_sample_core(q.reshape(B, X_HEADS, X_HEAD_DIM), mem_k, mem_v, l, XATTN_STEP_NB).reshape(B, D)
    return matmul_norm_res(o, P["xa_wo"][l], P["norm_x_post"][l], x2, B)


def _trunk_step(x, mem_k, mem_v, conv0, ssm0, hgrn0, P, W):
    B = x.shape[0]
    x2 = x.reshape(B, D)
    w_in = jnp.pad(P["ev_w_in"][0], ((0, 0), (0, IN0_STEP_PAD - IN0)))
    proj = norm_matmul(x2, P["norm_mix_pre"][0], w_in, B, TN_STEP)
    yab, av, conv, ssm = mixer_even_step(proj, jnp.swapaxes(conv0, 0, 1), ssm0, P, STEP_NB)
    x2 = matmul_norm_res(yab, P["ev_w_out"][0], P["norm_mix_post"][0], x2, B, TK_STEP)
    x2 = _xattn_step(x2, 0, mem_k, mem_v, P)
    x2 = ffn(x2, P["norm_ffn_pre"][0], P["norm_ffn_post"][0], W["router"],
             P["ffn_w_gate"], P["ffn_w_up"], P["ffn_w_down"], B, TF_DENSE, False, precise=True)
    proj = norm_matmul(x2, P["norm_mix_pre"][1], P["od_w_in"][0], B, TN_STEP)
    o, hgrn = hgrn_step(proj, hgrn0, W["lower_bounds"][1], P["c_norm"][0], STEP_NB)
    x2 = matmul_norm_res(o, P["od_w_out"][0], P["norm_mix_post"][1], x2, B)
    x2 = _xattn_step(x2, 1, mem_k, mem_v, P)
    x2 = _channel_mix(x2, 1, P, W, B)
    return x2.reshape(B, 1, D), jnp.swapaxes(conv, 0, 1), ssm, hgrn, av


def kernel(x_prompt, x_sample, mem_prompt, cache_mem_k, cache_mem_v, state_conv, state_ssm, state_hgrn,
           norm_mix_pre, norm_mix_post, norm_x_pre, norm_x_post, norm_ffn_pre, norm_ffn_post, norm_mem,
           xa_wq, xa_wk, xa_wv, xa_wo,
           ev_w_in, a_ws, a_bs, a_ln_g, a_ln_b, b_conv_w, b_conv_b, b_dt_bias, b_a_log, b_d, b_norm, ev_w_out,
           ffn_w_gate, ffn_w_up, ffn_w_down,
           od_w_in, hgrn_lb_logits, c_norm, od_w_out,
           moe_router, moe_w_gate, moe_w_up, moe_w_down):
    P = dict(norm_mix_pre=norm_mix_pre, norm_mix_post=norm_mix_post, norm_x_pre=norm_x_pre, norm_x_post=norm_x_post,
             norm_ffn_pre=norm_ffn_pre, norm_ffn_post=norm_ffn_post, xa_wq=xa_wq, xa_wk=xa_wk, xa_wv=xa_wv,
             xa_wo=xa_wo, ev_w_in=ev_w_in, a_ws=a_ws, a_bs=a_bs, a_ln_g=a_ln_g, a_ln_b=a_ln_b, b_conv_w=b_conv_w,
             b_conv_b=b_conv_b, b_dt_bias=b_dt_bias, b_a_log=b_a_log, b_d=b_d, b_norm=b_norm, ev_w_out=ev_w_out,
             ffn_w_gate=ffn_w_gate, ffn_w_up=ffn_w_up, ffn_w_down=ffn_w_down, od_w_in=od_w_in,
             hgrn_lb_logits=hgrn_lb_logits, c_norm=c_norm, od_w_out=od_w_out, moe_router=moe_router,
             moe_w_gate=moe_w_gate, moe_w_up=moe_w_up, moe_w_down=moe_w_down)
    W = _prep_weights(P)
    depth = norm_mem.shape[0]
    bp, T, _ = x_prompt.shape
    bs = x_sample.shape[0]

    mem_k_p, mem_v_p, mem_k_out, mem_v_out = mem_kv(mem_prompt.reshape(bp * MEM_LEN, D), norm_mem,
                                                    W["xa_wk"], W["xa_wv"], TM_PROJ)
    mem_k_p = mem_k_p.reshape(depth, bp, MEM_LEN, D)
    mem_v_p = mem_v_p.reshape(depth, bp, MEM_LEN, D)
    y_p, conv_p, ssm_p, hgrn_p = _trunk_prompt(x_prompt, mem_k_p, mem_v_p, P, W)

    y_s, conv_s, ssm_s, hgrn_s, av_s = _trunk_step(
        x_sample, cache_mem_k, cache_mem_v,
        state_conv[0], state_ssm[0].reshape(bs, B_GROUPS, B_GROUP_W, B_STATE), state_hgrn[0], P, W)

    kv_shape = (depth, bp, MEM_LEN, X_HEADS, X_HEAD_DIM)
    ssm_shape = (B_GROUPS, B_GROUP_W // B_HEAD_DIM, B_HEAD_DIM, B_STATE)
    return (y_p, y_s, mem_k_out.reshape(kv_shape), mem_v_out.reshape(kv_shape),
            conv_p[None], ssm_p.reshape((1, bp) + ssm_shape), hgrn_p[None],
            conv_s[None], ssm_s.reshape((1, bs) + ssm_shape), hgrn_s[None], av_s.reshape(1, bs, 1, D))
```
